```python
import jax, jax.numpy as jnp
from jax import lax
import numpy as np

D_MODEL = 1024
BATCH = 16
SEQ = 2048
DEPTH = 2

GRID_W = 64
CTX_LEN = 256
MIX_WIDTH = D_MODEL
NA_HEAD_DIM = 64
NA_WIDTH = MIX_WIDTH // 2
NA_HEADS = NA_WIDTH // NA_HEAD_DIM
NB_ROWS = 8
NB_COLS = 16
QB_COLS = 16
KB_COLS = QB_COLS + NB_COLS
CONV_CH = MIX_WIDTH // 4
CONV_WIDTH = 31
RET_WIDTH = MIX_WIDTH // 4
RET_HEADS = 4
RET_V_DIM = RET_WIDTH // RET_HEADS
RET_QK_DIM = RET_V_DIM // 2
RET_QK_WIDTH = RET_HEADS * RET_QK_DIM
RET_DECAY_BASE = 5.0
CHUNK = 128
IN_WIDTH = 3 * NA_WIDTH + 2 * CONV_CH + 2 * RET_QK_WIDTH + 3 * RET_WIDTH
D_FF = 2816
N_EXPERTS = 8
TOP_K = 2
D_FF_EXPERT = 3584
ROPE_BASE = 10000.0
EPS = 1e-6
NEG_INF = -1e30

kernel_name = "hybrid_na_conv_retention_moe_dit"


def _in_layout():
    names = ("na_q", "na_k", "na_v", "conv_glu", "ret_q", "ret_k", "ret_v", "ret_gf", "ret_gb")
    sizes = (NA_WIDTH, NA_WIDTH, NA_WIDTH, 2 * CONV_CH, RET_QK_WIDTH, RET_QK_WIDTH,
             RET_WIDTH, RET_WIDTH, RET_WIDTH)
    out, start = {}, 0
    for n, s in zip(names, sizes):
        out[n] = (start, s)
        start += s
    return out


def _cols(a, span):
    s, n = span
    return a[..., s:s + n]


def _heads(a, n_heads):
    b, l, _ = a.shape
    return a.reshape(b, l, n_heads, -1).transpose(0, 2, 1, 3)


def _merge(a):
    b, h, l, d = a.shape
    return a.transpose(0, 2, 1, 3).reshape(b, l, h * d)


def _flip(a):
    return a[:, :, ::-1]


def _rmsnorm(x, g):
    xf = x.astype(jnp.float32)
    y = xf * lax.rsqrt(jnp.mean(xf * xf, axis=-1, keepdims=True) + EPS)
    return (y * g.astype(jnp.float32)).astype(x.dtype)


def _layernorm(x, g, b):
    xf = x.astype(jnp.float32)
    mu = jnp.mean(xf, axis=-1, keepdims=True)
    var = jnp.mean(jnp.square(xf - mu), axis=-1, keepdims=True)
    y = (xf - mu) * lax.rsqrt(var + EPS) * g.astype(jnp.float32) + b.astype(jnp.float32)
    return y.astype(x.dtype)


def _adaln(cvec, w_mod, b_mod, n):
    m = jax.nn.silu(cvec) @ w_mod[:, :n * D_MODEL] + b_mod[:n * D_MODEL]
    return jnp.split(m[..., None, :], n, axis=-1)


def _modulate(h, g, shift, scale):
    return _rmsnorm(h, g) * (1.0 + scale) + shift


def _axial_rope(t_len, dim):
    t = jnp.arange(t_len)
    row = (t // GRID_W).astype(jnp.float32)
    col = (t % GRID_W).astype(jnp.float32)
    axis_dim = dim // 2
    inv = ROPE_BASE ** (-jnp.arange(0, axis_dim, 2, dtype=jnp.float32) / axis_dim)
    ang = jnp.concatenate([row[:, None] * inv, col[:, None] * inv], axis=-1)
    return jnp.cos(ang), jnp.sin(ang)


def _rope(x, cos, sin):
    x1, x2 = jnp.split(x, 2, axis=-1)
    return jnp.concatenate([x1 * cos - x2 * sin, x2 * cos + x1 * sin], axis=-1)


def _neighbourhood_attention(q, k, v, k_ctx, v_ctx, rpb):
    b, h, t, dh = q.shape
    rows = t // GRID_W
    wr = min(NB_ROWS, rows)
    ncb = GRID_W // QB_COLS
    r = np.arange(rows)
    key_rows = np.clip(r - wr // 2, 0, rows - wr)[:, None] + np.arange(wr)[None, :]
    qcol = np.arange(GRID_W).reshape(ncb, QB_COLS)
    kc0 = np.clip(qcol[:, 0] - NB_COLS // 2, 0, GRID_W - KB_COLS)
    key_cols = kc0[:, None] + np.arange(KB_COLS)[None, :]
    win_c0 = np.clip(qcol - NB_COLS // 2, 0, GRID_W - NB_COLS)
    kcol = key_cols[:, None, :]
    col_ok = (kcol >= win_c0[..., None]) & (kcol < win_c0[..., None] + NB_COLS)
    dr = key_rows - r[:, None] + NB_ROWS - 1
    dc = np.clip(kcol - qcol[..., None] + NB_COLS - 1, 0, 2 * NB_COLS - 2)
    bias = rpb.astype(jnp.float32)[:, dr[:, None, None, :, None], dc[None, :, :, None, :]]
    bias = jnp.where(col_ok[None, None, :, :, None, :], bias, NEG_INF)

    ridx = key_rows[:, None, :, None]
    cidx = key_cols[None, :, None, :]
    kg = k.reshape(b, h, rows, GRID_W, dh)[:, :, ridx, cidx]
    vg = v.reshape(b, h, rows, GRID_W, dh)[:, :, ridx, cidx]
    qb = q.reshape(b, h, rows, ncb, QB_COLS, dh) * (dh ** -0.5)
    s_loc = jnp.einsum("bhrnqd,bhrnwkd->bhrnqwk", qb, kg).astype(jnp.float32) + bias
    n_loc = wr * KB_COLS
    s_loc = s_loc.reshape(b, h, rows, ncb, QB_COLS, n_loc)
    s_ctx = jnp.einsum("bhrnqd,bhcd->bhrnqc", qb, k_ctx).astype(jnp.float32)
    p = jax.nn.softmax(jnp.concatenate([s_loc, s_ctx], axis=-1), axis=-1).astype(v.dtype)
    p_loc = p[..., :n_loc].reshape(b, h, rows, ncb, QB_COLS, wr, KB_COLS)
    o = (jnp.einsum("bhrnqwk,bhrnwkd->bhrnqd", p_loc, vg)
         + jnp.einsum("bhrnqc,bhcd->bhrnqd", p[..., n_loc:], v_ctx))
    return o.reshape(b, h, t, dh)


def _context_attention(q, k, v):
    s = jnp.einsum("bhid,bhjd->bhij", q * (q.shape[-1] ** -0.5), k).astype(jnp.float32)
    p = jax.nn.softmax(s, axis=-1).astype(v.dtype)
    return jnp.einsum("bhij,bhjd->bhid", p, v)


def _conv_module(u, conv_w, conv_b, ln_w, ln_b):
    a, g = jnp.split(u, 2, axis=-1)
    y = a * jax.nn.sigmoid(g)
    y = lax.conv_general_dilated(
        y, conv_w[:, None, :].astype(y.dtype), window_strides=(1,),
        padding=[(CONV_WIDTH // 2, CONV_WIDTH // 2)],
        dimension_numbers=("NWC", "WIO", "NWC"), feature_group_count=CONV_CH) + conv_b
    return jax.nn.silu(_layernorm(y, ln_w, ln_b))


def _retention_chunked(q, k, v, gamma, s0):
    b, h, t, _ = q.shape
    dv = v.shape[-1]
    n_chunks = t // CHUNK
    lg = jnp.log(gamma)[:, None]
    pos = jnp.arange(CHUNK, dtype=jnp.float32)
    diff = pos[:, None] - pos[None, :]
    intra = jnp.where(diff >= 0, jnp.exp(lg[:, :, None] * jnp.maximum(diff, 0.0)), 0.0)
    q_dec = jnp.exp(lg * (pos + 1.0))[..., None]
    k_dec = jnp.exp(lg * (CHUNK - 1.0 - pos))[..., None]
    c_dec = jnp.exp(lg[:, 0] * CHUNK)[:, None, None]

    def to_chunks(a):
        return jnp.moveaxis(a.reshape(b, h, n_chunks, CHUNK, a.shape[-1]), 2, 0)

    def step(s, xs):
        qc, kc, vc = xs
        inner = jnp.einsum("bhid,bhjd->bhij", qc, kc) * intra
        o = (jnp.einsum("bhij,bhje->bhie", inner, vc)
             + jnp.einsum("bhid,bhde->bhie", qc * q_dec, s))
        s = c_dec * s + jnp.einsum("bhjd,bhje->bhde", kc * k_dec, vc)
        return s, o

    s_fin, o = lax.scan(step, s0, (to_chunks(q), to_chunks(k), to_chunks(v)))
    return jnp.moveaxis(o, 0, 2).reshape(b, h, t, dv), s_fin


def _retention_state(k, v, gamma):
    t = k.shape[2]
    w = jnp.exp(jnp.log(gamma)[:, None] * (t - 1.0 - jnp.arange(t, dtype=jnp.float32)))
    return jnp.einsum("bhtd,bhte->bhde", k * w[..., None], v)


def _head_norm(o, g):
    mu = jnp.mean(o, axis=-1, keepdims=True)
    var = jnp.mean(jnp.square(o - mu), axis=-1, keepdims=True)
    return _merge((o - mu) * lax.rsqrt(var + EPS)) * g.astype(jnp.float32)


def _bidir_retention(q, k, v, g_f, g_b, s0, gamma, gn_w):
    o_f, s_f = _retention_chunked(q, k, v, gamma[0], s0[0])
    o_b, s_b = _retention_chunked(_flip(q), _flip(k), _flip(v), gamma[1], s0[1])
    y = (jax.nn.silu(g_f.astype(jnp.float32)) * _head_norm(o_f, gn_w)
         + jax.nn.silu(g_b.astype(jnp.float32)) * _head_norm(_flip(o_b), gn_w))
    return y, (s_f, s_b)


def _mixer(ux, uc, w_in, w_out, rpb, conv_w, conv_b, conv_ln_w, conv_ln_b, ret_decay, ret_gn_w,
           cos, sin, need_ctx_out):
    f32 = jnp.float32
    lay = _in_layout()
    px = ux @ w_in
    col_x = lambda name: _cols(px, lay[name])
    if need_ctx_out:
        pc = uc @ w_in
        col_c = lambda name: _cols(pc, lay[name])
    else:
        col_c = lambda name: uc @ _cols(w_in, lay[name])

    k_na_c = _heads(col_c("na_k"), NA_HEADS)
    v_na_c = _heads(col_c("na_v"), NA_HEADS)
    y_na = _merge(_neighbourhood_attention(
        _heads(col_x("na_q"), NA_HEADS), _heads(col_x("na_k"), NA_HEADS),
        _heads(col_x("na_v"), NA_HEADS), k_na_c, v_na_c, rpb))

    y_conv = _conv_module(col_x("conv_glu"), conv_w, conv_b, conv_ln_w, conv_ln_b)

    gamma = 1.0 - jnp.exp2(-ret_decay.astype(f32))
    k_scale = RET_QK_DIM ** -0.5
    kc = _heads(col_c("ret_k"), RET_HEADS).astype(f32) * k_scale
    vc = _heads(col_c("ret_v"), RET_HEADS).astype(f32)
    if need_ctx_out:
        qc = _heads(col_c("ret_q"), RET_HEADS).astype(f32)
        zero = jnp.zeros(kc.shape[:2] + (RET_QK_DIM, RET_V_DIM), f32)
        y_ret_c, s_ctx = _bidir_retention(qc, kc, vc, col_c("ret_gf"), col_c("ret_gb"),
                                          (zero, zero), gamma, ret_gn_w)
    else:
        s_ctx = (_retention_state(kc, vc, gamma[0]),
                 _retention_state(_flip(kc), _flip(vc), gamma[1]))
    q = _rope(_heads(col_x("ret_q"), RET_HEADS).astype(f32), cos, sin)
    k = _rope(_heads(col_x("ret_k"), RET_HEADS).astype(f32), cos, sin) * k_scale
    v = _heads(col_x("ret_v"), RET_HEADS).astype(f32)
    y_ret, _ = _bidir_retention(q, k, v, col_x("ret_gf"), col_x("ret_gb"), s_ctx, gamma, ret_gn_w)

    y_lat = jnp.concatenate([y_na, y_conv, y_ret.astype(ux.dtype)], axis=-1) @ w_out
    if not need_ctx_out:
        return y_lat, None
    y_na_c = _merge(_context_attention(_heads(col_c("na_q"), NA_HEADS), k_na_c, v_na_c))
    y_conv_c = _conv_module(col_c("conv_glu"), conv_w, conv_b, conv_ln_w, conv_ln_b)
    y_ctx = jnp.concatenate([y_na_c, y_conv_c, y_ret_c.astype(uc.dtype)], axis=-1) @ w_out
    return y_lat, y_ctx


def _swiglu(h, w_gate, w_up, w_down):
    return (jax.nn.silu(h @ w_gate) * (h @ w_up)) @ w_down


def _moe(h, router_w, router_b, w_gate, w_up, w_down):
    logits = (h @ router_w).astype(jnp.float32) + router_b.astype(jnp.float32)
    top_v, top_i = lax.top_k(logits, TOP_K)
    top_p = jax.nn.softmax(top_v, axis=-1)
    combine = jnp.sum(jax.nn.one_hot(top_i, N_EXPERTS, dtype=jnp.float32) * top_p[..., None],
                      axis=-2).astype(h.dtype)
    y = jnp.zeros_like(h)
    for e in range(N_EXPERTS):
        y = y + combine[..., e:e + 1] * _swiglu(h, w_gate[e], w_up[e], w_down[e])
    return y


def setup_inputs(seed: int = 0) -> dict:
    key = jax.random.key(seed)
    ks = iter(jax.random.split(key, 32))
    f32 = jnp.float32
    nrm = lambda shape, s: jax.random.normal(next(ks), shape, f32) * s
    n_dense = (DEPTH + 1) // 2
    n_moe = DEPTH // 2
    return {
        "x": nrm((BATCH, SEQ, D_MODEL), 1.0),
        "c": nrm((BATCH, D_MODEL), 1.0),
        "ctx": nrm((BATCH, CTX_LEN, D_MODEL), 1.0),
        "c_ctx": nrm((D_MODEL,), 1.0),
        "w_mod": nrm((DEPTH, D_MODEL, 6 * D_MODEL), 0.5 * D_MODEL ** -0.5),
        "b_mod": nrm((DEPTH, 6 * D_MODEL), 0.02),
        "norm1_w": 1.0 + nrm((DEPTH, D_MODEL), 0.05),
        "norm2_w": 1.0 + nrm((DEPTH, D_MODEL), 0.05),
        "w_in": nrm((DEPTH, D_MODEL, IN_WIDTH), D_MODEL ** -0.5),
        "w_out": nrm((DEPTH, MIX_WIDTH, D_MODEL), MIX_WIDTH ** -0.5),
        "na_rpb": nrm((DEPTH, NA_HEADS, 2 * NB_ROWS - 1, 2 * NB_COLS - 1), 0.2),
        "conv_w": nrm((DEPTH, CONV_WIDTH, CONV_CH), CONV_WIDTH ** -0.5),
        "conv_b": nrm((DEPTH, CONV_CH), 0.02),
        "conv_ln_w": 1.0 + nrm((DEPTH, CONV_CH), 0.05),
        "conv_ln_b": nrm((DEPTH, CONV_CH), 0.02),
        "ret_decay": RET_DECAY_BASE + jnp.arange(RET_HEADS, dtype=f32) + nrm((DEPTH, 2, RET_HEADS), 0.1),
        "ret_gn_w": 1.0 + nrm((DEPTH, RET_WIDTH), 0.05),
        "ffn_w_gate": nrm((n_dense, D_MODEL, D_FF), D_MODEL ** -0.5),
        "ffn_w_up": nrm((n_dense, D_MODEL, D_FF), D_MODEL ** -0.5),
        "ffn_w_down": nrm((n_dense, D_FF, D_MODEL), D_FF ** -0.5),
        "moe_router": nrm((n_moe, D_MODEL, N_EXPERTS), D_MODEL ** -0.5),
        "moe_router_b": nrm((n_moe, N_EXPERTS), 0.01),
        "moe_w_gate": nrm((n_moe, N_EXPERTS, D_MODEL, D_FF_EXPERT), D_MODEL ** -0.5),
        "moe_w_up": nrm((n_moe, N_EXPERTS, D_MODEL, D_FF_EXPERT), D_MODEL ** -0.5),
        "moe_w_down": nrm((n_moe, N_EXPERTS, D_FF_EXPERT, D_MODEL), D_FF_EXPERT ** -0.5),
        "final_norm_w": 1.0 + nrm((D_MODEL,), 0.05),
    }


def reference(x, c, ctx, c_ctx, w_mod, b_mod, norm1_w, norm2_w, w_in, w_out, na_rpb, conv_w, conv_b,
              conv_ln_w, conv_ln_b, ret_decay, ret_gn_w, ffn_w_gate, ffn_w_up, ffn_w_down,
              moe_router, moe_router_b, moe_w_gate, moe_w_up, moe_w_down, final_norm_w):
    n_ctx = ctx.shape[1]
    cos, sin = _axial_rope(x.shape[1], RET_QK_DIM)
    h_lat, h_ctx = x, ctx
    for l in range(DEPTH):
        last = l == DEPTH - 1
        sh1, sc1, g1, sh2, sc2, g2 = _adaln(c, w_mod[l], b_mod[l], 6)
        if last:
            csh1, csc1 = _adaln(c_ctx, w_mod[l], b_mod[l], 2)
        else:
            csh1, csc1, cg1, csh2, csc2, cg2 = _adaln(c_ctx, w_mod[l], b_mod[l], 6)
        ux = _modulate(h_lat, norm1_w[l], sh1, sc1)
        uc = _modulate(h_ctx, norm1_w[l], csh1, csc1)
        y_lat, y_ctx = _mixer(ux, uc, w_in[l], w_out[l], na_rpb[l], conv_w[l], conv_b[l],
                              conv_ln_w[l], conv_ln_b[l], ret_decay[l], ret_gn_w[l],
                              cos, sin, not last)
        h_lat = h_lat + g1 * y_lat
        if last:
            u = _modulate(h_lat, norm2_w[l], sh2, sc2)
        else:
            h_ctx = h_ctx + cg1 * y_ctx
            u = jnp.concatenate([_modulate(h_ctx, norm2_w[l], csh2, csc2),
                                 _modulate(h_lat, norm2_w[l], sh2, sc2)], axis=1)
        if l % 2 == 0:
            f = _swiglu(u, ffn_w_gate[l // 2], ffn_w_up[l // 2], ffn_w_down[l // 2])
        else:
            f = _moe(u, moe_router[l // 2], moe_router_b[l // 2], moe_w_gate[l // 2],
                     moe_w_up[l // 2], moe_w_down[l // 2])
        if last:
            h_lat = h_lat + g2 * f
        else:
            h_ctx = h_ctx + cg2 * f[:, :n_ctx]
            h_lat = h_lat + g2 * f[:, n_ctx:]
    return _rmsnorm(h_lat, final_norm_w)
```

```python
import functools

import numpy as np
import jax
import jax.numpy as jnp
from jax import lax
from jax.experimental import pallas as pl
from jax.experimental.pallas import tpu as pltpu

D_MODEL = 1024
DEPTH = 2
GRID_W = 64
NA_HEAD_DIM = 64
NA_WIDTH = 512
NA_HEADS = 8
NB_ROWS = 8
NB_COLS = 16
CONV_CH = 256
CONV_WIDTH = 31
RET_WIDTH = 256
RET_HEADS = 4
RET_V_DIM = 64
RET_QK_DIM = 32
RET_QK_WIDTH = 128
D_FF = 2816
N_EXPERTS = 8
D_FF_EXPERT = 3584
ROPE_BASE = 10000.0
EPS = 1e-6
NEG_INF = -1e30

A_WIDTH = 3 * NA_WIDTH + RET_WIDTH
B_WIDTH = 2 * CONV_CH + 2 * RET_QK_WIDTH + 2 * RET_WIDTH

HEAD_GROUP = 4
HG_LANES = HEAD_GROUP * NA_HEAD_DIM
Q_ROWS = 4
Q_TILE = Q_ROWS * GRID_W
K_ROWS = 12
K_TILE = K_ROWS * GRID_W
RET_CHUNK = 256
CONV_CHUNK = 128
CONV_PAD = 16
MOE_TM = 512
MOE_FF_CHUNK = 512
ROUTE_LANES = 128
VMEM_LIMIT = 56 * 1024 * 1024

_f32 = jnp.float32
_bf16 = jnp.bfloat16


def _cparams(sem, vmem=None):
    return pltpu.CompilerParams(dimension_semantics=sem, vmem_limit_bytes=vmem)


def _resident(shape):
    return pl.BlockSpec(shape, lambda *_: (0,) * len(shape), pipeline_mode=pl.Buffered(1))


def _sigmoid(x):
    return 1.0 / (1.0 + jnp.exp(-x))


def _silu(x):
    return x * _sigmoid(x)


def _dot(a, b):
    return jnp.dot(a, b, preferred_element_type=_f32)


def _dot_nt(a, b):
    return lax.dot_general(a, b, (((1,), (1,)), ((), ())), preferred_element_type=_f32)


def _dot_tn(a, b):
    return lax.dot_general(a, b, (((0,), (0,)), ((), ())), preferred_element_type=_f32)


def _split_bf16(a):
    hi = a.astype(_bf16)
    lo = (a - hi.astype(_f32)).astype(_bf16)
    return hi, lo


def _dot_split(a, b):
    ah, al = _split_bf16(a)
    bh, bl = _split_bf16(b)
    return _dot(ah, bh) + _dot(al, bh) + _dot(ah, bl)


def _rms_modulate(x, g, shift, scale):
    y = x * lax.rsqrt(jnp.mean(x * x, axis=-1, keepdims=True) + EPS)
    return (y * g) * (1.0 + scale) + shift


MOD_ROWS = 24
MOD_TN = 1536


def _mod_kernel(c_ref, w_ref, b_ref, o_ref):
    s = _silu(c_ref[...])
    o_ref[0] = _dot_split(s, w_ref[0]) + b_ref[0]


def _mod_vectors(cvecs, w_mod, b_mod):
    n = w_mod.shape[2]
    return pl.pallas_call(
        _mod_kernel,
        grid=(DEPTH, n // MOD_TN),
        in_specs=[
            pl.BlockSpec((MOD_ROWS, D_MODEL), lambda l, j: (0, 0)),
            pl.BlockSpec((1, D_MODEL, MOD_TN), lambda l, j: (l, 0, j)),
            pl.BlockSpec((1, 1, MOD_TN), lambda l, j: (l, 0, j)),
        ],
        out_specs=pl.BlockSpec((1, MOD_ROWS, MOD_TN), lambda l, j: (l, 0, j)),
        out_shape=jax.ShapeDtypeStruct((DEPTH, MOD_ROWS, n), _f32),
        compiler_params=_cparams(("arbitrary", "arbitrary"), VMEM_LIMIT),
        name="mod_vectors",
    )(cvecs, w_mod, b_mod.reshape(DEPTH, 1, n))


def _inproj_kernel(x_ref, m_ref, g_ref, wa_ref, wb_ref, oa_ref, ob_ref):
    m = m_ref[0]
    u = _rms_modulate(x_ref[0], g_ref[...], m[0:1], m[1:2]).astype(_bf16)
    oa_ref[0] = _dot(u, wa_ref[...]).astype(_bf16)
    ob_ref[0] = _dot(u, wb_ref[...])


def _inproj(x, mods, g, wa, wb, tm):
    b, l, d = x.shape
    nb = mods.shape[0]
    mod_map = (lambda i, j: (i, 0, 0)) if nb > 1 else (lambda i, j: (0, 0, 0))
    return pl.pallas_call(
        _inproj_kernel,
        grid=(b, l // tm),
        in_specs=[
            pl.BlockSpec((1, tm, d), lambda i, j: (i, j, 0)),
            pl.BlockSpec((1, 6, d), mod_map),
            pl.BlockSpec((1, d), lambda i, j: (0, 0)),
            _resident((d, A_WIDTH)),
            _resident((d, B_WIDTH)),
        ],
        out_specs=[
            pl.BlockSpec((1, tm, A_WIDTH), lambda i, j: (i, j, 0)),
            pl.BlockSpec((1, tm, B_WIDTH), lambda i, j: (i, j, 0)),
        ],
        out_shape=[
            jax.ShapeDtypeStruct((b, l, A_WIDTH), _bf16),
            jax.ShapeDtypeStruct((b, l, B_WIDTH), _f32),
        ],
        compiler_params=_cparams(("arbitrary", "arbitrary"), VMEM_LIMIT),
        name="inproj",
    )(x, mods, g, wa, wb)


def _masked_heads_attention(q, keys, vals, biases, lane):
    out = jnp.zeros((q.shape[0], HG_LANES), _f32)
    for h in range(HEAD_GROUP):
        hm = (lane >= h * NA_HEAD_DIM) & (lane < (h + 1) * NA_HEAD_DIM)
        qm = jnp.where(hm, q, jnp.zeros_like(q)) * jnp.asarray(NA_HEAD_DIM ** -0.5, q.dtype)
        scores = []
        for kk, bias in zip(keys, biases):
            s = _dot_nt(qm, kk)
            if bias is not None:
                s = s + bias(h)
            scores.append(s)
        mx = scores[0].max(axis=-1, keepdims=True)
        for s in scores[1:]:
            mx = jnp.maximum(mx, s.max(axis=-1, keepdims=True))
        den = jnp.zeros_like(mx)
        o = jnp.zeros((q.shape[0], HG_LANES), _f32)
        for s, vv in zip(scores, vals):
            p = jnp.exp(s - mx)
            den = den + p.sum(axis=-1, keepdims=True)
            o = o + _dot(p.astype(_bf16), vv)
        out = jnp.where(hm, o / den, out)
    return out


def _na_kernel(q_ref, k_ref, v_ref, kc_ref, vc_ref, bias_ref, o_ref):
    n_tiles = q_ref.shape[1] // Q_TILE
    rows = q_ref.shape[1] // GRID_W
    lane = lax.broadcasted_iota(jnp.int32, (1, HG_LANES), 1)
    kc = kc_ref[0]
    vc = vc_ref[0]

    def body(g, carry):
        krow0 = jnp.clip(Q_ROWS * g - NB_ROWS // 2, 0, rows - K_ROWS)
        start = pl.multiple_of(krow0 * GRID_W, GRID_W)
        cls = jnp.where(g == 0, 0, jnp.where(g == n_tiles - 1, 2, 1))
        qs = pl.multiple_of(g * Q_TILE, Q_TILE)
        q = q_ref[0, pl.ds(qs, Q_TILE), :]
        kl = k_ref[0, pl.ds(start, K_TILE), :]
        vl = v_ref[0, pl.ds(start, K_TILE), :]
        out = _masked_heads_attention(
            q, [kl, kc], [vl, vc], [lambda h: bias_ref[cls, h], None], lane)
        o_ref[0, pl.ds(qs, Q_TILE), :] = out.astype(o_ref.dtype)
        return carry

    lax.fori_loop(0, n_tiles, body, 0)


def _na_attention(a_lat, a_ctx, bias):
    b, t, _ = a_lat.shape
    lc = a_ctx.shape[1]
    ng = NA_HEADS // HEAD_GROUP
    kq, kk, kv = 0, NA_WIDTH // HG_LANES, 2 * NA_WIDTH // HG_LANES
    return pl.pallas_call(
        _na_kernel,
        grid=(ng, b),
        in_specs=[
            pl.BlockSpec((1, t, HG_LANES), lambda g, i: (i, 0, kq + g)),
            pl.BlockSpec((1, t, HG_LANES), lambda g, i: (i, 0, kk + g)),
            pl.BlockSpec((1, t, HG_LANES), lambda g, i: (i, 0, kv + g)),
            pl.BlockSpec((1, lc, HG_LANES), lambda g, i: (i, 0, kk + g)),
            pl.BlockSpec((1, lc, HG_LANES), lambda g, i: (i, 0, kv + g)),
            pl.BlockSpec((3, HEAD_GROUP, Q_TILE, K_TILE), lambda g, i: (0, g, 0, 0)),
        ],
        out_specs=pl.BlockSpec((1, t, HG_LANES), lambda g, i: (i, 0, g)),
        out_shape=jax.ShapeDtypeStruct((b, t, NA_WIDTH), _bf16),
        compiler_params=_cparams(("arbitrary", "arbitrary"), VMEM_LIMIT),
        name="na_attention",
    )(a_lat, a_lat, a_lat, a_ctx, a_ctx, bias)


def _ctx_attn_kernel(q_ref, k_ref, v_ref, o_ref):
    lane = lax.broadcasted_iota(jnp.int32, (1, HG_LANES), 1)
    out = _masked_heads_attention(q_ref[0], [k_ref[0]], [v_ref[0]], [None], lane)
    o_ref[0] = out.astype(o_ref.dtype)


def _ctx_attention(a_ctx):
    b, lc, _ = a_ctx.shape
    ng = NA_HEADS // HEAD_GROUP
    kq, kk, kv = 0, NA_WIDTH // HG_LANES, 2 * NA_WIDTH // HG_LANES
    return pl.pallas_call(
        _ctx_attn_kernel,
        grid=(ng, b),
        in_specs=[
            pl.BlockSpec((1, lc, HG_LANES), lambda g, i: (i, 0, kq + g)),
            pl.BlockSpec((1, lc, HG_LANES), lambda g, i: (i, 0, kk + g)),
            pl.BlockSpec((1, lc, HG_LANES), lambda g, i: (i, 0, kv + g)),
        ],
        out_specs=pl.BlockSpec((1, lc, HG_LANES), lambda g, i: (i, 0, g)),
        out_shape=jax.ShapeDtypeStruct((b, lc, NA_WIDTH), _bf16),
        compiler_params=_cparams(("arbitrary", "arbitrary")),
        name="ctx_attention",
    )(a_ctx, a_ctx, a_ctx)


def _na_bias_table(rpb):
    h = rpb.shape[0]
    pad = GRID_W - NB_COLS
    rp = jnp.pad(rpb.astype(_f32), ((0, 0), (0, 1), (pad, pad)))
    toep = jnp.stack([rp[:, :, GRID_W - 1 - c: 2 * GRID_W - 1 - c] for c in range(GRID_W)], axis=2)
    c = np.arange(GRID_W)
    wc0 = np.clip(c - NB_COLS // 2, 0, GRID_W - NB_COLS)
    col_ok = (c[None, :] >= wc0[:, None]) & (c[None, :] < wc0[:, None] + NB_COLS)
    toep = jnp.where(col_ok[None, None], toep, NEG_INF)
    i = np.arange(Q_ROWS)[:, None]
    j = np.arange(K_ROWS)[None, :]
    filler = 2 * NB_ROWS - 1
    sel = []
    ok = []
    for off, dr in ((np.zeros_like(i), j - i + NB_ROWS - 1),
                    (i, j - i + NB_ROWS - 1 - NB_ROWS // 2),
                    (np.full_like(i, K_ROWS - NB_ROWS), j - i + NB_ROWS - 1 - (K_ROWS - Q_ROWS))):
        valid = (j >= off) & (j < off + NB_ROWS)
        ok.append(valid)
        sel.append(np.where(valid, dr, filler))
    sel = np.stack(sel)
    ok = np.stack(ok)
    blocks = toep[:, sel]
    blocks = jnp.where(ok[None, :, :, :, None, None], blocks, NEG_INF)
    blocks = blocks.transpose(1, 0, 2, 4, 3, 5)
    return blocks.reshape(3, h, Q_TILE, K_TILE)


def _conv_kernel(u_ref, w_ref, b_ref, lnw_ref, lnb_ref, o_ref, ypad):
    l = u_ref.shape[1]
    ypad[0:CONV_PAD, :] = jnp.zeros((CONV_PAD, CONV_CH), _f32)
    ypad[CONV_PAD + l:2 * CONV_PAD + l, :] = jnp.zeros((CONV_PAD, CONV_CH), _f32)
    ypad[CONV_PAD:CONV_PAD + l, :] = u_ref[0, :, 0:CONV_CH] * _sigmoid(u_ref[0, :, CONV_CH:2 * CONV_CH])
    shift = CONV_PAD - CONV_WIDTH // 2

    def body(c, carry):
        base = pl.multiple_of(c * CONV_CHUNK, CONV_CHUNK)
        win = ypad[pl.ds(base, CONV_CHUNK + 2 * CONV_PAD), :]
        acc = jnp.zeros((CONV_CHUNK, CONV_CH), _f32)
        for j in range(CONV_WIDTH):
            acc = acc + win[j + shift:j + shift + CONV_CHUNK, :] * w_ref[j:j + 1, :]
        y = acc + b_ref[...]
        mu = jnp.mean(y, axis=-1, keepdims=True)
        yc = y - mu
        var = jnp.mean(yc * yc, axis=-1, keepdims=True)
        z = yc * lax.rsqrt(var + EPS) * lnw_ref[...] + lnb_ref[...]
        o_ref[0, pl.ds(base, CONV_CHUNK), :] = _silu(z).astype(o_ref.dtype)
        return carry

    lax.fori_loop(0, l // CONV_CHUNK, body, 0)


def _conv_module(bf, conv_w, conv_b, ln_w, ln_b):
    b, l, _ = bf.shape
    vec = lambda a: a.reshape(1, CONV_CH)
    return pl.pallas_call(
        _conv_kernel,
        grid=(b,),
        in_specs=[
            pl.BlockSpec((1, l, 2 * CONV_CH), lambda i: (i, 0, 0)),
            pl.BlockSpec((CONV_WIDTH, CONV_CH), lambda i: (0, 0)),
            pl.BlockSpec((1, CONV_CH), lambda i: (0, 0)),
            pl.BlockSpec((1, CONV_CH), lambda i: (0, 0)),
            pl.BlockSpec((1, CONV_CH), lambda i: (0, 0)),
        ],
        out_specs=pl.BlockSpec((1, l, CONV_CH), lambda i: (i, 0, 0)),
        out_shape=jax.ShapeDtypeStruct((b, l, CONV_CH), _bf16),
        scratch_shapes=[pltpu.VMEM((l + 2 * CONV_PAD, CONV_CH), _f32)],
        compiler_params=_cparams(("arbitrary",), VMEM_LIMIT),
        name="conv_module",
    )(bf, conv_w, vec(conv_b), vec(ln_w), vec(ln_b))


def _ret_kernel(lg_ref, q_ref, k_ref, v_ref, gf_ref, gb_ref, cos_ref, sin_ref, s0_ref, lgq_ref, lgv_ref,
                gnw_ref, y_ref, sfin_ref, dmat, of_s, ob_s, st_s):
    l = q_ref.shape[1]
    c = min(RET_CHUNK, l)
    nc = l // c
    half = RET_QK_WIDTH // 2
    sub = RET_QK_DIM // 2
    ii = lax.broadcasted_iota(jnp.int32, (c, c), 0)
    jj = lax.broadcasted_iota(jnp.int32, (c, c), 1)
    diff = (ii - jj).astype(_f32)
    for h in range(RET_HEADS):
        dmat[0, h] = jnp.where(diff >= 0, jnp.exp(lg_ref[0, h] * jnp.maximum(diff, 0.0)), 0.0)
        dmat[1, h] = jnp.where(diff <= 0, jnp.exp(lg_ref[1, h] * jnp.maximum(-diff, 0.0)), 0.0)
    pos = lax.broadcasted_iota(jnp.int32, (c, 1), 0).astype(_f32)
    lane_q = lax.broadcasted_iota(jnp.int32, (1, RET_QK_WIDTH), 1)
    head_q = (lane_q % half) // sub
    lane_v = lax.broadcasted_iota(jnp.int32, (1, RET_WIDTH), 1)
    head_v = lane_v // RET_V_DIM
    row_h = (lax.broadcasted_iota(jnp.int32, (RET_QK_WIDTH, RET_WIDTH), 0) % half) // sub
    col_h = lax.broadcasted_iota(jnp.int32, (RET_QK_WIDTH, RET_WIDTH), 1) // RET_V_DIM
    blockmask = row_h == col_h
    q_dec = (jnp.exp(lgq_ref[0] * (pos + 1.0)), jnp.exp(lgq_ref[1] * (c - pos)))
    k_dec = (jnp.exp(lgq_ref[0] * (c - 1.0 - pos)), jnp.exp(lgq_ref[1] * pos))
    c_dec = (jnp.exp(lgv_ref[0] * float(c)), jnp.exp(lgv_ref[1] * float(c)))
    k_scale = RET_QK_DIM ** -0.5
    st_s[...] = s0_ref[0]

    def step(n, carry):
        for d in range(2):
            cidx = n if d == 0 else nc - 1 - n
            base = pl.multiple_of(cidx * c, c)
            cs = cos_ref[pl.ds(base, c), :]
            sn = sin_ref[pl.ds(base, c), :]
            q = q_ref[0, pl.ds(base, c), :]
            k = k_ref[0, pl.ds(base, c), :]
            qr = q * cs + pltpu.roll(q, half, 1) * sn
            kr = (k * cs + pltpu.roll(k, half, 1) * sn) * k_scale
            v = v_ref[0, pl.ds(base, c), :]
            qb = qr.astype(_bf16)
            kb = kr.astype(_bf16)
            o = _dot((qr * q_dec[d]).astype(_bf16), st_s[d].astype(_bf16))
            for h in range(RET_HEADS):
                s = _dot_nt(jnp.where(head_q == h, qb, jnp.zeros_like(qb)), kb)
                inner = (s * dmat[d, h]).astype(_bf16)
                o = o + _dot(inner, jnp.where(head_v == h, v, jnp.zeros_like(v)))
            if d == 0:
                of_s[pl.ds(base, c), :] = o
            else:
                ob_s[pl.ds(base, c), :] = o
            upd = _dot_tn((kr * k_dec[d]).astype(_bf16), v)
            st_s[d] = c_dec[d] * st_s[d] + jnp.where(blockmask, upd, 0.0)
        return carry

    lax.fori_loop(0, nc, step, 0)
    sfin_ref[0] = st_s[...]

    gi = lax.broadcasted_iota(jnp.int32, (RET_WIDTH, RET_WIDTH), 0) // RET_V_DIM
    gj = lax.broadcasted_iota(jnp.int32, (RET_WIDTH, RET_WIDTH), 1) // RET_V_DIM
    gmean = jnp.where(gi == gj, 1.0 / RET_V_DIM, 0.0).astype(_bf16)

    def group_mean(a):
        hi, lo = _split_bf16(a)
        return _dot(hi, gmean) + _dot(lo, gmean)

    def head_norm(o):
        dlt = o - group_mean(o)
        var = group_mean(dlt * dlt)
        return dlt * lax.rsqrt(var + EPS) * gnw_ref[...]

    def fin(n, carry):
        base = pl.multiple_of(n * c, c)
        yf = head_norm(of_s[pl.ds(base, c), :])
        yb = head_norm(ob_s[pl.ds(base, c), :])
        y = _silu(gf_ref[0, pl.ds(base, c), :]) * yf + _silu(gb_ref[0, pl.ds(base, c), :]) * yb
        y_ref[0, pl.ds(base, c), :] = y.astype(y_ref.dtype)
        return carry

    lax.fori_loop(0, nc, fin, 0)


def _retention(a, bf, cos_t, sin_t, s0, lg, lgq, lgv, gn_w):
    b, l, _ = a.shape
    c = min(RET_CHUNK, l)
    qi = 2 * CONV_CH // RET_QK_WIDTH
    gi = (2 * CONV_CH + 2 * RET_QK_WIDTH) // RET_WIDTH
    vi = 3 * NA_WIDTH // RET_WIDTH
    return pl.pallas_call(
        _ret_kernel,
        grid=(b,),
        in_specs=[
            pl.BlockSpec(memory_space=pltpu.SMEM),
            pl.BlockSpec((1, l, RET_QK_WIDTH), lambda i: (i, 0, qi)),
            pl.BlockSpec((1, l, RET_QK_WIDTH), lambda i: (i, 0, qi + 1)),
            pl.BlockSpec((1, l, RET_WIDTH), lambda i: (i, 0, vi)),
            pl.BlockSpec((1, l, RET_WIDTH), lambda i: (i, 0, gi)),
            pl.BlockSpec((1, l, RET_WIDTH), lambda i: (i, 0, gi + 1)),
            pl.BlockSpec((l, RET_QK_WIDTH), lambda i: (0, 0)),
            pl.BlockSpec((l, RET_QK_WIDTH), lambda i: (0, 0)),
            pl.BlockSpec((1, 2, RET_QK_WIDTH, RET_WIDTH), lambda i: (i, 0, 0, 0)),
            pl.BlockSpec((2, 1, RET_QK_WIDTH), lambda i: (0, 0, 0)),
            pl.BlockSpec((2, 1, RET_WIDTH), lambda i: (0, 0, 0)),
            pl.BlockSpec((1, RET_WIDTH), lambda i: (0, 0)),
        ],
        out_specs=[
            pl.BlockSpec((1, l, RET_WIDTH), lambda i: (i, 0, 0)),
            pl.BlockSpec((1, 2, RET_QK_WIDTH, RET_WIDTH), lambda i: (i, 0, 0, 0)),
        ],
        out_shape=[
            jax.ShapeDtypeStruct((b, l, RET_WIDTH), _bf16),
            jax.ShapeDtypeStruct((b, 2, RET_QK_WIDTH, RET_WIDTH), _f32),
        ],
        scratch_shapes=[
            pltpu.VMEM((2, RET_HEADS, c, c), _f32),
            pltpu.VMEM((l, RET_WIDTH), _f32),
            pltpu.VMEM((l, RET_WIDTH), _f32),
            pltpu.VMEM((2, RET_QK_WIDTH, RET_WIDTH), _f32),
        ],
        compiler_params=_cparams(("arbitrary",), VMEM_LIMIT),
        name="retention",
    )(lg, bf, bf, a, bf, bf, cos_t, sin_t, s0, lgq, lgv, gn_w.reshape(1, RET_WIDTH))


def _rope_tables(t_len):
    t = np.arange(t_len)
    row = (t // GRID_W).astype(np.float32)
    col = (t % GRID_W).astype(np.float32)
    axis_dim = RET_QK_DIM // 2
    inv = jnp.asarray(ROPE_BASE, _f32) ** (-jnp.arange(0, axis_dim, 2, dtype=_f32) / axis_dim)
    ang = jnp.concatenate([jnp.asarray(row)[:, None] * inv, jnp.asarray(col)[:, None] * inv], axis=-1)
    cos, sin = jnp.cos(ang), jnp.sin(ang)
    cos_t = jnp.tile(cos, (1, 2 * RET_HEADS))
    sin_t = jnp.concatenate([jnp.tile(-sin, (1, RET_HEADS)), jnp.tile(sin, (1, RET_HEADS))], axis=-1)
    return cos_t, sin_t


def _top2_route(logits):
    lane = lax.broadcasted_iota(jnp.int32, logits.shape, 1).astype(_f32)
    m1 = logits.max(axis=-1, keepdims=True)
    i1 = jnp.where(logits == m1, lane, float(ROUTE_LANES)).min(axis=-1, keepdims=True)
    rest = jnp.where(lane == i1, -jnp.inf, logits)
    m2 = rest.max(axis=-1, keepdims=True)
    i2 = jnp.where(rest == m2, lane, float(ROUTE_LANES)).min(axis=-1, keepdims=True)
    e = jnp.exp(m2 - m1)
    p1 = 1.0 / (1.0 + e)
    p2 = e * p1
    return jnp.where(lane == 0.0, p1, jnp.where(lane == 1.0, p2, jnp.where(
        lane == 2.0, i1, jnp.where(lane == 3.0, i2, 0.0))))


def _outproj_kernel(with_router, *refs):
    if with_router:
        yna, yconv, yret, h_ref, w_ref, m_ref, g_ref, rw_ref, rb_ref, ho_ref, u_ref, route_ref = refs
    else:
        yna, yconv, yret, h_ref, w_ref, m_ref, g_ref, ho_ref, u_ref = refs
    y = jnp.concatenate([yna[0], yconv[0], yret[0]], axis=-1)
    m = m_ref[0]
    hn = h_ref[0] + m[2:3] * _dot(y, w_ref[...])
    ho_ref[0] = hn
    u = _rms_modulate(hn, g_ref[...], m[3:4], m[4:5])
    u_ref[0] = u.astype(u_ref.dtype)
    if with_router:
        route_ref[0] = _top2_route(_dot_split(u, rw_ref[...]) + rb_ref[...])


def _outproj(y_na, y_conv, y_ret, h, w_out, mods, g2, tm, router=None):
    b, l, d = h.shape
    nb = mods.shape[0]
    mod_map = (lambda i, j: (i, 0, 0)) if nb > 1 else (lambda i, j: (0, 0, 0))
    tok = lambda w: pl.BlockSpec((1, tm, w), lambda i, j: (i, j, 0))
    in_specs = [tok(NA_WIDTH), tok(CONV_CH), tok(RET_WIDTH), tok(d),
                _resident((d, d)),
                pl.BlockSpec((1, 6, d), mod_map),
                pl.BlockSpec((1, d), lambda i, j: (0, 0))]
    args = [y_na, y_conv, y_ret, h, w_out, mods, g2]
    out_specs = [tok(d), tok(d)]
    if router is None:
        out_shape = [jax.ShapeDtypeStruct((b, l, d), _f32), jax.ShapeDtypeStruct((b, l, d), _bf16)]
    else:
        in_specs += [pl.BlockSpec((d, ROUTE_LANES), lambda i, j: (0, 0)),
                     pl.BlockSpec((1, ROUTE_LANES), lambda i, j: (0, 0))]
        args += list(router)
        out_specs.append(tok(ROUTE_LANES))
        out_shape = [jax.ShapeDtypeStruct((b, l, d), _f32), jax.ShapeDtypeStruct((b, l, d), _f32),
                     jax.ShapeDtypeStruct((b, l, ROUTE_LANES), _f32)]
    return pl.pallas_call(
        functools.partial(_outproj_kernel, router is not None),
        grid=(b, l // tm),
        in_specs=in_specs,
        out_specs=out_specs,
        out_shape=out_shape,
        compiler_params=_cparams(("arbitrary", "arbitrary"), VMEM_LIMIT),
        name="outproj",
    )(*args)


def _ffn_kernel(u_ref, h_ref, m_ref, wg_ref, wu_ref, wd_ref, o_ref):
    u = u_ref[0]
    a = _dot(u, wg_ref[...])
    mid = (_silu(a) * _dot(u, wu_ref[...])).astype(_bf16)
    o_ref[0] = h_ref[0] + m_ref[0][5:6] * _dot(mid, wd_ref[...])


def _ffn(u, h, mods, wg, wu, wd, tm):
    b, l, d = h.shape
    nb = mods.shape[0]
    dff = wg.shape[1]
    mod_map = (lambda i, j: (i, 0, 0)) if nb > 1 else (lambda i, j: (0, 0, 0))
    tok = lambda: pl.BlockSpec((1, tm, d), lambda i, j: (i, j, 0))
    return pl.pallas_call(
        _ffn_kernel,
        grid=(b, l // tm),
        in_specs=[tok(), tok(), pl.BlockSpec((1, 6, d), mod_map),
                  _resident((d, dff)), _resident((d, dff)), _resident((dff, d))],
        out_specs=tok(),
        out_shape=jax.ShapeDtypeStruct((b, l, d), _f32),
        compiler_params=_cparams(("arbitrary", "arbitrary"), VMEM_LIMIT),
        name="ffn",
    )(u, h, mods, wg, wu, wd)


def _row_gather_start(src_hbm, idx_ref, dst, sem, n_rows):
    def body(r, carry):
        t = idx_ref[0, 0, r]
        pltpu.make_async_copy(src_hbm.at[pl.ds(t, 1), :], dst.at[pl.ds(r, 1), :], sem).start()
        return carry

    lax.fori_loop(0, n_rows, body, 0, unroll=8)


def _row_gather_wait(src_hbm, dst, sem, n_rows):
    pltpu.make_async_copy(src_hbm.at[pl.ds(0, n_rows), :], dst, sem).wait()


def _moe_kernel(te_ref, nu_ref, idx_ref, idx_next_ref, x_hbm, wg_ref, wu_ref, wd_ref, o_ref, xbuf, sem, acc):
    i = pl.program_id(0)
    n_used = nu_ref[0]
    slot = i % 2

    @pl.when(i == 0)
    def _():
        _row_gather_start(x_hbm, idx_ref, xbuf.at[0], sem.at[0], MOE_TM)

    @pl.when(i + 1 < n_used)
    def _():
        _row_gather_start(x_hbm, idx_next_ref, xbuf.at[1 - slot], sem.at[1 - slot], MOE_TM)

    @pl.when(i < n_used)
    def _():
        _row_gather_wait(x_hbm, xbuf.at[slot], sem.at[slot], MOE_TM)
        x = xbuf[slot].astype(_bf16)
        for c in range(D_FF_EXPERT // MOE_FF_CHUNK):
            cols = slice(c * MOE_FF_CHUNK, (c + 1) * MOE_FF_CHUNK)
            a = _dot(x, wg_ref[0, :, cols])
            mid = (_silu(a) * _dot(x, wu_ref[0, :, cols])).astype(_bf16)
            part = _dot(mid, wd_ref[0, cols, :])
            if c == 0:
                acc[...] = part
            else:
                acc[...] += part
        o_ref[...] = acc[...]

    @pl.when(i >= n_used)
    def _():
        o_ref[...] = jnp.zeros_like(o_ref)


def _moe_experts(tile_expert, n_used, sorted_tok, x, wg, wu, wd):
    n_tiles = tile_expert.shape[0]
    d = x.shape[1]
    idx = sorted_tok.reshape(n_tiles, 1, MOE_TM)
    w_spec = lambda shape: pl.BlockSpec(shape, lambda i, te, nu: (te[i], 0, 0), pipeline_mode=pl.Buffered(1))
    grid_spec = pltpu.PrefetchScalarGridSpec(
        num_scalar_prefetch=2,
        grid=(n_tiles,),
        in_specs=[
            pl.BlockSpec((1, 1, MOE_TM), lambda i, te, nu: (i, 0, 0), memory_space=pltpu.SMEM),
            pl.BlockSpec((1, 1, MOE_TM), lambda i, te, nu: (jnp.minimum(i + 1, n_tiles - 1), 0, 0),
                         memory_space=pltpu.SMEM),
            pl.BlockSpec(memory_space=pl.ANY),
            w_spec((1, d, D_FF_EXPERT)),
            w_spec((1, d, D_FF_EXPERT)),
            w_spec((1, D_FF_EXPERT, d)),
        ],
        out_specs=pl.BlockSpec((MOE_TM, d), lambda i, te, nu: (i, 0)),
        scratch_shapes=[
            pltpu.VMEM((2, MOE_TM, d), _f32),
            pltpu.SemaphoreType.DMA((2,)),
            pltpu.VMEM((MOE_TM, d), _f32),
        ],
    )
    return pl.pallas_call(
        _moe_kernel,
        grid_spec=grid_spec,
        out_shape=jax.ShapeDtypeStruct((n_tiles * MOE_TM, d), _f32),
        compiler_params=_cparams(("arbitrary",), VMEM_LIMIT),
        name="moe_experts",
    )(tile_expert, n_used, idx, idx, x, wg, wu, wd)


COMBINE_TM = 256


def _combine_kernel(pos_ref, pos_next_ref, route_ref, h_ref, m_ref, g_ref, y_hbm, o_ref, buf, sem):
    i = pl.program_id(0)
    n = pl.num_programs(0)
    slot = i % 2

    @pl.when(i == 0)
    def _():
        _row_gather_start(y_hbm, pos_ref, buf.at[0], sem.at[0], 2 * COMBINE_TM)

    @pl.when(i + 1 < n)
    def _():
        _row_gather_start(y_hbm, pos_next_ref, buf.at[1 - slot], sem.at[1 - slot], 2 * COMBINE_TM)

    _row_gather_wait(y_hbm, buf.at[slot], sem.at[slot], 2 * COMBINE_TM)
    r = route_ref[...]
    f = r[:, 0:1] * buf[slot, 0:COMBINE_TM, :] + r[:, 1:2] * buf[slot, COMBINE_TM:2 * COMBINE_TM, :]
    hn = h_ref[...] + m_ref[0][5:6] * f
    o_ref[...] = hn * lax.rsqrt(jnp.mean(hn * hn, axis=-1, keepdims=True) + EPS) * g_ref[...]


def _moe_combine(pos, route, h, mods, g, y_sorted, tokens_per_batch):
    n, d = h.shape
    nt = n // COMBINE_TM
    per_b = tokens_per_batch // COMBINE_TM
    idx = pos.reshape(nt, COMBINE_TM, 2).transpose(0, 2, 1).reshape(nt, 1, 2 * COMBINE_TM)
    return pl.pallas_call(
        _combine_kernel,
        grid=(nt,),
        in_specs=[
            pl.BlockSpec((1, 1, 2 * COMBINE_TM), lambda i: (i, 0, 0), memory_space=pltpu.SMEM),
            pl.BlockSpec((1, 1, 2 * COMBINE_TM), lambda i: (jnp.minimum(i + 1, nt - 1), 0, 0),
                         memory_space=pltpu.SMEM),
            pl.BlockSpec((COMBINE_TM, ROUTE_LANES), lambda i: (i, 0)),
            pl.BlockSpec((COMBINE_TM, d), lambda i: (i, 0)),
            pl.BlockSpec((1, 6, d), lambda i: (i // per_b, 0, 0)),
            pl.BlockSpec((1, d), lambda i: (0, 0)),
            pl.BlockSpec(memory_space=pl.ANY),
        ],
        out_specs=pl.BlockSpec((COMBINE_TM, d), lambda i: (i, 0)),
        out_shape=jax.ShapeDtypeStruct((n, d), _f32),
        scratch_shapes=[
            pltpu.VMEM((2, 2 * COMBINE_TM, d), _f32),
            pltpu.SemaphoreType.DMA((2,)),
        ],
        compiler_params=_cparams(("arbitrary",), VMEM_LIMIT),
        name="moe_combine",
    )(idx, idx, route, h, mods, g, y_sorted)


def _routing_plan(route):
    n = route.shape[0]
    e = route[:, 2:4].astype(jnp.int32).reshape(-1)
    onehot = (e[:, None] == jnp.arange(N_EXPERTS)[None, :]).astype(jnp.int32)
    csum = jnp.cumsum(onehot, axis=0)
    rank = jnp.sum(csum * onehot, axis=1) - 1
    counts = csum[-1]
    tiles_e = (counts + MOE_TM - 1) // MOE_TM
    tile_end = jnp.cumsum(tiles_e)
    off = (tile_end - tiles_e) * MOE_TM
    pos = jnp.sum(onehot * off[None, :], axis=1) + rank
    n_tiles = 2 * n // MOE_TM + N_EXPERTS
    sorted_tok = jnp.zeros((n_tiles * MOE_TM,), jnp.int32).at[pos].set(jnp.arange(2 * n, dtype=jnp.int32) // 2)
    n_used = tile_end[-1]
    t = jnp.minimum(jnp.arange(n_tiles), n_used - 1)
    tile_expert = jnp.sum((t[:, None] >= tile_end[None, :]).astype(jnp.int32), axis=1)
    return pos.reshape(n, 2), sorted_tok, tile_expert.astype(jnp.int32), n_used.reshape(1).astype(jnp.int32)


def _permuted_w_in(w_in_l):
    d = w_in_l.shape[0]
    c = {}
    start = 0
    for name, size in (("na_qkv", 3 * NA_WIDTH), ("conv_glu", 2 * CONV_CH), ("ret_q", RET_QK_WIDTH),
                       ("ret_k", RET_QK_WIDTH), ("ret_v", RET_WIDTH), ("ret_g", 2 * RET_WIDTH)):
        c[name] = w_in_l[:, start:start + size]
        start += size
    sub = RET_QK_DIM // 2
    halves_first = lambda w: w.reshape(d, RET_HEADS, 2, sub).transpose(0, 2, 1, 3).reshape(d, RET_QK_WIDTH)
    wa = jnp.concatenate([c["na_qkv"], c["ret_v"]], axis=1)
    wb = jnp.concatenate([c["conv_glu"], halves_first(c["ret_q"]), halves_first(c["ret_k"]), c["ret_g"]], axis=1)
    return wa.astype(_bf16), wb.astype(_bf16)


def kernel(x, c, ctx, c_ctx, w_mod, b_mod, norm1_w, norm2_w, w_in, w_out, na_rpb, conv_w, conv_b, conv_ln_w,
           conv_ln_b, ret_decay, ret_gn_w, ffn_w_gate, ffn_w_up, ffn_w_down, moe_router, moe_router_b,
           moe_w_gate, moe_w_up, moe_w_down, final_norm_w):
    b, t, d = x.shape
    lc = ctx.shape[1]
    assert d == D_MODEL and t % Q_TILE == 0 and t // GRID_W >= K_ROWS and b + 1 <= MOD_ROWS
    assert t % 512 == 0 and lc % 256 == 0 and (2 * b * t) % MOE_TM == 0

    cvecs = jnp.zeros((MOD_ROWS, d), _f32).at[:b].set(c).at[b].set(c_ctx)
    mods = _mod_vectors(cvecs, w_mod, b_mod).reshape(DEPTH, MOD_ROWS, 6, d)
    cos_lat, sin_lat = _rope_tables(t)
    cos_ctx = jnp.ones((lc, RET_QK_WIDTH), _f32)
    sin_ctx = jnp.zeros((lc, RET_QK_WIDTH), _f32)
    vec = lambda a: a.reshape(1, -1)
    lat_tm = 512
    ctx_tm = 256

    h_lat, h_ctx = x, ctx
    out = None
    for l in range(DEPTH):
        last = l == DEPTH - 1
        m_lat = mods[l, :b]
        m_ctx = mods[l, b:b + 1]
        wa, wb = _permuted_w_in(w_in[l])
        wo = w_out[l].astype(_bf16)
        gamma = 1.0 - jnp.exp2(-ret_decay[l].astype(_f32))
        lg = jnp.log(gamma)
        lgq = jnp.tile(jnp.repeat(lg, RET_QK_DIM // 2, axis=1), (1, 2)).reshape(2, 1, RET_QK_WIDTH)
        lgv = jnp.repeat(lg, RET_V_DIM, axis=1).reshape(2, 1, RET_WIDTH)

        a_lat, b_lat = _inproj(h_lat, m_lat, vec(norm1_w[l]), wa, wb, lat_tm)
        a_ctx, b_ctx = _inproj(h_ctx, m_ctx, vec(norm1_w[l]), wa, wb, ctx_tm)

        y_na = _na_attention(a_lat, a_ctx, _na_bias_table(na_rpb[l]))
        y_conv = _conv_module(b_lat, conv_w[l], conv_b[l], conv_ln_w[l], conv_ln_b[l])
        s_zero = jnp.zeros((b, 2, RET_QK_WIDTH, RET_WIDTH), _f32)
        y_ret_c, s_ctx = _retention(a_ctx, b_ctx, cos_ctx, sin_ctx, s_zero, lg, lgq, lgv, ret_gn_w[l])
        y_ret, _ = _retention(a_lat, b_lat, cos_lat, sin_lat, s_ctx, lg, lgq, lgv, ret_gn_w[l])

        if not last:
            y_na_c = _ctx_attention(a_ctx)
            y_conv_c = _conv_module(b_ctx, conv_w[l], conv_b[l], conv_ln_w[l], conv_ln_b[l])
            h_ctx, u_ctx = _outproj(y_na_c, y_conv_c, y_ret_c, h_ctx, wo, m_ctx, vec(norm2_w[l]), ctx_tm)

        if l % 2 == 0:
            h_lat, u_lat = _outproj(y_na, y_conv, y_ret, h_lat, wo, m_lat, vec(norm2_w[l]), lat_tm)
            wg = ffn_w_gate[l // 2].astype(_bf16)
            wu = ffn_w_up[l // 2].astype(_bf16)
            wd = ffn_w_down[l // 2].astype(_bf16)
            h_lat = _ffn(u_lat, h_lat, m_lat, wg, wu, wd, lat_tm)
            if not last:
                h_ctx = _ffn(u_ctx, h_ctx, m_ctx, wg, wu, wd, ctx_tm)
            out = h_lat
        else:
            assert last
            rw = jnp.zeros((d, ROUTE_LANES), _f32).at[:, :N_EXPERTS].set(moe_router[l // 2])
            rb = jnp.full((1, ROUTE_LANES), NEG_INF, _f32).at[0, :N_EXPERTS].set(moe_router_b[l // 2])
            h_lat, u_lat, route = _outproj(y_na, y_conv, y_ret, h_lat, wo, m_lat, vec(norm2_w[l]), lat_tm,
                                           router=(rw, rb))
            route = route.reshape(b * t, ROUTE_LANES)
            pos, sorted_tok, tile_expert, n_used = _routing_plan(route)
            y_sorted = _moe_experts(tile_expert, n_used, sorted_tok, u_lat.reshape(b * t, d),
                                    moe_w_gate[l // 2].astype(_bf16), moe_w_up[l // 2].astype(_bf16),
                                    moe_w_down[l // 2].astype(_bf16))
            out = _moe_combine(pos, route, h_lat.reshape(b * t, d), m_lat, vec(final_norm_w), y_sorted, t)
            out = out.reshape(b, t, d)
    return out
```

```python
import functools

import numpy as np
import jax
import jax.numpy as jnp
from jax import lax
from jax.experimental import pallas as pl
from jax.experimental.pallas import tpu as pltpu

D_MODEL = 1024
DEPTH = 2
GRID_W = 64
NA_HEAD_DIM = 64
NA_WIDTH = 512
NA_HEADS = 8
NB_ROWS = 8
NB_COLS = 16
CONV_CH = 256
CONV_WIDTH = 31
RET_WIDTH = 256
RET_HEADS = 4
RET_V_DIM = 64
RET_QK_DIM = 32
RET_QK_WIDTH = 128
D_FF = 2816
N_EXPERTS = 8
D_FF_EXPERT = 3584
ROPE_BASE = 10000.0
EPS = 1e-6
NEG_INF = -1e30

A_WIDTH = 3 * NA_WIDTH + RET_WIDTH
B_WIDTH = 2 * CONV_CH + 2 * RET_QK_WIDTH + 2 * RET_WIDTH

HEAD_GROUP = 4
HG_LANES = HEAD_GROUP * NA_HEAD_DIM
Q_ROWS = 4
Q_TILE = Q_ROWS * GRID_W
K_ROWS = 12
K_TILE = K_ROWS * GRID_W
RET_CHUNK = 256
CONV_CHUNK = 128
CONV_PAD = 16
SUBLANES = 8
MOE_TM = 512
MOE_FF_CHUNK = 512
ROW_BLK = SUBLANES
DISP_R = 512
DISP_ROWS = 2 * DISP_R + N_EXPERTS * ROW_BLK
ROUTE_LANES = 128
VMEM_LIMIT = 56 * 1024 * 1024

_f32 = jnp.float32
_bf16 = jnp.bfloat16


def _cparams(sem, vmem=None):
    return pltpu.CompilerParams(dimension_semantics=sem, vmem_limit_bytes=vmem)


def _resident(shape):
    return pl.BlockSpec(shape, lambda *_: (0,) * len(shape), pipeline_mode=pl.Buffered(1))


def _sigmoid(x):
    return 1.0 / (1.0 + jnp.exp(-x))


def _silu(x):
    return x * _sigmoid(x)


def _dot(a, b):
    return jnp.dot(a, b, preferred_element_type=_f32)


def _dot_nt(a, b):
    return lax.dot_general(a, b, (((1,), (1,)), ((), ())), preferred_element_type=_f32)


def _dot_tn(a, b):
    return lax.dot_general(a, b, (((0,), (0,)), ((), ())), preferred_element_type=_f32)


def _split_bf16(a):
    hi = a.astype(_bf16)
    lo = (a - hi.astype(_f32)).astype(_bf16)
    return hi, lo


def _dot_split(a, b):
    ah, al = _split_bf16(a)
    bh, bl = _split_bf16(b)
    return _dot(ah, bh) + _dot(al, bh) + _dot(ah, bl)


def _rms_modulate(x, g, shift, scale):
    y = x * lax.rsqrt(jnp.mean(x * x, axis=-1, keepdims=True) + EPS)
    return (y * g) * (1.0 + scale) + shift


MOD_ROWS = 24
MOD_TN = 1536


def _mod_kernel(c_ref, w_ref, b_ref, o_ref):
    s = _silu(c_ref[...])
    o_ref[0] = _dot_split(s, w_ref[0]) + b_ref[0]


def _mod_vectors(cvecs, w_mod, b_mod):
    n = w_mod.shape[2]
    return pl.pallas_call(
        _mod_kernel,
        grid=(DEPTH, n // MOD_TN),
        in_specs=[
            pl.BlockSpec((MOD_ROWS, D_MODEL), lambda l, j: (0, 0)),
            pl.BlockSpec((1, D_MODEL, MOD_TN), lambda l, j: (l, 0, j)),
            pl.BlockSpec((1, 1, MOD_TN), lambda l, j: (l, 0, j)),
        ],
        out_specs=pl.BlockSpec((1, MOD_ROWS, MOD_TN), lambda l, j: (l, 0, j)),
        out_shape=jax.ShapeDtypeStruct((DEPTH, MOD_ROWS, n), _f32),
        compiler_params=_cparams(("arbitrary", "arbitrary"), VMEM_LIMIT),
        name="mod_vectors",
    )(cvecs, w_mod, b_mod.reshape(DEPTH, 1, n))


def _inproj_kernel(x_ref, m_ref, g_ref, wa_ref, wb_ref, oa_ref, ob_ref):
    m = m_ref[0]
    u = _rms_modulate(x_ref[0], g_ref[...], m[0:1], m[1:2]).astype(_bf16)
    oa_ref[0] = _dot(u, wa_ref[...]).astype(_bf16)
    ob_ref[0] = _dot(u, wb_ref[...])


def _inproj(x, mods, g, wa, wb, tm):
    b, l, d = x.shape
    nb = mods.shape[0]
    mod_map = (lambda i, j: (i, 0, 0)) if nb > 1 else (lambda i, j: (0, 0, 0))
    return pl.pallas_call(
        _inproj_kernel,
        grid=(b, l // tm),
        in_specs=[
            pl.BlockSpec((1, tm, d), lambda i, j: (i, j, 0)),
            pl.BlockSpec((1, 6, d), mod_map),
            pl.BlockSpec((1, d), lambda i, j: (0, 0)),
            _resident((d, A_WIDTH)),
            _resident((d, B_WIDTH)),
        ],
        out_specs=[
            pl.BlockSpec((1, tm, A_WIDTH), lambda i, j: (i, j, 0)),
            pl.BlockSpec((1, tm, B_WIDTH), lambda i, j: (i, j, 0)),
        ],
        out_shape=[
            jax.ShapeDtypeStruct((b, l, A_WIDTH), _bf16),
            jax.ShapeDtypeStruct((b, l, B_WIDTH), _f32),
        ],
        compiler_params=_cparams(("arbitrary", "arbitrary"), VMEM_LIMIT),
        name="inproj",
    )(x, mods, g, wa, wb)


def _masked_heads_attention(q, keys, vals, n_biased, bias, lane):
    out = jnp.zeros((q.shape[0], HG_LANES), _f32)
    for h in range(HEAD_GROUP):
        hm = (lane >= h * NA_HEAD_DIM) & (lane < (h + 1) * NA_HEAD_DIM)
        qm = jnp.where(hm, q, jnp.zeros_like(q)) * jnp.asarray(NA_HEAD_DIM ** -0.5, q.dtype)
        s = _dot_nt(qm, keys)
        parts = [s[:, :n_biased] + bias(h), s[:, n_biased:]] if n_biased else [s]
        mx = parts[0].max(axis=-1, keepdims=True)
        for part in parts[1:]:
            mx = jnp.maximum(mx, part.max(axis=-1, keepdims=True))
        probs = [jnp.exp(part - mx) for part in parts]
        den = probs[0].sum(axis=-1, keepdims=True)
        for p in probs[1:]:
            den = den + p.sum(axis=-1, keepdims=True)
        p = jnp.concatenate([p.astype(_bf16) for p in probs], axis=-1)
        out = jnp.where(hm, _dot(p, vals) / den, out)
    return out


def _na_kernel(q_ref, k_ref, v_ref, kc_ref, vc_ref, bias_ref, o_ref, kall, vall):
    n_tiles = q_ref.shape[1] // Q_TILE
    rows = q_ref.shape[1] // GRID_W
    lane = lax.broadcasted_iota(jnp.int32, (1, HG_LANES), 1)
    kall[K_TILE:, :] = kc_ref[0]
    vall[K_TILE:, :] = vc_ref[0]

    def body(g, carry):
        krow0 = jnp.clip(Q_ROWS * g - NB_ROWS // 2, 0, rows - K_ROWS)
        start = pl.multiple_of(krow0 * GRID_W, GRID_W)
        cls = jnp.where(g == 0, 0, jnp.where(g == n_tiles - 1, 2, 1))
        qs = pl.multiple_of(g * Q_TILE, Q_TILE)
        q = q_ref[0, pl.ds(qs, Q_TILE), :]
        kall[0:K_TILE, :] = k_ref[0, pl.ds(start, K_TILE), :]
        vall[0:K_TILE, :] = v_ref[0, pl.ds(start, K_TILE), :]
        out = _masked_heads_attention(q, kall[...], vall[...], K_TILE, lambda h: bias_ref[cls, h], lane)
        o_ref[0, pl.ds(qs, Q_TILE), :] = out.astype(o_ref.dtype)
        return carry

    lax.fori_loop(0, n_tiles, body, 0)


def _na_attention(a_lat, a_ctx, bias):
    b, t, _ = a_lat.shape
    lc = a_ctx.shape[1]
    ng = NA_HEADS // HEAD_GROUP
    kq, kk, kv = 0, NA_WIDTH // HG_LANES, 2 * NA_WIDTH // HG_LANES
    return pl.pallas_call(
        _na_kernel,
        grid=(ng, b),
        in_specs=[
            pl.BlockSpec((1, t, HG_LANES), lambda g, i: (i, 0, kq + g)),
            pl.BlockSpec((1, t, HG_LANES), lambda g, i: (i, 0, kk + g)),
            pl.BlockSpec((1, t, HG_LANES), lambda g, i: (i, 0, kv + g)),
            pl.BlockSpec((1, lc, HG_LANES), lambda g, i: (i, 0, kk + g)),
            pl.BlockSpec((1, lc, HG_LANES), lambda g, i: (i, 0, kv + g)),
            pl.BlockSpec((3, HEAD_GROUP, Q_TILE, K_TILE), lambda g, i: (0, g, 0, 0)),
        ],
        out_specs=pl.BlockSpec((1, t, HG_LANES), lambda g, i: (i, 0, g)),
        out_shape=jax.ShapeDtypeStruct((b, t, NA_WIDTH), _bf16),
        scratch_shapes=[pltpu.VMEM((K_TILE + lc, HG_LANES), _bf16),
                        pltpu.VMEM((K_TILE + lc, HG_LANES), _bf16)],
        compiler_params=_cparams(("arbitrary", "arbitrary"), VMEM_LIMIT),
        name="na_attention",
    )(a_lat, a_lat, a_lat, a_ctx, a_ctx, bias)


def _ctx_attn_kernel(q_ref, k_ref, v_ref, o_ref):
    lane = lax.broadcasted_iota(jnp.int32, (1, HG_LANES), 1)
    out = _masked_heads_attention(q_ref[0], k_ref[0], v_ref[0], 0, None, lane)
    o_ref[0] = out.astype(o_ref.dtype)


def _ctx_attention(a_ctx):
    b, lc, _ = a_ctx.shape
    ng = NA_HEADS // HEAD_GROUP
    kq, kk, kv = 0, NA_WIDTH // HG_LANES, 2 * NA_WIDTH // HG_LANES
    return pl.pallas_call(
        _ctx_attn_kernel,
        grid=(ng, b),
        in_specs=[
            pl.BlockSpec((1, lc, HG_LANES), lambda g, i: (i, 0, kq + g)),
            pl.BlockSpec((1, lc, HG_LANES), lambda g, i: (i, 0, kk + g)),
            pl.BlockSpec((1, lc, HG_LANES), lambda g, i: (i, 0, kv + g)),
        ],
        out_specs=pl.BlockSpec((1, lc, HG_LANES), lambda g, i: (i, 0, g)),
        out_shape=jax.ShapeDtypeStruct((b, lc, NA_WIDTH), _bf16),
        compiler_params=_cparams(("arbitrary", "arbitrary")),
        name="ctx_attention",
    )(a_ctx, a_ctx, a_ctx)


def _na_bias_table(rpb):
    h = rpb.shape[0]
    pad = GRID_W - NB_COLS
    rp = jnp.pad(rpb.astype(_f32), ((0, 0), (0, 1), (pad, pad)))
    toep = jnp.stack([rp[:, :, GRID_W - 1 - c: 2 * GRID_W - 1 - c] for c in range(GRID_W)], axis=2)
    c = np.arange(GRID_W)
    wc0 = np.clip(c - NB_COLS // 2, 0, GRID_W - NB_COLS)
    col_ok = (c[None, :] >= wc0[:, None]) & (c[None, :] < wc0[:, None] + NB_COLS)
    toep = jnp.where(col_ok[None, None], toep, NEG_INF)
    i = np.arange(Q_ROWS)[:, None]
    j = np.arange(K_ROWS)[None, :]
    filler = 2 * NB_ROWS - 1
    sel = []
    ok = []
    for off, dr in ((np.zeros_like(i), j - i + NB_ROWS - 1),
                    (i, j - i + NB_ROWS - 1 - NB_ROWS // 2),
                    (np.full_like(i, K_ROWS - NB_ROWS), j - i + NB_ROWS - 1 - (K_ROWS - Q_ROWS))):
        valid = (j >= off) & (j < off + NB_ROWS)
        ok.append(valid)
        sel.append(np.where(valid, dr, filler))
    sel = np.stack(sel)
    ok = np.stack(ok)
    blocks = toep[:, sel]
    blocks = jnp.where(ok[None, :, :, :, None, None], blocks, NEG_INF)
    blocks = blocks.transpose(1, 0, 2, 4, 3, 5)
    return blocks.reshape(3, h, Q_TILE, K_TILE)


def _conv_kernel(u_ref, w_ref, b_ref, lnw_ref, lnb_ref, o_ref, ypad):
    l = u_ref.shape[1]
    ypad[0:CONV_PAD, :] = jnp.zeros((CONV_PAD, CONV_CH), _f32)
    ypad[CONV_PAD + l:2 * CONV_PAD + l, :] = jnp.zeros((CONV_PAD, CONV_CH), _f32)
    ypad[CONV_PAD:CONV_PAD + l, :] = u_ref[0, :, 0:CONV_CH] * _sigmoid(u_ref[0, :, CONV_CH:2 * CONV_CH])
    shift = CONV_PAD - CONV_WIDTH // 2

    def body(c, carry):
        base = pl.multiple_of(c * CONV_CHUNK, CONV_CHUNK)
        win = ypad[pl.ds(base, CONV_CHUNK + 2 * CONV_PAD), :]
        acc = jnp.zeros((CONV_CHUNK, CONV_CH), _f32)
        for r in range(SUBLANES):
            offs = [o for o in range(shift, shift + CONV_WIDTH) if o % SUBLANES == r]
            wr = win if r == 0 else pltpu.roll(win, win.shape[0] - r, 0)
            for o in offs:
                acc = acc + wr[o - r:o - r + CONV_CHUNK, :] * w_ref[o - shift:o - shift + 1, :]
        y = acc + b_ref[...]
        mu = jnp.mean(y, axis=-1, keepdims=True)
        yc = y - mu
        var = jnp.mean(yc * yc, axis=-1, keepdims=True)
        z = yc * lax.rsqrt(var + EPS) * lnw_ref[...] + lnb_ref[...]
        o_ref[0, pl.ds(base, CONV_CHUNK), :] = _silu(z).astype(o_ref.dtype)
        return carry

    lax.fori_loop(0, l // CONV_CHUNK, body, 0)


def _conv_module(bf, conv_w, conv_b, ln_w, ln_b):
    b, l, _ = bf.shape
    vec = lambda a: a.reshape(1, CONV_CH)
    return pl.pallas_call(
        _conv_kernel,
        grid=(b,),
        in_specs=[
            pl.BlockSpec((1, l, 2 * CONV_CH), lambda i: (i, 0, 0)),
            pl.BlockSpec((CONV_WIDTH, CONV_CH), lambda i: (0, 0)),
            pl.BlockSpec((1, CONV_CH), lambda i: (0, 0)),
            pl.BlockSpec((1, CONV_CH), lambda i: (0, 0)),
            pl.BlockSpec((1, CONV_CH), lambda i: (0, 0)),
        ],
        out_specs=pl.BlockSpec((1, l, CONV_CH), lambda i: (i, 0, 0)),
        out_shape=jax.ShapeDtypeStruct((b, l, CONV_CH), _bf16),
        scratch_shapes=[pltpu.VMEM((l + 2 * CONV_PAD, CONV_CH), _f32)],
        compiler_params=_cparams(("arbitrary",), VMEM_LIMIT),
        name="conv_module",
    )(bf, conv_w, vec(conv_b), vec(ln_w), vec(ln_b))


def _ret_kernel(lg_ref, q_ref, k_ref, v_ref, gf_ref, gb_ref, cos_ref, sin_ref, s0_ref, lgq_ref, lgv_ref,
                gnw_ref, y_ref, sfin_ref, dmat, of_s, ob_s, st_s):
    l = q_ref.shape[1]
    c = min(RET_CHUNK, l)
    nc = l // c
    half = RET_QK_WIDTH // 2
    sub = RET_QK_DIM // 2
    ii = lax.broadcasted_iota(jnp.int32, (c, c), 0)
    jj = lax.broadcasted_iota(jnp.int32, (c, c), 1)
    diff = (ii - jj).astype(_f32)
    for h in range(RET_HEADS):
        dmat[0, h] = jnp.where(diff >= 0, jnp.exp(lg_ref[0, h] * jnp.maximum(diff, 0.0)), 0.0)
        dmat[1, h] = jnp.where(diff <= 0, jnp.exp(lg_ref[1, h] * jnp.maximum(-diff, 0.0)), 0.0)
    pos = lax.broadcasted_iota(jnp.int32, (c, 1), 0).astype(_f32)
    lane_q = lax.broadcasted_iota(jnp.int32, (1, RET_QK_WIDTH), 1)
    head_q = (lane_q % half) // sub
    lane_v = lax.broadcasted_iota(jnp.int32, (1, RET_WIDTH), 1)
    head_v = lane_v // RET_V_DIM
    row_h = (lax.broadcasted_iota(jnp.int32, (RET_QK_WIDTH, RET_WIDTH), 0) % half) // sub
    col_h = lax.broadcasted_iota(jnp.int32, (RET_QK_WIDTH, RET_WIDTH), 1) // RET_V_DIM
    blockmask = row_h == col_h
    q_dec = (jnp.exp(lgq_ref[0] * (pos + 1.0)), jnp.exp(lgq_ref[1] * (c - pos)))
    k_dec = (jnp.exp(lgq_ref[0] * (c - 1.0 - pos)), jnp.exp(lgq_ref[1] * pos))
    c_dec = (jnp.exp(lgv_ref[0] * float(c)), jnp.exp(lgv_ref[1] * float(c)))
    k_scale = RET_QK_DIM ** -0.5
    st_s[...] = s0_ref[0]

    def step(n, carry):
        for d in range(2):
            cidx = n if d == 0 else nc - 1 - n
            base = pl.multiple_of(cidx * c, c)
            cs = cos_ref[pl.ds(base, c), :]
            sn = sin_ref[pl.ds(base, c), :]
            q = q_ref[0, pl.ds(base, c), :]
            k = k_ref[0, pl.ds(base, c), :]
            qr = q * cs + pltpu.roll(q, half, 1) * sn
            kr = (k * cs + pltpu.roll(k, half, 1) * sn) * k_scale
            v = v_ref[0, pl.ds(base, c), :]
            qb = qr.astype(_bf16)
            kb = kr.astype(_bf16)
            o = _dot((qr * q_dec[d]).astype(_bf16), st_s[d].astype(_bf16))
            for h in range(RET_HEADS):
                s = _dot_nt(jnp.where(head_q == h, qb, jnp.zeros_like(qb)), kb)
                inner = (s * dmat[d, h]).astype(_bf16)
                o = o + _dot(inner, jnp.where(head_v == h, v, jnp.zeros_like(v)))
            if d == 0:
                of_s[pl.ds(base, c), :] = o
            else:
                ob_s[pl.ds(base, c), :] = o
            upd = _dot_tn((kr * k_dec[d]).astype(_bf16), v)
            st_s[d] = c_dec[d] * st_s[d] + jnp.where(blockmask, upd, 0.0)
        return carry

    lax.fori_loop(0, nc, step, 0)
    sfin_ref[0] = st_s[...]

    gi = lax.broadcasted_iota(jnp.int32, (RET_WIDTH, RET_WIDTH), 0) // RET_V_DIM
    gj = lax.broadcasted_iota(jnp.int32, (RET_WIDTH, RET_WIDTH), 1) // RET_V_DIM
    gmean = jnp.where(gi == gj, 1.0 / RET_V_DIM, 0.0).astype(_bf16)

    def group_mean(a):
        return _dot(a.astype(_bf16), gmean)

    def head_norm(o):
        dlt = o - group_mean(o)
        var = group_mean(dlt * dlt)
        return dlt * lax.rsqrt(var + EPS) * gnw_ref[...]

    def fin(n, carry):
        base = pl.multiple_of(n * c, c)
        yf = head_norm(of_s[pl.ds(base, c), :])
        yb = head_norm(ob_s[pl.ds(base, c), :])
        y = _silu(gf_ref[0, pl.ds(base, c), :]) * yf + _silu(gb_ref[0, pl.ds(base, c), :]) * yb
        y_ref[0, pl.ds(base, c), :] = y.astype(y_ref.dtype)
        return carry

    lax.fori_loop(0, nc, fin, 0)


def _retention(a, bf, cos_t, sin_t, s0, lg, lgq, lgv, gn_w):
    b, l, _ = a.shape
    c = min(RET_CHUNK, l)
    qi = 2 * CONV_CH // RET_QK_WIDTH
    gi = (2 * CONV_CH + 2 * RET_QK_WIDTH) // RET_WIDTH
    vi = 3 * NA_WIDTH // RET_WIDTH
    return pl.pallas_call(
        _ret_kernel,
        grid=(b,),
        in_specs=[
            pl.BlockSpec(memory_space=pltpu.SMEM),
            pl.BlockSpec((1, l, RET_QK_WIDTH), lambda i: (i, 0, qi)),
            pl.BlockSpec((1, l, RET_QK_WIDTH), lambda i: (i, 0, qi + 1)),
            pl.BlockSpec((1, l, RET_WIDTH), lambda i: (i, 0, vi)),
            pl.BlockSpec((1, l, RET_WIDTH), lambda i: (i, 0, gi)),
            pl.BlockSpec((1, l, RET_WIDTH), lambda i: (i, 0, gi + 1)),
            pl.BlockSpec((l, RET_QK_WIDTH), lambda i: (0, 0)),
            pl.BlockSpec((l, RET_QK_WIDTH), lambda i: (0, 0)),
            pl.BlockSpec((1, 2, RET_QK_WIDTH, RET_WIDTH), lambda i: (i, 0, 0, 0)),
            pl.BlockSpec((2, 1, RET_QK_WIDTH), lambda i: (0, 0, 0)),
            pl.BlockSpec((2, 1, RET_WIDTH), lambda i: (0, 0, 0)),
            pl.BlockSpec((1, RET_WIDTH), lambda i: (0, 0)),
        ],
        out_specs=[
            pl.BlockSpec((1, l, RET_WIDTH), lambda i: (i, 0, 0)),
            pl.BlockSpec((1, 2, RET_QK_WIDTH, RET_WIDTH), lambda i: (i, 0, 0, 0)),
        ],
        out_shape=[
            jax.ShapeDtypeStruct((b, l, RET_WIDTH), _bf16),
            jax.ShapeDtypeStruct((b, 2, RET_QK_WIDTH, RET_WIDTH), _f32),
        ],
        scratch_shapes=[
            pltpu.VMEM((2, RET_HEADS, c, c), _f32),
            pltpu.VMEM((l, RET_WIDTH), _f32),
            pltpu.VMEM((l, RET_WIDTH), _f32),
            pltpu.VMEM((2, RET_QK_WIDTH, RET_WIDTH), _f32),
        ],
        compiler_params=_cparams(("arbitrary",), VMEM_LIMIT),
        name="retention",
    )(lg, bf, bf, a, bf, bf, cos_t, sin_t, s0, lgq, lgv, gn_w.reshape(1, RET_WIDTH))


def _rope_tables(t_len):
    t = np.arange(t_len)
    row = (t // GRID_W).astype(np.float32)
    col = (t % GRID_W).astype(np.float32)
    axis_dim = RET_QK_DIM // 2
    inv = jnp.asarray(ROPE_BASE, _f32) ** (-jnp.arange(0, axis_dim, 2, dtype=_f32) / axis_dim)
    ang = jnp.concatenate([jnp.asarray(row)[:, None] * inv, jnp.asarray(col)[:, None] * inv], axis=-1)
    cos, sin = jnp.cos(ang), jnp.sin(ang)
    cos_t = jnp.tile(cos, (1, 2 * RET_HEADS))
    sin_t = jnp.concatenate([jnp.tile(-sin, (1, RET_HEADS)), jnp.tile(sin, (1, RET_HEADS))], axis=-1)
    return cos_t, sin_t


def _top2_route(logits):
    lane = lax.broadcasted_iota(jnp.int32, logits.shape, 1).astype(_f32)
    m1 = logits.max(axis=-1, keepdims=True)
    i1 = jnp.where(logits == m1, lane, float(ROUTE_LANES)).min(axis=-1, keepdims=True)
    rest = jnp.where(lane == i1, -jnp.inf, logits)
    m2 = rest.max(axis=-1, keepdims=True)
    i2 = jnp.where(rest == m2, lane, float(ROUTE_LANES)).min(axis=-1, keepdims=True)
    e = jnp.exp(m2 - m1)
    p1 = 1.0 / (1.0 + e)
    p2 = e * p1
    return jnp.where(lane == 0.0, p1, jnp.where(lane == 1.0, p2, jnp.where(
        lane == 2.0, i1, jnp.where(lane == 3.0, i2, 0.0))))


def _outproj_kernel(with_router, *refs):
    if with_router:
        yna, yconv, yret, h_ref, w_ref, m_ref, g_ref, rw_ref, rb_ref, ho_ref, u_ref, route_ref = refs
    else:
        yna, yconv, yret, h_ref, w_ref, m_ref, g_ref, ho_ref, u_ref = refs
    y = jnp.concatenate([yna[0], yconv[0], yret[0]], axis=-1)
    m = m_ref[0]
    hn = h_ref[0] + m[2:3] * _dot(y, w_ref[...])
    ho_ref[0] = hn
    u = _rms_modulate(hn, g_ref[...], m[3:4], m[4:5])
    u_ref[0] = u.astype(u_ref.dtype)
    if with_router:
        route_ref[0] = _top2_route(_dot_split(u, rw_ref[...]) + rb_ref[...])


def _outproj(y_na, y_conv, y_ret, h, w_out, mods, g2, tm, router=None):
    b, l, d = h.shape
    nb = mods.shape[0]
    mod_map = (lambda i, j: (i, 0, 0)) if nb > 1 else (lambda i, j: (0, 0, 0))
    tok = lambda w: pl.BlockSpec((1, tm, w), lambda i, j: (i, j, 0))
    in_specs = [tok(NA_WIDTH), tok(CONV_CH), tok(RET_WIDTH), tok(d),
                _resident((d, d)),
                pl.BlockSpec((1, 6, d), mod_map),
                pl.BlockSpec((1, d), lambda i, j: (0, 0))]
    args = [y_na, y_conv, y_ret, h, w_out, mods, g2]
    out_specs = [tok(d), tok(d)]
    if router is None:
        out_shape = [jax.ShapeDtypeStruct((b, l, d), _f32), jax.ShapeDtypeStruct((b, l, d), _bf16)]
    else:
        in_specs += [pl.BlockSpec((d, ROUTE_LANES), lambda i, j: (0, 0)),
                     pl.BlockSpec((1, ROUTE_LANES), lambda i, j: (0, 0))]
        args += list(router)
        out_specs.append(tok(ROUTE_LANES))
        out_shape = [jax.ShapeDtypeStruct((b, l, d), _f32), jax.ShapeDtypeStruct((b, l, d), _f32),
                     jax.ShapeDtypeStruct((b, l, ROUTE_LANES), _f32)]
    return pl.pallas_call(
        functools.partial(_outproj_kernel, router is not None),
        grid=(b, l // tm),
        in_specs=in_specs,
        out_specs=out_specs,
        out_shape=out_shape,
        compiler_params=_cparams(("arbitrary", "arbitrary"), VMEM_LIMIT),
        name="outproj",
    )(*args)


def _ffn_kernel(u_ref, h_ref, m_ref, wg_ref, wu_ref, wd_ref, o_ref):
    u = u_ref[0]
    a = _dot(u, wg_ref[...])
    mid = (_silu(a) * _dot(u, wu_ref[...])).astype(_bf16)
    o_ref[0] = h_ref[0] + m_ref[0][5:6] * _dot(mid, wd_ref[...])


def _ffn(u, h, mods, wg, wu, wd, tm):
    b, l, d = h.shape
    nb = mods.shape[0]
    dff = wg.shape[1]
    mod_map = (lambda i, j: (i, 0, 0)) if nb > 1 else (lambda i, j: (0, 0, 0))
    tok = lambda: pl.BlockSpec((1, tm, d), lambda i, j: (i, j, 0))
    return pl.pallas_call(
        _ffn_kernel,
        grid=(b, l // tm),
        in_specs=[tok(), tok(), pl.BlockSpec((1, 6, d), mod_map),
                  _resident((d, dff)), _resident((d, dff)), _resident((dff, d))],
        out_specs=tok(),
        out_shape=jax.ShapeDtypeStruct((b, l, d), _f32),
        compiler_params=_cparams(("arbitrary", "arbitrary"), VMEM_LIMIT),
        name="ffn",
    )(u, h, mods, wg, wu, wd)


def _block_copies_start(nblk_ref, seg, src, src_blk_ref, dst, dst_blk_ref, sem):
    for e in range(N_EXPERTS):
        k = seg * N_EXPERTS + e
        sb = src_blk_ref[k]
        db = dst_blk_ref[k]

        def body(j, carry):
            s0 = pl.multiple_of((sb + j) * ROW_BLK, ROW_BLK)
            d0 = pl.multiple_of((db + j) * ROW_BLK, ROW_BLK)
            pltpu.make_async_copy(src.at[pl.ds(s0, ROW_BLK), :], dst.at[pl.ds(d0, ROW_BLK), :], sem).start()
            return carry

        lax.fori_loop(0, nblk_ref[k], body, 0)


def _block_copies_wait(n, src, dst, sem):
    def body(j, carry):
        pltpu.make_async_copy(src.at[pl.ds(0, ROW_BLK), :], dst.at[pl.ds(0, ROW_BLK), :], sem).wait()
        return carry

    lax.fori_loop(0, n, body, 0)


def _dispatch_kernel(gblk, nblk, sblk, totblk, tail_blk, tail_n, d_ref, u_ref, x_hbm, y, sem, zbuf, zsem):
    t = pl.program_id(0)
    nt = pl.num_programs(0)
    slot = t % 2

    @pl.when(t >= 2)
    def _():
        _block_copies_wait(totblk[t - 2], y.at[slot], x_hbm, sem.at[slot])

    rows = lax.broadcasted_iota(jnp.int32, (DISP_ROWS, DISP_R), 0)
    hit = jnp.where(rows == d_ref[0, 0:1, :], 1.0, jnp.where(rows == d_ref[0, 1:2, :], 1.0, 0.0))
    y[slot] = _dot(hit.astype(_bf16), u_ref[...].astype(_bf16))
    _block_copies_start(nblk, t, y.at[slot], sblk, x_hbm, gblk, sem.at[slot])

    @pl.when(t == nt - 1)
    def _():
        @pl.when(t >= 1)
        def _():
            _block_copies_wait(totblk[jnp.maximum(t - 1, 0)], y.at[1 - slot], x_hbm, sem.at[1 - slot])

        _block_copies_wait(totblk[t], y.at[slot], x_hbm, sem.at[slot])
        zbuf[...] = jnp.zeros_like(zbuf)
        for e in range(N_EXPERTS + 1):
            tb = tail_blk[e]

            def zero_body(j, carry):
                d0 = pl.multiple_of((tb + j) * ROW_BLK, ROW_BLK)
                pltpu.make_async_copy(zbuf, x_hbm.at[pl.ds(d0, ROW_BLK), :], zsem).start()
                return carry

            lax.fori_loop(0, tail_n[e], zero_body, 0)
        for e in range(N_EXPERTS + 1):
            _block_copies_wait(tail_n[e], zbuf, x_hbm, zsem)


def _moe_dispatch(plan, u, n_rows):
    n, d = u.shape
    nt = n // DISP_R
    grid_spec = pltpu.PrefetchScalarGridSpec(
        num_scalar_prefetch=6,
        grid=(nt,),
        in_specs=[
            pl.BlockSpec((1, 2, DISP_R), lambda t, *_: (t, 0, 0)),
            pl.BlockSpec((DISP_R, d), lambda t, *_: (t, 0)),
        ],
        out_specs=pl.BlockSpec(memory_space=pl.ANY),
        scratch_shapes=[
            pltpu.VMEM((2, DISP_ROWS, d), _f32),
            pltpu.SemaphoreType.DMA((2,)),
            pltpu.VMEM((ROW_BLK, d), _f32),
            pltpu.SemaphoreType.DMA(()),
        ],
    )
    return pl.pallas_call(
        _dispatch_kernel,
        grid_spec=grid_spec,
        out_shape=jax.ShapeDtypeStruct((n_rows, d), _f32),
        compiler_params=_cparams(("arbitrary",), VMEM_LIMIT),
        name="moe_dispatch",
    )(plan["gblk"], plan["nblk"], plan["sblk"], plan["totblk"], plan["tail_blk"], plan["tail_n"], plan["d"], u)


def _moe_kernel(te_ref, nu_ref, x_ref, wg_ref, wu_ref, wd_ref, o_ref, acc):
    i = pl.program_id(0)
    n_used = nu_ref[0]

    @pl.when(i < n_used)
    def _():
        x = x_ref[...].astype(_bf16)
        for c in range(D_FF_EXPERT // MOE_FF_CHUNK):
            cols = slice(c * MOE_FF_CHUNK, (c + 1) * MOE_FF_CHUNK)
            a = _dot(x, wg_ref[0, :, cols])
            mid = (_silu(a) * _dot(x, wu_ref[0, :, cols])).astype(_bf16)
            part = _dot(mid, wd_ref[0, cols, :])
            if c == 0:
                acc[...] = part
            else:
                acc[...] += part
        o_ref[...] = acc[...]

    @pl.when(i >= n_used)
    def _():
        o_ref[...] = jnp.zeros_like(o_ref)


def _moe_experts(plan, x, wg, wu, wd):
    tile_expert, n_used = plan["tile_expert"], plan["n_used"]
    n_tiles = tile_expert.shape[0]
    d = x.shape[1]
    w_spec = lambda shape: pl.BlockSpec(shape, lambda i, te, nu: (te[i], 0, 0), pipeline_mode=pl.Buffered(1))
    grid_spec = pltpu.PrefetchScalarGridSpec(
        num_scalar_prefetch=2,
        grid=(n_tiles,),
        in_specs=[
            pl.BlockSpec((MOE_TM, d), lambda i, te, nu: (jnp.minimum(i, nu[0] - 1), 0)),
            w_spec((1, d, D_FF_EXPERT)),
            w_spec((1, d, D_FF_EXPERT)),
            w_spec((1, D_FF_EXPERT, d)),
        ],
        out_specs=pl.BlockSpec((MOE_TM, d), lambda i, te, nu: (i, 0)),
        scratch_shapes=[pltpu.VMEM((MOE_TM, d), _f32)],
    )
    return pl.pallas_call(
        _moe_kernel,
        grid_spec=grid_spec,
        out_shape=jax.ShapeDtypeStruct((n_tiles * MOE_TM, d), _f32),
        compiler_params=_cparams(("arbitrary",), VMEM_LIMIT),
        name="moe_experts",
    )(tile_expert, n_used, x, wg, wu, wd)


def _combine_kernel(gblk, nblk, sblk, totblk, d_ref, route_ref, h_ref, m_ref, g_ref, y_hbm, o_ref, z, sem):
    t = pl.program_id(0)
    nt = pl.num_programs(0)
    slot = t % 2

    @pl.when(t == 0)
    def _():
        z[...] = jnp.zeros_like(z)
        _block_copies_start(nblk, t, y_hbm, gblk, z.at[0], sblk, sem.at[0])

    @pl.when(t + 1 < nt)
    def _():
        _block_copies_start(nblk, t + 1, y_hbm, gblk, z.at[1 - slot], sblk, sem.at[1 - slot])

    _block_copies_wait(totblk[t], y_hbm, z.at[slot], sem.at[slot])
    zb = z[slot].astype(_bf16)
    rows = lax.broadcasted_iota(jnp.int32, (DISP_ROWS, DISP_R), 0)
    pick = lambda k: _dot_tn(jnp.where(rows == d_ref[0, k:k + 1, :], 1.0, 0.0).astype(_bf16), zb)
    r = route_ref[...]
    f = r[:, 0:1] * pick(0) + r[:, 1:2] * pick(1)
    hn = h_ref[...] + m_ref[0][5:6] * f
    o_ref[...] = hn * lax.rsqrt(jnp.mean(hn * hn, axis=-1, keepdims=True) + EPS) * g_ref[...]


def _moe_combine(plan, route, h, mods, g, y_sorted, tokens_per_batch):
    n, d = h.shape
    nt = n // DISP_R
    per_b = tokens_per_batch // DISP_R
    grid_spec = pltpu.PrefetchScalarGridSpec(
        num_scalar_prefetch=4,
        grid=(nt,),
        in_specs=[
            pl.BlockSpec((1, 2, DISP_R), lambda t, *_: (t, 0, 0)),
            pl.BlockSpec((DISP_R, ROUTE_LANES), lambda t, *_: (t, 0)),
            pl.BlockSpec((DISP_R, d), lambda t, *_: (t, 0)),
            pl.BlockSpec((1, 6, d), lambda t, *_: (t // per_b, 0, 0)),
            pl.BlockSpec((1, d), lambda t, *_: (0, 0)),
            pl.BlockSpec(memory_space=pl.ANY),
        ],
        out_specs=pl.BlockSpec((DISP_R, d), lambda t, *_: (t, 0)),
        scratch_shapes=[
            pltpu.VMEM((2, DISP_ROWS, d), _f32),
            pltpu.SemaphoreType.DMA((2,)),
        ],
    )
    return pl.pallas_call(
        _combine_kernel,
        grid_spec=grid_spec,
        out_shape=jax.ShapeDtypeStruct((n, d), _f32),
        compiler_params=_cparams(("arbitrary",), VMEM_LIMIT),
        name="moe_combine",
    )(plan["gblk"], plan["nblk"], plan["sblk"], plan["totblk"], plan["d"], route, h, mods, g, y_sorted)


def _routing_plan(route):
    n = route.shape[0]
    nt = n // DISP_R
    i32 = jnp.int32
    e = route[:, 2:4].astype(i32).reshape(nt, 1, 2 * DISP_R)
    onehot = (e == jnp.arange(N_EXPERTS, dtype=i32)[None, :, None]).astype(i32)
    csum = jnp.cumsum(onehot, axis=2)
    rank = jnp.sum(csum * onehot, axis=1) - 1
    cnt = csum[:, :, -1]
    cpad = (cnt + ROW_BLK - 1) // ROW_BLK * ROW_BLK
    seg = jnp.cumsum(cpad, axis=1) - cpad
    d = jnp.sum(onehot * seg[:, :, None], axis=1) + rank
    rows_e = jnp.sum(cpad, axis=0)
    tiles_e = (rows_e + MOE_TM - 1) // MOE_TM
    tile_end = jnp.cumsum(tiles_e)
    off = (tile_end - tiles_e) * MOE_TM
    glob = off[None, :] + jnp.cumsum(cpad, axis=0) - cpad
    n_tiles = -(-(2 * n + nt * N_EXPERTS * (ROW_BLK - 1)) // MOE_TM) + N_EXPERTS
    n_used = tile_end[-1]
    ti = jnp.minimum(jnp.arange(n_tiles, dtype=i32), n_used - 1)
    tile_expert = jnp.sum((ti[:, None] >= tile_end[None, :]).astype(i32), axis=1)
    nblk = cpad // ROW_BLK
    return {
        "d": d.reshape(nt, DISP_R, 2).transpose(0, 2, 1),
        "gblk": (glob // ROW_BLK).reshape(-1).astype(i32),
        "nblk": nblk.reshape(-1).astype(i32),
        "sblk": (seg // ROW_BLK).reshape(-1).astype(i32),
        "totblk": jnp.sum(nblk, axis=1).astype(i32),
        "tail_blk": jnp.append((off + rows_e) // ROW_BLK, n_used * (MOE_TM // ROW_BLK)).astype(i32),
        "tail_n": jnp.append((tiles_e * MOE_TM - rows_e) // ROW_BLK,
                             (n_tiles - n_used) * (MOE_TM // ROW_BLK)).astype(i32),
        "tile_expert": tile_expert.astype(i32),
        "n_used": n_used.reshape(1).astype(i32),
        "n_rows": n_tiles * MOE_TM,
    }


def _permuted_w_in(w_in_l):
    d = w_in_l.shape[0]
    c = {}
    start = 0
    for name, size in (("na_qkv", 3 * NA_WIDTH), ("conv_glu", 2 * CONV_CH), ("ret_q", RET_QK_WIDTH),
                       ("ret_k", RET_QK_WIDTH), ("ret_v", RET_WIDTH), ("ret_g", 2 * RET_WIDTH)):
        c[name] = w_in_l[:, start:start + size]
        start += size
    sub = RET_QK_DIM // 2
    halves_first = lambda w: w.reshape(d, RET_HEADS, 2, sub).transpose(0, 2, 1, 3).reshape(d, RET_QK_WIDTH)
    wa = jnp.concatenate([c["na_qkv"], c["ret_v"]], axis=1)
    wb = jnp.concatenate([c["conv_glu"], halves_first(c["ret_q"]), halves_first(c["ret_k"]), c["ret_g"]], axis=1)
    return wa.astype(_bf16), wb.astype(_bf16)


def kernel(x, c, ctx, c_ctx, w_mod, b_mod, norm1_w, norm2_w, w_in, w_out, na_rpb, conv_w, conv_b, conv_ln_w,
           conv_ln_b, ret_decay, ret_gn_w, ffn_w_gate, ffn_w_up, ffn_w_down, moe_router, moe_router_b,
           moe_w_gate, moe_w_up, moe_w_down, final_norm_w):
    b, t, d = x.shape
    lc = ctx.shape[1]
    assert d == D_MODEL and t % Q_TILE == 0 and t // GRID_W >= K_ROWS and b + 1 <= MOD_ROWS
    assert t % 512 == 0 and lc % 256 == 0 and (2 * b * t) % MOE_TM == 0

    cvecs = jnp.zeros((MOD_ROWS, d), _f32).at[:b].set(c).at[b].set(c_ctx)
    mods = _mod_vectors(cvecs, w_mod, b_mod).reshape(DEPTH, MOD_ROWS, 6, d)
    cos_lat, sin_lat = _rope_tables(t)
    cos_ctx = jnp.ones((lc, RET_QK_WIDTH), _f32)
    sin_ctx = jnp.zeros((lc, RET_QK_WIDTH), _f32)
    vec = lambda a: a.reshape(1, -1)
    lat_tm = 512
    ctx_tm = 256

    h_lat, h_ctx = x, ctx
    out = None
    for l in range(DEPTH):
        last = l == DEPTH - 1
        m_lat = mods[l, :b]
        m_ctx = mods[l, b:b + 1]
        wa, wb = _permuted_w_in(w_in[l])
        wo = w_out[l].astype(_bf16)
        gamma = 1.0 - jnp.exp2(-ret_decay[l].astype(_f32))
        lg = jnp.log(gamma)
        lgq = jnp.tile(jnp.repeat(lg, RET_QK_DIM // 2, axis=1), (1, 2)).reshape(2, 1, RET_QK_WIDTH)
        lgv = jnp.repeat(lg, RET_V_DIM, axis=1).reshape(2, 1, RET_WIDTH)

        a_lat, b_lat = _inproj(h_lat, m_lat, vec(norm1_w[l]), wa, wb, lat_tm)
        a_ctx, b_ctx = _inproj(h_ctx, m_ctx, vec(norm1_w[l]), wa, wb, ctx_tm)

        y_na = _na_attention(a_lat, a_ctx, _na_bias_table(na_rpb[l]))
        y_conv = _conv_module(b_lat, conv_w[l], conv_b[l], conv_ln_w[l], conv_ln_b[l])
        s_zero = jnp.zeros((b, 2, RET_QK_WIDTH, RET_WIDTH), _f32)
        y_ret_c, s_ctx = _retention(a_ctx, b_ctx, cos_ctx, sin_ctx, s_zero, lg, lgq, lgv, ret_gn_w[l])
        y_ret, _ = _retention(a_lat, b_lat, cos_lat, sin_lat, s_ctx, lg, lgq, lgv, ret_gn_w[l])

        if not last:
            y_na_c = _ctx_attention(a_ctx)
            y_conv_c = _conv_module(b_ctx, conv_w[l], conv_b[l], conv_ln_w[l], conv_ln_b[l])
            h_ctx, u_ctx = _outproj(y_na_c, y_conv_c, y_ret_c, h_ctx, wo, m_ctx, vec(norm2_w[l]), ctx_tm)

        if l % 2 == 0:
            h_lat, u_lat = _outproj(y_na, y_conv, y_ret, h_lat, wo, m_lat, vec(norm2_w[l]), lat_tm)
            wg = ffn_w_gate[l // 2].astype(_bf16)
            wu = ffn_w_up[l // 2].astype(_bf16)
            wd = ffn_w_down[l // 2].astype(_bf16)
            h_lat = _ffn(u_lat, h_lat, m_lat, wg, wu, wd, lat_tm)
            if not last:
                h_ctx = _ffn(u_ctx, h_ctx, m_ctx, wg, wu, wd, ctx_tm)
            out = h_lat
        else:
            assert last
            rw = jnp.zeros((d, ROUTE_LANES), _f32).at[:, :N_EXPERTS].set(moe_router[l // 2])
            rb = jnp.full((1, ROUTE_LANES), NEG_INF, _f32).at[0, :N_EXPERTS].set(moe_router_b[l // 2])
            h_lat, u_lat, route = _outproj(y_na, y_conv, y_ret, h_lat, wo, m_lat, vec(norm2_w[l]), lat_tm,
                                           router=(rw, rb))
            route = route.reshape(b * t, ROUTE_LANES)
            plan = _routing_plan(route)
            x_sorted = _moe_dispatch(plan, u_lat.reshape(b * t, d), plan["n_rows"])
            y_sorted = _moe_experts(plan, x_sorted, moe_w_gate[l // 2].astype(_bf16),
                                    moe_w_up[l // 2].astype(_bf16), moe_w_down[l // 2].astype(_bf16))
            out = _moe_combine(plan, route, h_lat.reshape(b * t, d), m_lat, vec(final_norm_w), y_sorted, t)
            out = out.reshape(b, t, d)
    return out
```

```python
import functools

import numpy as np
import jax
import jax.numpy as jnp
from jax import lax
from jax.experimental import pallas as pl
from jax.experimental.pallas import tpu as pltpu

D_MODEL = 1024
DEPTH = 2
GRID_W = 64
NA_HEAD_DIM = 64
NA_WIDTH = 512
NA_HEADS = 8
NB_ROWS = 8
NB_COLS = 16
CONV_CH = 256
CONV_WIDTH = 31
RET_WIDTH = 256
RET_HEADS = 4
RET_V_DIM = 64
RET_QK_DIM = 32
RET_QK_WIDTH = 128
D_FF = 2816
N_EXPERTS = 8
D_FF_EXPERT = 3584
ROPE_BASE = 10000.0
EPS = 1e-6
NEG_INF = -1e30

A_WIDTH = 3 * NA_WIDTH + RET_WIDTH
B_WIDTH = 2 * CONV_CH + 2 * RET_QK_WIDTH + 2 * RET_WIDTH

HEAD_GROUP = 4
HG_LANES = HEAD_GROUP * NA_HEAD_DIM
Q_ROWS = 4
Q_TILE = Q_ROWS * GRID_W
K_ROWS = 12
K_TILE = K_ROWS * GRID_W
RET_CHUNK = 256
CONV_CHUNK = 128
CONV_PAD = 16
SUBLANES = 8
MOE_TM = 512
MOE_FF_CHUNK = 512
ROW_BLK = SUBLANES
DISP_R = 512
DISP_ROWS = 2 * DISP_R + N_EXPERTS * ROW_BLK
ROUTE_LANES = 128
VMEM_LIMIT = 56 * 1024 * 1024

_f32 = jnp.float32
_bf16 = jnp.bfloat16


def _cparams(sem, vmem=None):
    return pltpu.CompilerParams(dimension_semantics=sem, vmem_limit_bytes=vmem)


def _resident(shape):
    return pl.BlockSpec(shape, lambda *_: (0,) * len(shape), pipeline_mode=pl.Buffered(1))


def _sigmoid(x):
    return 1.0 / (1.0 + jnp.exp(-x))


def _silu(x):
    return x * _sigmoid(x)


def _dot(a, b):
    return jnp.dot(a, b, preferred_element_type=_f32)


def _dot_nt(a, b):
    return lax.dot_general(a, b, (((1,), (1,)), ((), ())), preferred_element_type=_f32)


def _dot_tn(a, b):
    return lax.dot_general(a, b, (((0,), (0,)), ((), ())), preferred_element_type=_f32)


def _split_bf16(a):
    hi = a.astype(_bf16)
    lo = (a - hi.astype(_f32)).astype(_bf16)
    return hi, lo


def _dot_split(a, b):
    ah, al = _split_bf16(a)
    bh, bl = _split_bf16(b)
    return _dot(ah, bh) + _dot(al, bh) + _dot(ah, bl)


def _rms_modulate(x, g, shift, scale):
    y = x * lax.rsqrt(jnp.mean(x * x, axis=-1, keepdims=True) + EPS)
    return (y * g) * (1.0 + scale) + shift


MOD_ROWS = 24
MOD_TN = 1536


def _mod_kernel(c_ref, w_ref, b_ref, o_ref):
    s = _silu(c_ref[...])
    o_ref[0] = _dot_split(s, w_ref[0]) + b_ref[0]


def _mod_vectors(cvecs, w_mod, b_mod):
    n = w_mod.shape[2]
    return pl.pallas_call(
        _mod_kernel,
        grid=(DEPTH, n // MOD_TN),
        in_specs=[
            pl.BlockSpec((MOD_ROWS, D_MODEL), lambda l, j: (0, 0)),
            pl.BlockSpec((1, D_MODEL, MOD_TN), lambda l, j: (l, 0, j)),
            pl.BlockSpec((1, 1, MOD_TN), lambda l, j: (l, 0, j)),
        ],
        out_specs=pl.BlockSpec((1, MOD_ROWS, MOD_TN), lambda l, j: (l, 0, j)),
        out_shape=jax.ShapeDtypeStruct((DEPTH, MOD_ROWS, n), _f32),
        compiler_params=_cparams(("arbitrary", "arbitrary"), VMEM_LIMIT),
        name="mod_vectors",
    )(cvecs, w_mod, b_mod.reshape(DEPTH, 1, n))


def _inproj_kernel(x_ref, m_ref, g_ref, wa_ref, wb_ref, oa_ref, ob_ref):
    m = m_ref[0]
    u = _rms_modulate(x_ref[0], g_ref[...], m[0:1], m[1:2]).astype(_bf16)
    oa_ref[0] = _dot(u, wa_ref[...]).astype(_bf16)
    ob_ref[0] = _dot(u, wb_ref[...])


def _inproj(x, mods, g, wa, wb, tm):
    b, l, d = x.shape
    nb = mods.shape[0]
    mod_map = (lambda i, j: (i, 0, 0)) if nb > 1 else (lambda i, j: (0, 0, 0))
    return pl.pallas_call(
        _inproj_kernel,
        grid=(b, l // tm),
        in_specs=[
            pl.BlockSpec((1, tm, d), lambda i, j: (i, j, 0)),
            pl.BlockSpec((1, 6, d), mod_map),
            pl.BlockSpec((1, d), lambda i, j: (0, 0)),
            _resident((d, A_WIDTH)),
            _resident((d, B_WIDTH)),
        ],
        out_specs=[
            pl.BlockSpec((1, tm, A_WIDTH), lambda i, j: (i, j, 0)),
            pl.BlockSpec((1, tm, B_WIDTH), lambda i, j: (i, j, 0)),
        ],
        out_shape=[
            jax.ShapeDtypeStruct((b, l, A_WIDTH), _bf16),
            jax.ShapeDtypeStruct((b, l, B_WIDTH), _f32),
        ],
        compiler_params=_cparams(("arbitrary", "arbitrary"), VMEM_LIMIT),
        name="inproj",
    )(x, mods, g, wa, wb)


def _masked_heads_attention(q, keys, vals, n_biased, bias, lane):
    out = jnp.zeros((q.shape[0], HG_LANES), _f32)
    for h in range(HEAD_GROUP):
        hm = (lane >= h * NA_HEAD_DIM) & (lane < (h + 1) * NA_HEAD_DIM)
        qm = jnp.where(hm, q, jnp.zeros_like(q)) * jnp.asarray(NA_HEAD_DIM ** -0.5, q.dtype)
        s = _dot_nt(qm, keys)
        parts = [s[:, :n_biased] + bias(h), s[:, n_biased:]] if n_biased else [s]
        mx = parts[0].max(axis=-1, keepdims=True)
        for part in parts[1:]:
            mx = jnp.maximum(mx, part.max(axis=-1, keepdims=True))
        probs = [jnp.exp(part - mx) for part in parts]
        den = probs[0].sum(axis=-1, keepdims=True)
        for p in probs[1:]:
            den = den + p.sum(axis=-1, keepdims=True)
        p = jnp.concatenate([p.astype(_bf16) for p in probs], axis=-1)
        out = jnp.where(hm, _dot(p, vals) / den, out)
    return out


def _na_kernel(q_ref, k_ref, v_ref, kc_ref, vc_ref, bias_ref, o_ref, kall, vall):
    n_tiles = q_ref.shape[1] // Q_TILE
    rows = q_ref.shape[1] // GRID_W
    lane = lax.broadcasted_iota(jnp.int32, (1, HG_LANES), 1)
    kall[K_TILE:, :] = kc_ref[0]
    vall[K_TILE:, :] = vc_ref[0]

    def body(g, carry):
        krow0 = jnp.clip(Q_ROWS * g - NB_ROWS // 2, 0, rows - K_ROWS)
        start = pl.multiple_of(krow0 * GRID_W, GRID_W)
        cls = jnp.where(g == 0, 0, jnp.where(g == n_tiles - 1, 2, 1))
        qs = pl.multiple_of(g * Q_TILE, Q_TILE)
        q = q_ref[0, pl.ds(qs, Q_TILE), :]
        kall[0:K_TILE, :] = k_ref[0, pl.ds(start, K_TILE), :]
        vall[0:K_TILE, :] = v_ref[0, pl.ds(start, K_TILE), :]
        out = _masked_heads_attention(q, kall[...], vall[...], K_TILE, lambda h: bias_ref[cls, h], lane)
        o_ref[0, pl.ds(qs, Q_TILE), :] = out.astype(o_ref.dtype)
        return carry

    lax.fori_loop(0, n_tiles, body, 0, unroll=2)


def _na_attention(a_lat, a_ctx, bias):
    b, t, _ = a_lat.shape
    lc = a_ctx.shape[1]
    ng = NA_HEADS // HEAD_GROUP
    kq, kk, kv = 0, NA_WIDTH // HG_LANES, 2 * NA_WIDTH // HG_LANES
    return pl.pallas_call(
        _na_kernel,
        grid=(ng, b),
        in_specs=[
            pl.BlockSpec((1, t, HG_LANES), lambda g, i: (i, 0, kq + g)),
            pl.BlockSpec((1, t, HG_LANES), lambda g, i: (i, 0, kk + g)),
            pl.BlockSpec((1, t, HG_LANES), lambda g, i: (i, 0, kv + g)),
            pl.BlockSpec((1, lc, HG_LANES), lambda g, i: (i, 0, kk + g)),
            pl.BlockSpec((1, lc, HG_LANES), lambda g, i: (i, 0, kv + g)),
            pl.BlockSpec((3, HEAD_GROUP, Q_TILE, K_TILE), lambda g, i: (0, g, 0, 0)),
        ],
        out_specs=pl.BlockSpec((1, t, HG_LANES), lambda g, i: (i, 0, g)),
        out_shape=jax.ShapeDtypeStruct((b, t, NA_WIDTH), _bf16),
        scratch_shapes=[pltpu.VMEM((K_TILE + lc, HG_LANES), _bf16),
                        pltpu.VMEM((K_TILE + lc, HG_LANES), _bf16)],
        compiler_params=_cparams(("arbitrary", "arbitrary"), VMEM_LIMIT),
        name="na_attention",
    )(a_lat, a_lat, a_lat, a_ctx, a_ctx, bias)


def _ctx_attn_kernel(q_ref, k_ref, v_ref, o_ref):
    lane = lax.broadcasted_iota(jnp.int32, (1, HG_LANES), 1)
    out = _masked_heads_attention(q_ref[0], k_ref[0], v_ref[0], 0, None, lane)
    o_ref[0] = out.astype(o_ref.dtype)


def _ctx_attention(a_ctx):
    b, lc, _ = a_ctx.shape
    ng = NA_HEADS // HEAD_GROUP
    kq, kk, kv = 0, NA_WIDTH // HG_LANES, 2 * NA_WIDTH // HG_LANES
    return pl.pallas_call(
        _ctx_attn_kernel,
        grid=(ng, b),
        in_specs=[
            pl.BlockSpec((1, lc, HG_LANES), lambda g, i: (i, 0, kq + g)),
            pl.BlockSpec((1, lc, HG_LANES), lambda g, i: (i, 0, kk + g)),
            pl.BlockSpec((1, lc, HG_LANES), lambda g, i: (i, 0, kv + g)),
        ],
        out_specs=pl.BlockSpec((1, lc, HG_LANES), lambda g, i: (i, 0, g)),
        out_shape=jax.ShapeDtypeStruct((b, lc, NA_WIDTH), _bf16),
        compiler_params=_cparams(("arbitrary", "arbitrary")),
        name="ctx_attention",
    )(a_ctx, a_ctx, a_ctx)


def _na_bias_table(rpb):
    h = rpb.shape[0]
    pad = GRID_W - NB_COLS
    rp = jnp.pad(rpb.astype(_f32), ((0, 0), (0, 0), (pad, pad)))
    toep = jnp.stack([rp[:, :, GRID_W - 1 - c: 2 * GRID_W - 1 - c] for c in range(GRID_W)], axis=2)
    c = np.arange(GRID_W)
    wc0 = np.clip(c - NB_COLS // 2, 0, GRID_W - NB_COLS)
    col_ok = (c[None, :] >= wc0[:, None]) & (c[None, :] < wc0[:, None] + NB_COLS)
    toep = jnp.where(col_ok[None, None], toep, NEG_INF)
    masked = jnp.full((h, GRID_W, GRID_W), NEG_INF, _f32)
    classes = ((lambda i: 0, NB_ROWS - 1),
               (lambda i: i, NB_ROWS - 1 - NB_ROWS // 2),
               (lambda i: K_ROWS - NB_ROWS, NB_ROWS - 1 - (K_ROWS - Q_ROWS)))
    tables = []
    for off, dr0 in classes:
        rows = []
        for i in range(Q_ROWS):
            blocks = [toep[:, j - i + dr0] if off(i) <= j < off(i) + NB_ROWS else masked for j in range(K_ROWS)]
            rows.append(jnp.concatenate(blocks, axis=-1))
        tables.append(jnp.stack(rows, axis=1).reshape(h, Q_TILE, K_TILE))
    return jnp.stack(tables, axis=0)


def _conv_kernel(u_ref, w_ref, b_ref, lnw_ref, lnb_ref, o_ref, ypad):
    l = u_ref.shape[1]
    ypad[0:CONV_PAD, :] = jnp.zeros((CONV_PAD, CONV_CH), _f32)
    ypad[CONV_PAD + l:2 * CONV_PAD + l, :] = jnp.zeros((CONV_PAD, CONV_CH), _f32)
    ypad[CONV_PAD:CONV_PAD + l, :] = u_ref[0, :, 0:CONV_CH] * _sigmoid(u_ref[0, :, CONV_CH:2 * CONV_CH])
    shift = CONV_PAD - CONV_WIDTH // 2

    def body(c, carry):
        base = pl.multiple_of(c * CONV_CHUNK, CONV_CHUNK)
        win = ypad[pl.ds(base, CONV_CHUNK + 2 * CONV_PAD), :]
        acc = jnp.zeros((CONV_CHUNK, CONV_CH), _f32)
        for r in range(SUBLANES):
            offs = [o for o in range(shift, shift + CONV_WIDTH) if o % SUBLANES == r]
            wr = win if r == 0 else pltpu.roll(win, win.shape[0] - r, 0)
            for o in offs:
                acc = acc + wr[o - r:o - r + CONV_CHUNK, :] * w_ref[o - shift:o - shift + 1, :]
        y = acc + b_ref[...]
        mu = jnp.mean(y, axis=-1, keepdims=True)
        yc = y - mu
        var = jnp.mean(yc * yc, axis=-1, keepdims=True)
        z = yc * lax.rsqrt(var + EPS) * lnw_ref[...] + lnb_ref[...]
        o_ref[0, pl.ds(base, CONV_CHUNK), :] = _silu(z).astype(o_ref.dtype)
        return carry

    lax.fori_loop(0, l // CONV_CHUNK, body, 0)


def _conv_module(bf, conv_w, conv_b, ln_w, ln_b):
    b, l, _ = bf.shape
    vec = lambda a: a.reshape(1, CONV_CH)
    return pl.pallas_call(
        _conv_kernel,
        grid=(b,),
        in_specs=[
            pl.BlockSpec((1, l, 2 * CONV_CH), lambda i: (i, 0, 0)),
            pl.BlockSpec((CONV_WIDTH, CONV_CH), lambda i: (0, 0)),
            pl.BlockSpec((1, CONV_CH), lambda i: (0, 0)),
            pl.BlockSpec((1, CONV_CH), lambda i: (0, 0)),
            pl.BlockSpec((1, CONV_CH), lambda i: (0, 0)),
        ],
        out_specs=pl.BlockSpec((1, l, CONV_CH), lambda i: (i, 0, 0)),
        out_shape=jax.ShapeDtypeStruct((b, l, CONV_CH), _bf16),
        scratch_shapes=[pltpu.VMEM((l + 2 * CONV_PAD, CONV_CH), _f32)],
        compiler_params=_cparams(("arbitrary",), VMEM_LIMIT),
        name="conv_module",
    )(bf, conv_w, vec(conv_b), vec(ln_w), vec(ln_b))


def _ret_kernel(lg_ref, q_ref, k_ref, v_ref, gf_ref, gb_ref, cos_ref, sin_ref, s0_ref, lgq_ref, lgv_ref,
                gnw_ref, y_ref, sfin_ref, dmat, of_s, ob_s, st_s):
    l = q_ref.shape[1]
    c = min(RET_CHUNK, l)
    nc = l // c
    half = RET_QK_WIDTH // 2
    sub = RET_QK_DIM // 2
    ii = lax.broadcasted_iota(jnp.int32, (c, c), 0)
    jj = lax.broadcasted_iota(jnp.int32, (c, c), 1)
    diff = (ii - jj).astype(_f32)
    for h in range(RET_HEADS):
        dmat[0, h] = jnp.where(diff >= 0, jnp.exp(lg_ref[0, h] * jnp.maximum(diff, 0.0)), 0.0)
        dmat[1, h] = jnp.where(diff <= 0, jnp.exp(lg_ref[1, h] * jnp.maximum(-diff, 0.0)), 0.0)
    pos = lax.broadcasted_iota(jnp.int32, (c, 1), 0).astype(_f32)
    lane_q = lax.broadcasted_iota(jnp.int32, (1, RET_QK_WIDTH), 1)
    head_q = (lane_q % half) // sub
    lane_v = lax.broadcasted_iota(jnp.int32, (1, RET_WIDTH), 1)
    head_v = lane_v // RET_V_DIM
    row_h = (lax.broadcasted_iota(jnp.int32, (RET_QK_WIDTH, RET_WIDTH), 0) % half) // sub
    col_h = lax.broadcasted_iota(jnp.int32, (RET_QK_WIDTH, RET_WIDTH), 1) // RET_V_DIM
    blockmask = row_h == col_h
    q_dec = (jnp.exp(lgq_ref[0] * (pos + 1.0)), jnp.exp(lgq_ref[1] * (c - pos)))
    k_dec = (jnp.exp(lgq_ref[0] * (c - 1.0 - pos)), jnp.exp(lgq_ref[1] * pos))
    c_dec = (jnp.exp(lgv_ref[0] * float(c)), jnp.exp(lgv_ref[1] * float(c)))
    k_scale = RET_QK_DIM ** -0.5
    st_s[...] = s0_ref[0]

    def step(n, carry):
        for d in range(2):
            cidx = n if d == 0 else nc - 1 - n
            base = pl.multiple_of(cidx * c, c)
            cs = cos_ref[pl.ds(base, c), :]
            sn = sin_ref[pl.ds(base, c), :]
            q = q_ref[0, pl.ds(base, c), :]
            k = k_ref[0, pl.ds(base, c), :]
            qr = q * cs + pltpu.roll(q, half, 1) * sn
            kr = (k * cs + pltpu.roll(k, half, 1) * sn) * k_scale
            v = v_ref[0, pl.ds(base, c), :]
            qb = qr.astype(_bf16)
            kb = kr.astype(_bf16)
            o = _dot((qr * q_dec[d]).astype(_bf16), st_s[d].astype(_bf16))
            for h in range(RET_HEADS):
                s = _dot_nt(jnp.where(head_q == h, qb, jnp.zeros_like(qb)), kb)
                inner = (s * dmat[d, h]).astype(_bf16)
                o = o + _dot(inner, jnp.where(head_v == h, v, jnp.zeros_like(v)))
            if d == 0:
                of_s[pl.ds(base, c), :] = o
            else:
                ob_s[pl.ds(base, c), :] = o
            upd = _dot_tn((kr * k_dec[d]).astype(_bf16), v)
            st_s[d] = c_dec[d] * st_s[d] + jnp.where(blockmask, upd, 0.0)
        return carry

    lax.fori_loop(0, nc, step, 0)
    sfin_ref[0] = st_s[...]

    gi = lax.broadcasted_iota(jnp.int32, (RET_WIDTH, RET_WIDTH), 0) // RET_V_DIM
    gj = lax.broadcasted_iota(jnp.int32, (RET_WIDTH, RET_WIDTH), 1) // RET_V_DIM
    gmean = jnp.where(gi == gj, 1.0 / RET_V_DIM, 0.0).astype(_bf16)

    def group_mean(a):
        return _dot(a.astype(_bf16), gmean)

    def head_norm(o):
        dlt = o - group_mean(o)
        var = group_mean(dlt * dlt)
        return dlt * lax.rsqrt(var + EPS) * gnw_ref[...]

    def fin(n, carry):
        base = pl.multiple_of(n * c, c)
        yf = head_norm(of_s[pl.ds(base, c), :])
        yb = head_norm(ob_s[pl.ds(base, c), :])
        y = _silu(gf_ref[0, pl.ds(base, c), :]) * yf + _silu(gb_ref[0, pl.ds(base, c), :]) * yb
        y_ref[0, pl.ds(base, c), :] = y.astype(y_ref.dtype)
        return carry

    lax.fori_loop(0, nc, fin, 0)


def _retention(a, bf, cos_t, sin_t, s0, lg, lgq, lgv, gn_w):
    b, l, _ = a.shape
    c = min(RET_CHUNK, l)
    qi = 2 * CONV_CH // RET_QK_WIDTH
    gi = (2 * CONV_CH + 2 * RET_QK_WIDTH) // RET_WIDTH
    vi = 3 * NA_WIDTH // RET_WIDTH
    return pl.pallas_call(
        _ret_kernel,
        grid=(b,),
        in_specs=[
            pl.BlockSpec(memory_space=pltpu.SMEM),
            pl.BlockSpec((1, l, RET_QK_WIDTH), lambda i: (i, 0, qi)),
            pl.BlockSpec((1, l, RET_QK_WIDTH), lambda i: (i, 0, qi + 1)),
            pl.BlockSpec((1, l, RET_WIDTH), lambda i: (i, 0, vi)),
            pl.BlockSpec((1, l, RET_WIDTH), lambda i: (i, 0, gi)),
            pl.BlockSpec((1, l, RET_WIDTH), lambda i: (i, 0, gi + 1)),
            pl.BlockSpec((l, RET_QK_WIDTH), lambda i: (0, 0)),
            pl.BlockSpec((l, RET_QK_WIDTH), lambda i: (0, 0)),
            pl.BlockSpec((1, 2, RET_QK_WIDTH, RET_WIDTH), lambda i: (i, 0, 0, 0)),
            pl.BlockSpec((2, 1, RET_QK_WIDTH), lambda i: (0, 0, 0)),
            pl.BlockSpec((2, 1, RET_WIDTH), lambda i: (0, 0, 0)),
            pl.BlockSpec((1, RET_WIDTH), lambda i: (0, 0)),
        ],
        out_specs=[
            pl.BlockSpec((1, l, RET_WIDTH), lambda i: (i, 0, 0)),
            pl.BlockSpec((1, 2, RET_QK_WIDTH, RET_WIDTH), lambda i: (i, 0, 0, 0)),
        ],
        out_shape=[
            jax.ShapeDtypeStruct((b, l, RET_WIDTH), _bf16),
            jax.ShapeDtypeStruct((b, 2, RET_QK_WIDTH, RET_WIDTH), _f32),
        ],
        scratch_shapes=[
            pltpu.VMEM((2, RET_HEADS, c, c), _f32),
            pltpu.VMEM((l, RET_WIDTH), _f32),
            pltpu.VMEM((l, RET_WIDTH), _f32),
            pltpu.VMEM((2, RET_QK_WIDTH, RET_WIDTH), _f32),
        ],
        compiler_params=_cparams(("arbitrary",), VMEM_LIMIT),
        name="retention",
    )(lg, bf, bf, a, bf, bf, cos_t, sin_t, s0, lgq, lgv, gn_w.reshape(1, RET_WIDTH))


def _rope_tables(t_len):
    t = np.arange(t_len)
    row = (t // GRID_W).astype(np.float32)
    col = (t % GRID_W).astype(np.float32)
    axis_dim = RET_QK_DIM // 2
    inv = jnp.asarray(ROPE_BASE, _f32) ** (-jnp.arange(0, axis_dim, 2, dtype=_f32) / axis_dim)
    ang = jnp.concatenate([jnp.asarray(row)[:, None] * inv, jnp.asarray(col)[:, None] * inv], axis=-1)
    cos, sin = jnp.cos(ang), jnp.sin(ang)
    cos_t = jnp.tile(cos, (1, 2 * RET_HEADS))
    sin_t = jnp.concatenate([jnp.tile(-sin, (1, RET_HEADS)), jnp.tile(sin, (1, RET_HEADS))], axis=-1)
    return cos_t, sin_t


def _top2_route(logits):
    m = logits.shape[0]
    lt = logits.T[0:N_EXPERTS, :]
    sub = lax.broadcasted_iota(jnp.int32, lt.shape, 0).astype(_f32)
    m1 = lt.max(axis=0, keepdims=True)
    i1 = jnp.where(lt == m1, sub, float(N_EXPERTS)).min(axis=0, keepdims=True)
    rest = jnp.where(sub == i1, -jnp.inf, lt)
    m2 = rest.max(axis=0, keepdims=True)
    i2 = jnp.where(rest == m2, sub, float(N_EXPERTS)).min(axis=0, keepdims=True)
    e = jnp.exp(m2 - m1)
    p1 = 1.0 / (1.0 + e)
    p2 = e * p1
    oh1 = jnp.where(sub == i1, 1.0, 0.0)
    oh2 = jnp.where(sub == i2, 1.0, 0.0)
    cnt = oh1 + oh2
    before = (lax.broadcasted_iota(jnp.int32, (m, m), 0) < lax.broadcasted_iota(jnp.int32, (m, m), 1))
    pref = _dot(cnt.astype(_bf16), jnp.where(before, 1.0, 0.0).astype(_bf16))
    tot = cnt.sum(axis=1, keepdims=True)
    cpad = jnp.floor((tot + (ROW_BLK - 1.0)) * (1.0 / ROW_BLK)) * ROW_BLK
    d1 = (pref * oh1).sum(axis=0, keepdims=True)
    d2 = (pref * oh2).sum(axis=0, keepdims=True)
    seg = jnp.zeros((1, 1), _f32)
    for ex in range(N_EXPERTS):
        d1 = d1 + oh1[ex:ex + 1, :] * seg
        d2 = d2 + oh2[ex:ex + 1, :] * seg
        seg = seg + cpad[ex:ex + 1, :]
    rows8 = jnp.concatenate([p1, p2, i1, i2, d1, d2, jnp.zeros((ROUTE_LANES - 6, m), _f32)], axis=0)
    return (rows8.T, jnp.concatenate([d1, d2], axis=0).astype(jnp.int32),
            jnp.broadcast_to(tot, (N_EXPERTS, ROUTE_LANES)))


def _outproj_kernel(with_router, *refs):
    if with_router:
        (yna, yconv, yret, h_ref, w_ref, m_ref, g_ref, rw_ref, rb_ref,
         ho_ref, u_ref, route_ref, rows_ref, cnt_ref) = refs
    else:
        yna, yconv, yret, h_ref, w_ref, m_ref, g_ref, ho_ref, u_ref = refs
    y = jnp.concatenate([yna[0], yconv[0], yret[0]], axis=-1)
    m = m_ref[0]
    hn = h_ref[0] + m[2:3] * _dot(y, w_ref[...])
    ho_ref[0] = hn
    u = _rms_modulate(hn, g_ref[...], m[3:4], m[4:5])
    u_ref[0] = u.astype(u_ref.dtype)
    if with_router:
        route_ref[0], rows_ref[0], cnt_ref[0] = _top2_route(_dot_split(u, rw_ref[...]) + rb_ref[...])


def _outproj(y_na, y_conv, y_ret, h, w_out, mods, g2, tm, router=None):
    b, l, d = h.shape
    nb = mods.shape[0]
    mod_map = (lambda i, j: (i, 0, 0)) if nb > 1 else (lambda i, j: (0, 0, 0))
    tok = lambda w: pl.BlockSpec((1, tm, w), lambda i, j: (i, j, 0))
    in_specs = [tok(NA_WIDTH), tok(CONV_CH), tok(RET_WIDTH), tok(d),
                _resident((d, d)),
                pl.BlockSpec((1, 6, d), mod_map),
                pl.BlockSpec((1, d), lambda i, j: (0, 0))]
    args = [y_na, y_conv, y_ret, h, w_out, mods, g2]
    out_specs = [tok(d), tok(d)]
    if router is None:
        out_shape = [jax.ShapeDtypeStruct((b, l, d), _f32), jax.ShapeDtypeStruct((b, l, d), _bf16)]
    else:
        in_specs += [pl.BlockSpec((d, ROUTE_LANES), lambda i, j: (0, 0)),
                     pl.BlockSpec((1, ROUTE_LANES), lambda i, j: (0, 0))]
        args += list(router)
        assert tm == DISP_R
        nj = l // tm
        out_specs += [tok(ROUTE_LANES),
                      pl.BlockSpec((1, 2, tm), lambda i, j: (i * nj + j, 0, 0)),
                      pl.BlockSpec((1, N_EXPERTS, ROUTE_LANES), lambda i, j: (i * nj + j, 0, 0))]
        out_shape = [jax.ShapeDtypeStruct((b, l, d), _f32), jax.ShapeDtypeStruct((b, l, d), _bf16),
                     jax.ShapeDtypeStruct((b, l, ROUTE_LANES), _f32),
                     jax.ShapeDtypeStruct((b * nj, 2, tm), jnp.int32),
                     jax.ShapeDtypeStruct((b * nj, N_EXPERTS, ROUTE_LANES), _f32)]
    return pl.pallas_call(
        functools.partial(_outproj_kernel, router is not None),
        grid=(b, l // tm),
        in_specs=in_specs,
        out_specs=out_specs,
        out_shape=out_shape,
        compiler_params=_cparams(("arbitrary", "arbitrary"), VMEM_LIMIT),
        name="outproj",
    )(*args)


def _ffn_kernel(u_ref, h_ref, m_ref, wg_ref, wu_ref, wd_ref, o_ref):
    u = u_ref[0]
    a = _dot(u, wg_ref[...])
    mid = (_silu(a) * _dot(u, wu_ref[...])).astype(_bf16)
    o_ref[0] = h_ref[0] + m_ref[0][5:6] * _dot(mid, wd_ref[...])


def _ffn(u, h, mods, wg, wu, wd, tm):
    b, l, d = h.shape
    nb = mods.shape[0]
    dff = wg.shape[1]
    mod_map = (lambda i, j: (i, 0, 0)) if nb > 1 else (lambda i, j: (0, 0, 0))
    tok = lambda: pl.BlockSpec((1, tm, d), lambda i, j: (i, j, 0))
    return pl.pallas_call(
        _ffn_kernel,
        grid=(b, l // tm),
        in_specs=[tok(), tok(), pl.BlockSpec((1, 6, d), mod_map),
                  _resident((d, dff)), _resident((d, dff)), _resident((dff, d))],
        out_specs=tok(),
        out_shape=jax.ShapeDtypeStruct((b, l, d), _f32),
        compiler_params=_cparams(("arbitrary", "arbitrary"), VMEM_LIMIT),
        name="ffn",
    )(u, h, mods, wg, wu, wd)


def _block_copies_start(nblk_ref, seg, src, src_blk_ref, dst, dst_blk_ref, sem):
    for e in range(N_EXPERTS):
        k = seg * N_EXPERTS + e
        sb = src_blk_ref[k]
        db = dst_blk_ref[k]

        def body(j, carry):
            s0 = pl.multiple_of((sb + j) * ROW_BLK, ROW_BLK)
            d0 = pl.multiple_of((db + j) * ROW_BLK, ROW_BLK)
            pltpu.make_async_copy(src.at[pl.ds(s0, ROW_BLK), :], dst.at[pl.ds(d0, ROW_BLK), :], sem).start()
            return carry

        lax.fori_loop(0, nblk_ref[k], body, 0)


def _block_copies_wait(n, src, dst, sem):
    def body(j, carry):
        pltpu.make_async_copy(src.at[pl.ds(0, ROW_BLK), :], dst.at[pl.ds(0, ROW_BLK), :], sem).wait()
        return carry

    lax.fori_loop(0, n, body, 0)


def _dispatch_kernel(gblk, nblk, sblk, totblk, tail_blk, tail_n, d_ref, u_ref, x_hbm, y, sem, zbuf, zsem):
    t = pl.program_id(0)
    nt = pl.num_programs(0)
    slot = t % 2

    @pl.when(t >= 2)
    def _():
        _block_copies_wait(totblk[t - 2], y.at[slot], x_hbm, sem.at[slot])

    rows = lax.broadcasted_iota(jnp.int32, (DISP_ROWS, DISP_R), 0)
    hit = jnp.where(rows == d_ref[0, 0:1, :], 1.0, jnp.where(rows == d_ref[0, 1:2, :], 1.0, 0.0))
    y[slot] = _dot(hit.astype(_bf16), u_ref[...].astype(_bf16))
    _block_copies_start(nblk, t, y.at[slot], sblk, x_hbm, gblk, sem.at[slot])

    @pl.when(t == nt - 1)
    def _():
        @pl.when(t >= 1)
        def _():
            _block_copies_wait(totblk[jnp.maximum(t - 1, 0)], y.at[1 - slot], x_hbm, sem.at[1 - slot])

        _block_copies_wait(totblk[t], y.at[slot], x_hbm, sem.at[slot])
        zbuf[...] = jnp.zeros_like(zbuf)
        for e in range(N_EXPERTS + 1):
            tb = tail_blk[e]

            def zero_body(j, carry):
                d0 = pl.multiple_of((tb + j) * ROW_BLK, ROW_BLK)
                pltpu.make_async_copy(zbuf, x_hbm.at[pl.ds(d0, ROW_BLK), :], zsem).start()
                return carry

            lax.fori_loop(0, tail_n[e], zero_body, 0)
        for e in range(N_EXPERTS + 1):
            _block_copies_wait(tail_n[e], zbuf, x_hbm, zsem)


def _moe_dispatch(plan, u, n_rows):
    n, d = u.shape
    nt = n // DISP_R
    grid_spec = pltpu.PrefetchScalarGridSpec(
        num_scalar_prefetch=6,
        grid=(nt,),
        in_specs=[
            pl.BlockSpec((1, 2, DISP_R), lambda t, *_: (t, 0, 0)),
            pl.BlockSpec((DISP_R, d), lambda t, *_: (t, 0)),
        ],
        out_specs=pl.BlockSpec(memory_space=pl.ANY),
        scratch_shapes=[
            pltpu.VMEM((2, DISP_ROWS, d), _f32),
            pltpu.SemaphoreType.DMA((2,)),
            pltpu.VMEM((ROW_BLK, d), _f32),
            pltpu.SemaphoreType.DMA(()),
        ],
    )
    return pl.pallas_call(
        _dispatch_kernel,
        grid_spec=grid_spec,
        out_shape=jax.ShapeDtypeStruct((n_rows, d), _f32),
        compiler_params=_cparams(("arbitrary",), VMEM_LIMIT),
        name="moe_dispatch",
    )(plan["gblk"], plan["nblk"], plan["sblk"], plan["totblk"], plan["tail_blk"], plan["tail_n"], plan["d"], u)


def _moe_kernel(te_ref, nu_ref, x_ref, wg_ref, wu_ref, wd_ref, o_ref, acc):
    i = pl.program_id(0)
    n_used = nu_ref[0]

    @pl.when(i < n_used)
    def _():
        x = x_ref[...].astype(_bf16)
        for c in range(D_FF_EXPERT // MOE_FF_CHUNK):
            cols = slice(c * MOE_FF_CHUNK, (c + 1) * MOE_FF_CHUNK)
            a = _dot(x, wg_ref[0, :, cols])
            mid = (_silu(a) * _dot(x, wu_ref[0, :, cols])).astype(_bf16)
            part = _dot(mid, wd_ref[0, cols, :])
            if c == 0:
                acc[...] = part
            else:
                acc[...] += part
        o_ref[...] = acc[...]

    @pl.when(i >= n_used)
    def _():
        o_ref[...] = jnp.zeros_like(o_ref)


def _moe_experts(plan, x, wg, wu, wd):
    tile_expert, n_used = plan["tile_expert"], plan["n_used"]
    n_tiles = tile_expert.shape[0]
    d = x.shape[1]
    w_spec = lambda shape: pl.BlockSpec(shape, lambda i, te, nu: (te[i], 0, 0), pipeline_mode=pl.Buffered(1))
    grid_spec = pltpu.PrefetchScalarGridSpec(
        num_scalar_prefetch=2,
        grid=(n_tiles,),
        in_specs=[
            pl.BlockSpec((MOE_TM, d), lambda i, te, nu: (jnp.minimum(i, nu[0] - 1), 0)),
            w_spec((1, d, D_FF_EXPERT)),
            w_spec((1, d, D_FF_EXPERT)),
            w_spec((1, D_FF_EXPERT, d)),
        ],
        out_specs=pl.BlockSpec((MOE_TM, d), lambda i, te, nu: (i, 0)),
        scratch_shapes=[pltpu.VMEM((MOE_TM, d), _f32)],
    )
    return pl.pallas_call(
        _moe_kernel,
        grid_spec=grid_spec,
        out_shape=jax.ShapeDtypeStruct((n_tiles * MOE_TM, d), _f32),
        compiler_params=_cparams(("arbitrary",), VMEM_LIMIT),
        name="moe_experts",
    )(tile_expert, n_used, x, wg, wu, wd)


def _combine_kernel(gblk, nblk, sblk, totblk, route_ref, h_ref, m_ref, g_ref, y_hbm, o_ref, z, sem):
    t = pl.program_id(0)
    nt = pl.num_programs(0)
    slot = t % 2

    @pl.when(t == 0)
    def _():
        z[...] = jnp.zeros_like(z)
        _block_copies_start(nblk, t, y_hbm, gblk, z.at[0], sblk, sem.at[0])

    @pl.when(t + 1 < nt)
    def _():
        _block_copies_start(nblk, t + 1, y_hbm, gblk, z.at[1 - slot], sblk, sem.at[1 - slot])

    _block_copies_wait(totblk[t], y_hbm, z.at[slot], sem.at[slot])
    r = route_ref[...]
    col = lax.broadcasted_iota(jnp.int32, (DISP_R, DISP_ROWS), 1).astype(_f32)
    w = jnp.where(col == r[:, 4:5], r[:, 0:1], jnp.where(col == r[:, 5:6], r[:, 1:2], 0.0))
    f = _dot(w.astype(_bf16), z[slot].astype(_bf16))
    hn = h_ref[...] + m_ref[0][5:6] * f
    o_ref[...] = hn * lax.rsqrt(jnp.mean(hn * hn, axis=-1, keepdims=True) + EPS) * g_ref[...]


def _moe_combine(plan, route, h, mods, g, y_sorted, tokens_per_batch):
    n, d = h.shape
    nt = n // DISP_R
    per_b = tokens_per_batch // DISP_R
    grid_spec = pltpu.PrefetchScalarGridSpec(
        num_scalar_prefetch=4,
        grid=(nt,),
        in_specs=[
            pl.BlockSpec((DISP_R, ROUTE_LANES), lambda t, *_: (t, 0)),
            pl.BlockSpec((DISP_R, d), lambda t, *_: (t, 0)),
            pl.BlockSpec((1, 6, d), lambda t, *_: (t // per_b, 0, 0)),
            pl.BlockSpec((1, d), lambda t, *_: (0, 0)),
            pl.BlockSpec(memory_space=pl.ANY),
        ],
        out_specs=pl.BlockSpec((DISP_R, d), lambda t, *_: (t, 0)),
        scratch_shapes=[
            pltpu.VMEM((2, DISP_ROWS, d), _f32),
            pltpu.SemaphoreType.DMA((2,)),
        ],
    )
    return pl.pallas_call(
        _combine_kernel,
        grid_spec=grid_spec,
        out_shape=jax.ShapeDtypeStruct((n, d), _f32),
        compiler_params=_cparams(("arbitrary",), VMEM_LIMIT),
        name="moe_combine",
    )(plan["gblk"], plan["nblk"], plan["sblk"], plan["totblk"], route, h, mods, g, y_sorted)


def _routing_plan(counts, rows):
    nt = counts.shape[0]
    n = nt * DISP_R
    i32 = jnp.int32
    cnt = counts[:, :, 0].astype(i32)
    cpad = (cnt + ROW_BLK - 1) // ROW_BLK * ROW_BLK
    seg = jnp.cumsum(cpad, axis=1) - cpad
    rows_e = jnp.sum(cpad, axis=0)
    tiles_e = (rows_e + MOE_TM - 1) // MOE_TM
    tile_end = jnp.cumsum(tiles_e)
    off = (tile_end - tiles_e) * MOE_TM
    glob = off[None, :] + jnp.cumsum(cpad, axis=0) - cpad
    n_tiles = -(-(2 * n + nt * N_EXPERTS * (ROW_BLK - 1)) // MOE_TM) + N_EXPERTS
    n_used = tile_end[-1]
    ti = jnp.minimum(jnp.arange(n_tiles, dtype=i32), n_used - 1)
    tile_expert = jnp.sum((ti[:, None] >= tile_end[None, :]).astype(i32), axis=1)
    nblk = cpad // ROW_BLK
    return {
        "d": rows,
        "gblk": (glob // ROW_BLK).reshape(-1).astype(i32),
        "nblk": nblk.reshape(-1).astype(i32),
        "sblk": (seg // ROW_BLK).reshape(-1).astype(i32),
        "totblk": jnp.sum(nblk, axis=1).astype(i32),
        "tail_blk": jnp.append((off + rows_e) // ROW_BLK, n_used * (MOE_TM // ROW_BLK)).astype(i32),
        "tail_n": jnp.append((tiles_e * MOE_TM - rows_e) // ROW_BLK,
                             (n_tiles - n_used) * (MOE_TM // ROW_BLK)).astype(i32),
        "tile_expert": tile_expert.astype(i32),
        "n_used": n_used.reshape(1).astype(i32),
        "n_rows": n_tiles * MOE_TM,
    }


def _permuted_w_in(w_in_l):
    d = w_in_l.shape[0]
    c = {}
    start = 0
    for name, size in (("na_qkv", 3 * NA_WIDTH), ("conv_glu", 2 * CONV_CH), ("ret_q", RET_QK_WIDTH),
                       ("ret_k", RET_QK_WIDTH), ("ret_v", RET_WIDTH), ("ret_g", 2 * RET_WIDTH)):
        c[name] = w_in_l[:, start:start + size]
        start += size
    sub = RET_QK_DIM // 2
    halves_first = lambda w: w.reshape(d, RET_HEADS, 2, sub).transpose(0, 2, 1, 3).reshape(d, RET_QK_WIDTH)
    wa = jnp.concatenate([c["na_qkv"], c["ret_v"]], axis=1)
    wb = jnp.concatenate([c["conv_glu"], halves_first(c["ret_q"]), halves_first(c["ret_k"]), c["ret_g"]], axis=1)
    return wa.astype(_bf16), wb.astype(_bf16)


def kernel(x, c, ctx, c_ctx, w_mod, b_mod, norm1_w, norm2_w, w_in, w_out, na_rpb, conv_w, conv_b, conv_ln_w,
           conv_ln_b, ret_decay, ret_gn_w, ffn_w_gate, ffn_w_up, ffn_w_down, moe_router, moe_router_b,
           moe_w_gate, moe_w_up, moe_w_down, final_norm_w):
    b, t, d = x.shape
    lc = ctx.shape[1]
    assert d == D_MODEL and t % Q_TILE == 0 and t // GRID_W >= K_ROWS and b + 1 <= MOD_ROWS
    assert t % 512 == 0 and lc % 256 == 0 and (2 * b * t) % MOE_TM == 0

    cvecs = jnp.zeros((MOD_ROWS, d), _f32).at[:b].set(c).at[b].set(c_ctx)
    mods = _mod_vectors(cvecs, w_mod, b_mod).reshape(DEPTH, MOD_ROWS, 6, d)
    cos_lat, sin_lat = _rope_tables(t)
    cos_ctx = jnp.ones((lc, RET_QK_WIDTH), _f32)
    sin_ctx = jnp.zeros((lc, RET_QK_WIDTH), _f32)
    vec = lambda a: a.reshape(1, -1)
    lat_tm = 512
    ctx_tm = 256

    h_lat, h_ctx = x, ctx
    out = None
    for l in range(DEPTH):
        last = l == DEPTH - 1
        m_lat = mods[l, :b]
        m_ctx = mods[l, b:b + 1]
        wa, wb = _permuted_w_in(w_in[l])
        wo = w_out[l].astype(_bf16)
        gamma = 1.0 - jnp.exp2(-ret_decay[l].astype(_f32))
        lg = jnp.log(gamma)
        lgq = jnp.tile(jnp.repeat(lg, RET_QK_DIM // 2, axis=1), (1, 2)).reshape(2, 1, RET_QK_WIDTH)
        lgv = jnp.repeat(lg, RET_V_DIM, axis=1).reshape(2, 1, RET_WIDTH)

        a_lat, b_lat = _inproj(h_lat, m_lat, vec(norm1_w[l]), wa, wb, lat_tm)
        a_ctx, b_ctx = _inproj(h_ctx, m_ctx, vec(norm1_w[l]), wa, wb, ctx_tm)

        y_na = _na_attention(a_lat, a_ctx, _na_bias_table(na_rpb[l]))
        y_conv = _conv_module(b_lat, conv_w[l], conv_b[l], conv_ln_w[l], conv_ln_b[l])
        s_zero = jnp.zeros((b, 2, RET_QK_WIDTH, RET_WIDTH), _f32)
        y_ret_c, s_ctx = _retention(a_ctx, b_ctx, cos_ctx, sin_ctx, s_zero, lg, lgq, lgv, ret_gn_w[l])
        y_ret, _ = _retention(a_lat, b_lat, cos_lat, sin_lat, s_ctx, lg, lgq, lgv, ret_gn_w[l])

        if not last:
            y_na_c = _ctx_attention(a_ctx)
            y_conv_c = _conv_module(b_ctx, conv_w[l], conv_b[l], conv_ln_w[l], conv_ln_b[l])
            h_ctx, u_ctx = _outproj(y_na_c, y_conv_c, y_ret_c, h_ctx, wo, m_ctx, vec(norm2_w[l]), ctx_tm)

        if l % 2 == 0:
            h_lat, u_lat = _outproj(y_na, y_conv, y_ret, h_lat, wo, m_lat, vec(norm2_w[l]), lat_tm)
            wg = ffn_w_gate[l // 2].astype(_bf16)
            wu = ffn_w_up[l // 2].astype(_bf16)
            wd = ffn_w_down[l // 2].astype(_bf16)
            h_lat = _ffn(u_lat, h_lat, m_lat, wg, wu, wd, lat_tm)
            if not last:
                h_ctx = _ffn(u_ctx, h_ctx, m_ctx, wg, wu, wd, ctx_tm)
            out = h_lat
        else:
            assert last
            rw = jnp.zeros((d, ROUTE_LANES), _f32).at[:, :N_EXPERTS].set(moe_router[l // 2])
            rb = jnp.full((1, ROUTE_LANES), NEG_INF, _f32).at[0, :N_EXPERTS].set(moe_router_b[l // 2])
            h_lat, u_lat, route, rows, counts = _outproj(y_na, y_conv, y_ret, h_lat, wo, m_lat,
                                                         vec(norm2_w[l]), lat_tm, router=(rw, rb))
            route = route.reshape(b * t, ROUTE_LANES)
            plan = _routing_plan(counts, rows)
            x_sorted = _moe_dispatch(plan, u_lat.reshape(b * t, d), plan["n_rows"])
            y_sorted = _moe_experts(plan, x_sorted, moe_w_gate[l // 2].astype(_bf16),
                                    moe_w_up[l // 2].astype(_bf16), moe_w_down[l // 2].astype(_bf16))
            out = _moe_combine(plan, route, h_lat.reshape(b * t, d), m_lat, vec(final_norm_w), y_sorted, t)
            out = out.reshape(b, t, d)
    return out
```

```python
import functools

import numpy as np
import jax
import jax.numpy as jnp
from jax import lax
from jax.experimental import pallas as pl
from jax.experimental.pallas import tpu as pltpu

D_MODEL = 1024
DEPTH = 2
GRID_W = 64
NA_HEAD_DIM = 64
NA_WIDTH = 512
NA_HEADS = 8
NB_ROWS = 8
NB_COLS = 16
CONV_CH = 256
CONV_WIDTH = 31
RET_WIDTH = 256
RET_HEADS = 4
RET_V_DIM = 64
RET_QK_DIM = 32
RET_QK_WIDTH = 128
D_FF = 2816
N_EXPERTS = 8
D_FF_EXPERT = 3584
ROPE_BASE = 10000.0
EPS = 1e-6
NEG_INF = -1e30

A_WIDTH = 3 * NA_WIDTH + RET_WIDTH
B_WIDTH = 2 * CONV_CH + 2 * RET_QK_WIDTH + 2 * RET_WIDTH

HEAD_GROUP = 4
HG_LANES = HEAD_GROUP * NA_HEAD_DIM
Q_ROWS = 4
Q_TILE = Q_ROWS * GRID_W
K_ROWS = 12
K_TILE = K_ROWS * GRID_W
RET_CHUNK = 256
CONV_CHUNK = 128
CONV_PAD = 16
SUBLANES = 8
MOE_TM = 512
MOE_FF_CHUNK = 512
ROW_BLK = SUBLANES
BIG_SHIFT = 3
BIG_BLKS = 1 << BIG_SHIFT
DISP_R = 512
DISP_ROWS = 2 * DISP_R + N_EXPERTS * ROW_BLK
ROUTE_LANES = 128
VMEM_LIMIT = 56 * 1024 * 1024

_f32 = jnp.float32
_bf16 = jnp.bfloat16


def _cparams(sem, vmem=None):
    return pltpu.CompilerParams(dimension_semantics=sem, vmem_limit_bytes=vmem)


def _resident(shape):
    return pl.BlockSpec(shape, lambda *_: (0,) * len(shape), pipeline_mode=pl.Buffered(1))


def _sigmoid(x):
    return 1.0 / (1.0 + jnp.exp(-x))


def _silu(x):
    return x * _sigmoid(x)


def _dot(a, b):
    return jnp.dot(a, b, preferred_element_type=_f32)


def _dot_nt(a, b):
    return lax.dot_general(a, b, (((1,), (1,)), ((), ())), preferred_element_type=_f32)


def _dot_tn(a, b):
    return lax.dot_general(a, b, (((0,), (0,)), ((), ())), preferred_element_type=_f32)


def _split_bf16(a):
    hi = a.astype(_bf16)
    lo = (a - hi.astype(_f32)).astype(_bf16)
    return hi, lo


def _dot_split(a, b):
    ah, al = _split_bf16(a)
    bh, bl = _split_bf16(b)
    return _dot(ah, bh) + _dot(al, bh) + _dot(ah, bl)


def _rms_modulate(x, g, shift, scale):
    y = x * lax.rsqrt(jnp.mean(x * x, axis=-1, keepdims=True) + EPS)
    return (y * g) * (1.0 + scale) + shift


MOD_ROWS = 24
MOD_TN = 1536


def _mod_kernel(c_ref, w_ref, b_ref, o_ref):
    s = _silu(c_ref[...])
    o_ref[0] = _dot_split(s, w_ref[0]) + b_ref[0]


def _mod_vectors(cvecs, w_mod, b_mod):
    n = w_mod.shape[2]
    return pl.pallas_call(
        _mod_kernel,
        grid=(DEPTH, n // MOD_TN),
        in_specs=[
            pl.BlockSpec((MOD_ROWS, D_MODEL), lambda l, j: (0, 0)),
            pl.BlockSpec((1, D_MODEL, MOD_TN), lambda l, j: (l, 0, j)),
            pl.BlockSpec((1, 1, MOD_TN), lambda l, j: (l, 0, j)),
        ],
        out_specs=pl.BlockSpec((1, MOD_ROWS, MOD_TN), lambda l, j: (l, 0, j)),
        out_shape=jax.ShapeDtypeStruct((DEPTH, MOD_ROWS, n), _f32),
        compiler_params=_cparams(("arbitrary", "arbitrary"), VMEM_LIMIT),
        name="mod_vectors",
    )(cvecs, w_mod, b_mod.reshape(DEPTH, 1, n))


def _inproj_kernel(x_ref, m_ref, g_ref, wa_ref, wb_ref, oa_ref, ob_ref):
    m = m_ref[0]
    u = _rms_modulate(x_ref[0], g_ref[...], m[0:1], m[1:2]).astype(_bf16)
    oa_ref[0] = _dot(u, wa_ref[...]).astype(_bf16)
    ob_ref[0] = _dot(u, wb_ref[...])


def _inproj(x, mods, g, wa, wb, tm):
    b, l, d = x.shape
    nb = mods.shape[0]
    mod_map = (lambda i, j: (i, 0, 0)) if nb > 1 else (lambda i, j: (0, 0, 0))
    return pl.pallas_call(
        _inproj_kernel,
        grid=(b, l // tm),
        in_specs=[
            pl.BlockSpec((1, tm, d), lambda i, j: (i, j, 0)),
            pl.BlockSpec((1, 6, d), mod_map),
            pl.BlockSpec((1, d), lambda i, j: (0, 0)),
            _resident((d, A_WIDTH)),
            _resident((d, B_WIDTH)),
        ],
        out_specs=[
            pl.BlockSpec((1, tm, A_WIDTH), lambda i, j: (i, j, 0)),
            pl.BlockSpec((1, tm, B_WIDTH), lambda i, j: (i, j, 0)),
        ],
        out_shape=[
            jax.ShapeDtypeStruct((b, l, A_WIDTH), _bf16),
            jax.ShapeDtypeStruct((b, l, B_WIDTH), _f32),
        ],
        compiler_params=_cparams(("arbitrary", "arbitrary"), VMEM_LIMIT),
        name="inproj",
    )(x, mods, g, wa, wb)


def _masked_heads_attention(q, keys, vals, n_biased, bias, lane):
    out = jnp.zeros((q.shape[0], HG_LANES), _f32)
    for h in range(HEAD_GROUP):
        hm = (lane >= h * NA_HEAD_DIM) & (lane < (h + 1) * NA_HEAD_DIM)
        qm = jnp.where(hm, q, jnp.zeros_like(q)) * jnp.asarray(NA_HEAD_DIM ** -0.5, q.dtype)
        s = _dot_nt(qm, keys)
        parts = [s[:, :n_biased] + bias(h), s[:, n_biased:]] if n_biased else [s]
        mx = parts[0].max(axis=-1, keepdims=True)
        for part in parts[1:]:
            mx = jnp.maximum(mx, part.max(axis=-1, keepdims=True))
        probs = [jnp.exp(part - mx) for part in parts]
        den = probs[0].sum(axis=-1, keepdims=True)
        for p in probs[1:]:
            den = den + p.sum(axis=-1, keepdims=True)
        p = jnp.concatenate([p.astype(_bf16) for p in probs], axis=-1)
        out = jnp.where(hm, _dot(p, vals) / den, out)
    return out


def _na_kernel(q_ref, k_ref, v_ref, kc_ref, vc_ref, bias_ref, o_ref, kall, vall):
    n_tiles = q_ref.shape[1] // Q_TILE
    rows = q_ref.shape[1] // GRID_W
    lane = lax.broadcasted_iota(jnp.int32, (1, HG_LANES), 1)
    kall[K_TILE:, :] = kc_ref[0]
    vall[K_TILE:, :] = vc_ref[0]

    def body(g, carry):
        krow0 = jnp.clip(Q_ROWS * g - NB_ROWS // 2, 0, rows - K_ROWS)
        start = pl.multiple_of(krow0 * GRID_W, GRID_W)
        cls = jnp.where(g == 0, 0, jnp.where(g == n_tiles - 1, 2, 1))
        qs = pl.multiple_of(g * Q_TILE, Q_TILE)
        q = q_ref[0, pl.ds(qs, Q_TILE), :]
        kall[0:K_TILE, :] = k_ref[0, pl.ds(start, K_TILE), :]
        vall[0:K_TILE, :] = v_ref[0, pl.ds(start, K_TILE), :]
        out = _masked_heads_attention(q, kall[...], vall[...], K_TILE, lambda h: bias_ref[cls, h], lane)
        o_ref[0, pl.ds(qs, Q_TILE), :] = out.astype(o_ref.dtype)
        return carry

    lax.fori_loop(0, n_tiles, body, 0, unroll=2)


def _na_attention(a_lat, a_ctx, bias):
    b, t, _ = a_lat.shape
    lc = a_ctx.shape[1]
    ng = NA_HEADS // HEAD_GROUP
    kq, kk, kv = 0, NA_WIDTH // HG_LANES, 2 * NA_WIDTH // HG_LANES
    return pl.pallas_call(
        _na_kernel,
        grid=(ng, b),
        in_specs=[
            pl.BlockSpec((1, t, HG_LANES), lambda g, i: (i, 0, kq + g)),
            pl.BlockSpec((1, t, HG_LANES), lambda g, i: (i, 0, kk + g)),
            pl.BlockSpec((1, t, HG_LANES), lambda g, i: (i, 0, kv + g)),
            pl.BlockSpec((1, lc, HG_LANES), lambda g, i: (i, 0, kk + g)),
            pl.BlockSpec((1, lc, HG_LANES), lambda g, i: (i, 0, kv + g)),
            pl.BlockSpec((3, HEAD_GROUP, Q_TILE, K_TILE), lambda g, i: (0, g, 0, 0)),
        ],
        out_specs=pl.BlockSpec((1, t, HG_LANES), lambda g, i: (i, 0, g)),
        out_shape=jax.ShapeDtypeStruct((b, t, NA_WIDTH), _bf16),
        scratch_shapes=[pltpu.VMEM((K_TILE + lc, HG_LANES), _bf16),
                        pltpu.VMEM((K_TILE + lc, HG_LANES), _bf16)],
        compiler_params=_cparams(("arbitrary", "arbitrary"), VMEM_LIMIT),
        name="na_attention",
    )(a_lat, a_lat, a_lat, a_ctx, a_ctx, bias)


def _ctx_attn_kernel(q_ref, k_ref, v_ref, o_ref):
    lane = lax.broadcasted_iota(jnp.int32, (1, HG_LANES), 1)
    out = _masked_heads_attention(q_ref[0], k_ref[0], v_ref[0], 0, None, lane)
    o_ref[0] = out.astype(o_ref.dtype)


def _ctx_attention(a_ctx):
    b, lc, _ = a_ctx.shape
    ng = NA_HEADS // HEAD_GROUP
    kq, kk, kv = 0, NA_WIDTH // HG_LANES, 2 * NA_WIDTH // HG_LANES
    return pl.pallas_call(
        _ctx_attn_kernel,
        grid=(ng, b),
        in_specs=[
            pl.BlockSpec((1, lc, HG_LANES), lambda g, i: (i, 0, kq + g)),
            pl.BlockSpec((1, lc, HG_LANES), lambda g, i: (i, 0, kk + g)),
            pl.BlockSpec((1, lc, HG_LANES), lambda g, i: (i, 0, kv + g)),
        ],
        out_specs=pl.BlockSpec((1, lc, HG_LANES), lambda g, i: (i, 0, g)),
        out_shape=jax.ShapeDtypeStruct((b, lc, NA_WIDTH), _bf16),
        compiler_params=_cparams(("arbitrary", "arbitrary")),
        name="ctx_attention",
    )(a_ctx, a_ctx, a_ctx)


def _na_bias_table(rpb):
    h = rpb.shape[0]
    pad = GRID_W - NB_COLS
    rp = jnp.pad(rpb.astype(_f32), ((0, 0), (0, 0), (pad, pad)))
    toep = jnp.stack([rp[:, :, GRID_W - 1 - c: 2 * GRID_W - 1 - c] for c in range(GRID_W)], axis=2)
    c = np.arange(GRID_W)
    wc0 = np.clip(c - NB_COLS // 2, 0, GRID_W - NB_COLS)
    col_ok = (c[None, :] >= wc0[:, None]) & (c[None, :] < wc0[:, None] + NB_COLS)
    toep = jnp.where(col_ok[None, None], toep, NEG_INF)
    masked = jnp.full((h, GRID_W, GRID_W), NEG_INF, _f32)
    classes = ((lambda i: 0, NB_ROWS - 1),
               (lambda i: i, NB_ROWS - 1 - NB_ROWS // 2),
               (lambda i: K_ROWS - NB_ROWS, NB_ROWS - 1 - (K_ROWS - Q_ROWS)))
    tables = []
    for off, dr0 in classes:
        rows = []
        for i in range(Q_ROWS):
            blocks = [toep[:, j - i + dr0] if off(i) <= j < off(i) + NB_ROWS else masked for j in range(K_ROWS)]
            rows.append(jnp.concatenate(blocks, axis=-1))
        tables.append(jnp.stack(rows, axis=1).reshape(h, Q_TILE, K_TILE))
    return jnp.stack(tables, axis=0)


def _conv_kernel(u_ref, w_ref, b_ref, lnw_ref, lnb_ref, o_ref, ypad):
    l = u_ref.shape[1]
    ypad[0:CONV_PAD, :] = jnp.zeros((CONV_PAD, CONV_CH), _f32)
    ypad[CONV_PAD + l:2 * CONV_PAD + l, :] = jnp.zeros((CONV_PAD, CONV_CH), _f32)
    ypad[CONV_PAD:CONV_PAD + l, :] = u_ref[0, :, 0:CONV_CH] * _sigmoid(u_ref[0, :, CONV_CH:2 * CONV_CH])
    shift = CONV_PAD - CONV_WIDTH // 2

    def body(c, carry):
        base = pl.multiple_of(c * CONV_CHUNK, CONV_CHUNK)
        win = ypad[pl.ds(base, CONV_CHUNK + 2 * CONV_PAD), :]
        acc = jnp.zeros((CONV_CHUNK, CONV_CH), _f32)
        for r in range(SUBLANES):
            offs = [o for o in range(shift, shift + CONV_WIDTH) if o % SUBLANES == r]
            wr = win if r == 0 else pltpu.roll(win, win.shape[0] - r, 0)
            for o in offs:
                acc = acc + wr[o - r:o - r + CONV_CHUNK, :] * w_ref[o - shift:o - shift + 1, :]
        y = acc + b_ref[...]
        mu = jnp.mean(y, axis=-1, keepdims=True)
        yc = y - mu
        var = jnp.mean(yc * yc, axis=-1, keepdims=True)
        z = yc * lax.rsqrt(var + EPS) * lnw_ref[...] + lnb_ref[...]
        o_ref[0, pl.ds(base, CONV_CHUNK), :] = _silu(z).astype(o_ref.dtype)
        return carry

    lax.fori_loop(0, l // CONV_CHUNK, body, 0)


def _conv_module(bf, conv_w, conv_b, ln_w, ln_b):
    b, l, _ = bf.shape
    vec = lambda a: a.reshape(1, CONV_CH)
    return pl.pallas_call(
        _conv_kernel,
        grid=(b,),
        in_specs=[
            pl.BlockSpec((1, l, 2 * CONV_CH), lambda i: (i, 0, 0)),
            pl.BlockSpec((CONV_WIDTH, CONV_CH), lambda i: (0, 0)),
            pl.BlockSpec((1, CONV_CH), lambda i: (0, 0)),
            pl.BlockSpec((1, CONV_CH), lambda i: (0, 0)),
            pl.BlockSpec((1, CONV_CH), lambda i: (0, 0)),
        ],
        out_specs=pl.BlockSpec((1, l, CONV_CH), lambda i: (i, 0, 0)),
        out_shape=jax.ShapeDtypeStruct((b, l, CONV_CH), _bf16),
        scratch_shapes=[pltpu.VMEM((l + 2 * CONV_PAD, CONV_CH), _f32)],
        compiler_params=_cparams(("arbitrary",), VMEM_LIMIT),
        name="conv_module",
    )(bf, conv_w, vec(conv_b), vec(ln_w), vec(ln_b))


def _ret_kernel(lg_ref, q_ref, k_ref, v_ref, gf_ref, gb_ref, cos_ref, sin_ref, s0_ref, lgq_ref, lgv_ref,
                gnw_ref, y_ref, sfin_ref, dmat, of_s, ob_s, st_s):
    l = q_ref.shape[1]
    c = min(RET_CHUNK, l)
    nc = l // c
    half = RET_QK_WIDTH // 2
    sub = RET_QK_DIM // 2
    ii = lax.broadcasted_iota(jnp.int32, (c, c), 0)
    jj = lax.broadcasted_iota(jnp.int32, (c, c), 1)
    diff = (ii - jj).astype(_f32)
    for h in range(RET_HEADS):
        dmat[0, h] = jnp.where(diff >= 0, jnp.exp(lg_ref[0, h] * jnp.maximum(diff, 0.0)), 0.0)
        dmat[1, h] = jnp.where(diff <= 0, jnp.exp(lg_ref[1, h] * jnp.maximum(-diff, 0.0)), 0.0)
    pos = lax.broadcasted_iota(jnp.int32, (c, 1), 0).astype(_f32)
    lane_q = lax.broadcasted_iota(jnp.int32, (1, RET_QK_WIDTH), 1)
    head_q = (lane_q % half) // sub
    lane_v = lax.broadcasted_iota(jnp.int32, (1, RET_WIDTH), 1)
    head_v = lane_v // RET_V_DIM
    row_h = (lax.broadcasted_iota(jnp.int32, (RET_QK_WIDTH, RET_WIDTH), 0) % half) // sub
    col_h = lax.broadcasted_iota(jnp.int32, (RET_QK_WIDTH, RET_WIDTH), 1) // RET_V_DIM
    blockmask = row_h == col_h
    q_dec = (jnp.exp(lgq_ref[0] * (pos + 1.0)), jnp.exp(lgq_ref[1] * (c - pos)))
    k_dec = (jnp.exp(lgq_ref[0] * (c - 1.0 - pos)), jnp.exp(lgq_ref[1] * pos))
    c_dec = (jnp.exp(lgv_ref[0] * float(c)), jnp.exp(lgv_ref[1] * float(c)))
    k_scale = RET_QK_DIM ** -0.5
    st_s[...] = s0_ref[0]

    def step(n, carry):
        for d in range(2):
            cidx = n if d == 0 else nc - 1 - n
            base = pl.multiple_of(cidx * c, c)
            cs = cos_ref[pl.ds(base, c), :]
            sn = sin_ref[pl.ds(base, c), :]
            q = q_ref[0, pl.ds(base, c), :]
            k = k_ref[0, pl.ds(base, c), :]
            qr = q * cs + pltpu.roll(q, half, 1) * sn
            kr = (k * cs + pltpu.roll(k, half, 1) * sn) * k_scale
            v = v_ref[0, pl.ds(base, c), :]
            qb = qr.astype(_bf16)
            kb = kr.astype(_bf16)
            o = _dot((qr * q_dec[d]).astype(_bf16), st_s[d].astype(_bf16))
            for h in range(RET_HEADS):
                s = _dot_nt(jnp.where(head_q == h, qb, jnp.zeros_like(qb)), kb)
                inner = (s * dmat[d, h]).astype(_bf16)
                o = o + _dot(inner, jnp.where(head_v == h, v, jnp.zeros_like(v)))
            if d == 0:
                of_s[pl.ds(base, c), :] = o
            else:
                ob_s[pl.ds(base, c), :] = o
            upd = _dot_tn((kr * k_dec[d]).astype(_bf16), v)
            st_s[d] = c_dec[d] * st_s[d] + jnp.where(blockmask, upd, 0.0)
        return carry

    lax.fori_loop(0, nc, step, 0)
    sfin_ref[0] = st_s[...]

    gi = lax.broadcasted_iota(jnp.int32, (RET_WIDTH, RET_WIDTH), 0) // RET_V_DIM
    gj = lax.broadcasted_iota(jnp.int32, (RET_WIDTH, RET_WIDTH), 1) // RET_V_DIM
    gmean = jnp.where(gi == gj, 1.0 / RET_V_DIM, 0.0).astype(_bf16)

    def group_mean(a):
        return _dot(a.astype(_bf16), gmean)

    def head_norm(o):
        dlt = o - group_mean(o)
        var = group_mean(dlt * dlt)
        return dlt * lax.rsqrt(var + EPS) * gnw_ref[...]

    def fin(n, carry):
        base = pl.multiple_of(n * c, c)
        yf = head_norm(of_s[pl.ds(base, c), :])
        yb = head_norm(ob_s[pl.ds(base, c), :])
        y = _silu(gf_ref[0, pl.ds(base, c), :]) * yf + _silu(gb_ref[0, pl.ds(base, c), :]) * yb
        y_ref[0, pl.ds(base, c), :] = y.astype(y_ref.dtype)
        return carry

    lax.fori_loop(0, nc, fin, 0)


def _retention(a, bf, cos_t, sin_t, s0, lg, lgq, lgv, gn_w):
    b, l, _ = a.shape
    c = min(RET_CHUNK, l)
    qi = 2 * CONV_CH // RET_QK_WIDTH
    gi = (2 * CONV_CH + 2 * RET_QK_WIDTH) // RET_WIDTH
    vi = 3 * NA_WIDTH // RET_WIDTH
    return pl.pallas_call(
        _ret_kernel,
        grid=(b,),
        in_specs=[
            pl.BlockSpec(memory_space=pltpu.SMEM),
            pl.BlockSpec((1, l, RET_QK_WIDTH), lambda i: (i, 0, qi)),
            pl.BlockSpec((1, l, RET_QK_WIDTH), lambda i: (i, 0, qi + 1)),
            pl.BlockSpec((1, l, RET_WIDTH), lambda i: (i, 0, vi)),
            pl.BlockSpec((1, l, RET_WIDTH), lambda i: (i, 0, gi)),
            pl.BlockSpec((1, l, RET_WIDTH), lambda i: (i, 0, gi + 1)),
            pl.BlockSpec((l, RET_QK_WIDTH), lambda i: (0, 0)),
            pl.BlockSpec((l, RET_QK_WIDTH), lambda i: (0, 0)),
            pl.BlockSpec((1, 2, RET_QK_WIDTH, RET_WIDTH), lambda i: (i, 0, 0, 0)),
            pl.BlockSpec((2, 1, RET_QK_WIDTH), lambda i: (0, 0, 0)),
            pl.BlockSpec((2, 1, RET_WIDTH), lambda i: (0, 0, 0)),
            pl.BlockSpec((1, RET_WIDTH), lambda i: (0, 0)),
        ],
        out_specs=[
            pl.BlockSpec((1, l, RET_WIDTH), lambda i: (i, 0, 0)),
            pl.BlockSpec((1, 2, RET_QK_WIDTH, RET_WIDTH), lambda i: (i, 0, 0, 0)),
        ],
        out_shape=[
            jax.ShapeDtypeStruct((b, l, RET_WIDTH), _bf16),
            jax.ShapeDtypeStruct((b, 2, RET_QK_WIDTH, RET_WIDTH), _f32),
        ],
        scratch_shapes=[
            pltpu.VMEM((2, RET_HEADS, c, c), _f32),
            pltpu.VMEM((l, RET_WIDTH), _f32),
            pltpu.VMEM((l, RET_WIDTH), _f32),
            pltpu.VMEM((2, RET_QK_WIDTH, RET_WIDTH), _f32),
        ],
        compiler_params=_cparams(("arbitrary",), VMEM_LIMIT),
        name="retention",
    )(lg, bf, bf, a, bf, bf, cos_t, sin_t, s0, lgq, lgv, gn_w.reshape(1, RET_WIDTH))


def _rope_tables(t_len):
    t = np.arange(t_len)
    row = (t // GRID_W).astype(np.float32)
    col = (t % GRID_W).astype(np.float32)
    axis_dim = RET_QK_DIM // 2
    inv = jnp.asarray(ROPE_BASE, _f32) ** (-jnp.arange(0, axis_dim, 2, dtype=_f32) / axis_dim)
    ang = jnp.concatenate([jnp.asarray(row)[:, None] * inv, jnp.asarray(col)[:, None] * inv], axis=-1)
    cos, sin = jnp.cos(ang), jnp.sin(ang)
    cos_t = jnp.tile(cos, (1, 2 * RET_HEADS))
    sin_t = jnp.concatenate([jnp.tile(-sin, (1, RET_HEADS)), jnp.tile(sin, (1, RET_HEADS))], axis=-1)
    return cos_t, sin_t


def _top2_route(logits):
    m = logits.shape[0]
    lt = logits.T[0:N_EXPERTS, :]
    sub = lax.broadcasted_iota(jnp.int32, lt.shape, 0).astype(_f32)
    m1 = lt.max(axis=0, keepdims=True)
    i1 = jnp.where(lt == m1, sub, float(N_EXPERTS)).min(axis=0, keepdims=True)
    rest = jnp.where(sub == i1, -jnp.inf, lt)
    m2 = rest.max(axis=0, keepdims=True)
    i2 = jnp.where(rest == m2, sub, float(N_EXPERTS)).min(axis=0, keepdims=True)
    e = jnp.exp(m2 - m1)
    p1 = 1.0 / (1.0 + e)
    p2 = e * p1
    oh1 = jnp.where(sub == i1, 1.0, 0.0)
    oh2 = jnp.where(sub == i2, 1.0, 0.0)
    cnt = oh1 + oh2
    before = (lax.broadcasted_iota(jnp.int32, (m, m), 0) < lax.broadcasted_iota(jnp.int32, (m, m), 1))
    pref = _dot(cnt.astype(_bf16), jnp.where(before, 1.0, 0.0).astype(_bf16))
    tot = cnt.sum(axis=1, keepdims=True)
    cpad = jnp.floor((tot + (ROW_BLK - 1.0)) * (1.0 / ROW_BLK)) * ROW_BLK
    d1 = (pref * oh1).sum(axis=0, keepdims=True)
    d2 = (pref * oh2).sum(axis=0, keepdims=True)
    seg = jnp.zeros((1, 1), _f32)
    for ex in range(N_EXPERTS):
        d1 = d1 + oh1[ex:ex + 1, :] * seg
        d2 = d2 + oh2[ex:ex + 1, :] * seg
        seg = seg + cpad[ex:ex + 1, :]
    rows8 = jnp.concatenate([p1, p2, i1, i2, d1, d2, jnp.zeros((ROUTE_LANES - 6, m), _f32)], axis=0)
    return (rows8.T, jnp.concatenate([d1, d2], axis=0).astype(jnp.int32),
            jnp.broadcast_to(tot, (N_EXPERTS, ROUTE_LANES)))


def _outproj_kernel(with_router, *refs):
    if with_router:
        (yna, yconv, yret, h_ref, w_ref, m_ref, g_ref, rw_ref, rb_ref,
         ho_ref, u_ref, route_ref, rows_ref, cnt_ref) = refs
    else:
        yna, yconv, yret, h_ref, w_ref, m_ref, g_ref, ho_ref, u_ref = refs
    y = jnp.concatenate([yna[0], yconv[0], yret[0]], axis=-1)
    m = m_ref[0]
    hn = h_ref[0] + m[2:3] * _dot(y, w_ref[...])
    ho_ref[0] = hn
    u = _rms_modulate(hn, g_ref[...], m[3:4], m[4:5])
    u_ref[0] = u.astype(u_ref.dtype)
    if with_router:
        route_ref[0], rows_ref[0], cnt_ref[0] = _top2_route(_dot_split(u, rw_ref[...]) + rb_ref[...])


def _outproj(y_na, y_conv, y_ret, h, w_out, mods, g2, tm, router=None):
    b, l, d = h.shape
    nb = mods.shape[0]
    mod_map = (lambda i, j: (i, 0, 0)) if nb > 1 else (lambda i, j: (0, 0, 0))
    tok = lambda w: pl.BlockSpec((1, tm, w), lambda i, j: (i, j, 0))
    in_specs = [tok(NA_WIDTH), tok(CONV_CH), tok(RET_WIDTH), tok(d),
                _resident((d, d)),
                pl.BlockSpec((1, 6, d), mod_map),
                pl.BlockSpec((1, d), lambda i, j: (0, 0))]
    args = [y_na, y_conv, y_ret, h, w_out, mods, g2]
    out_specs = [tok(d), tok(d)]
    if router is None:
        out_shape = [jax.ShapeDtypeStruct((b, l, d), _f32), jax.ShapeDtypeStruct((b, l, d), _bf16)]
    else:
        in_specs += [pl.BlockSpec((d, ROUTE_LANES), lambda i, j: (0, 0)),
                     pl.BlockSpec((1, ROUTE_LANES), lambda i, j: (0, 0))]
        args += list(router)
        assert tm == DISP_R
        nj = l // tm
        out_specs += [tok(ROUTE_LANES),
                      pl.BlockSpec((1, 2, tm), lambda i, j: (i * nj + j, 0, 0)),
                      pl.BlockSpec((1, N_EXPERTS, ROUTE_LANES), lambda i, j: (i * nj + j, 0, 0))]
        out_shape = [jax.ShapeDtypeStruct((b, l, d), _f32), jax.ShapeDtypeStruct((b, l, d), _bf16),
                     jax.ShapeDtypeStruct((b, l, ROUTE_LANES), _f32),
                     jax.ShapeDtypeStruct((b * nj, 2, tm), jnp.int32),
                     jax.ShapeDtypeStruct((b * nj, N_EXPERTS, ROUTE_LANES), _f32)]
    return pl.pallas_call(
        functools.partial(_outproj_kernel, router is not None),
        grid=(b, l // tm),
        in_specs=in_specs,
        out_specs=out_specs,
        out_shape=out_shape,
        compiler_params=_cparams(("arbitrary", "arbitrary"), VMEM_LIMIT),
        name="outproj",
    )(*args)


def _ffn_kernel(u_ref, h_ref, m_ref, wg_ref, wu_ref, wd_ref, o_ref):
    u = u_ref[0]
    a = _dot(u, wg_ref[...])
    mid = (_silu(a) * _dot(u, wu_ref[...])).astype(_bf16)
    o_ref[0] = h_ref[0] + m_ref[0][5:6] * _dot(mid, wd_ref[...])


def _ffn(u, h, mods, wg, wu, wd, tm):
    b, l, d = h.shape
    nb = mods.shape[0]
    dff = wg.shape[1]
    mod_map = (lambda i, j: (i, 0, 0)) if nb > 1 else (lambda i, j: (0, 0, 0))
    tok = lambda: pl.BlockSpec((1, tm, d), lambda i, j: (i, j, 0))
    return pl.pallas_call(
        _ffn_kernel,
        grid=(b, l // tm),
        in_specs=[tok(), tok(), pl.BlockSpec((1, 6, d), mod_map),
                  _resident((d, dff)), _resident((d, dff)), _resident((dff, d))],
        out_specs=tok(),
        out_shape=jax.ShapeDtypeStruct((b, l, d), _f32),
        compiler_params=_cparams(("arbitrary", "arbitrary"), VMEM_LIMIT),
        name="ffn",
    )(u, h, mods, wg, wu, wd)


def _block_copies_start(nblk_ref, seg, src, src_blk_ref, dst, dst_blk_ref, sem):
    for e in range(N_EXPERTS):
        k = seg * N_EXPERTS + e
        sb = src_blk_ref[k]
        db = dst_blk_ref[k]
        n_big = lax.shift_right_logical(nblk_ref[k], BIG_SHIFT)
        n_small = nblk_ref[k] & (BIG_BLKS - 1)

        def copy(src_blk, dst_blk, blks):
            s0 = pl.multiple_of(src_blk * ROW_BLK, ROW_BLK)
            d0 = pl.multiple_of(dst_blk * ROW_BLK, ROW_BLK)
            rows = blks * ROW_BLK
            pltpu.make_async_copy(src.at[pl.ds(s0, rows), :], dst.at[pl.ds(d0, rows), :], sem).start()

        def big(j, carry):
            copy(sb + j * BIG_BLKS, db + j * BIG_BLKS, BIG_BLKS)
            return carry

        def small(j, carry):
            copy(sb + n_big * BIG_BLKS + j, db + n_big * BIG_BLKS + j, 1)
            return carry

        lax.fori_loop(0, n_big, big, 0)
        lax.fori_loop(0, n_small, small, 0)


def _block_copies_wait(n, src, dst, sem, blks=1):
    rows = blks * ROW_BLK

    def body(j, carry):
        pltpu.make_async_copy(src.at[pl.ds(0, rows), :], dst.at[pl.ds(0, rows), :], sem).wait()
        return carry

    lax.fori_loop(0, n, body, 0)


def _tile_copies_wait(tot_ref, seg, src, dst, sem):
    _block_copies_wait(tot_ref[2 * seg], src, dst, sem, BIG_BLKS)
    _block_copies_wait(tot_ref[2 * seg + 1], src, dst, sem)


def _dispatch_kernel(gblk, nblk, sblk, totblk, tail_blk, tail_n, d_ref, u_ref, x_hbm, y, sem, zbuf, zsem):
    t = pl.program_id(0)
    nt = pl.num_programs(0)
    slot = t % 2

    @pl.when(t >= 2)
    def _():
        _tile_copies_wait(totblk, t - 2, y.at[slot], x_hbm, sem.at[slot])

    rows = lax.broadcasted_iota(jnp.int32, (DISP_ROWS, DISP_R), 0)
    hit = jnp.where(rows == d_ref[0, 0:1, :], 1.0, jnp.where(rows == d_ref[0, 1:2, :], 1.0, 0.0))
    y[slot] = _dot(hit.astype(_bf16), u_ref[...].astype(_bf16))
    _block_copies_start(nblk, t, y.at[slot], sblk, x_hbm, gblk, sem.at[slot])

    @pl.when(t == nt - 1)
    def _():
        @pl.when(t >= 1)
        def _():
            _tile_copies_wait(totblk, jnp.maximum(t - 1, 0), y.at[1 - slot], x_hbm, sem.at[1 - slot])

        _tile_copies_wait(totblk, t, y.at[slot], x_hbm, sem.at[slot])
        zbuf[...] = jnp.zeros_like(zbuf)
        for e in range(N_EXPERTS + 1):
            tb = tail_blk[e]

            def zero_body(j, carry):
                d0 = pl.multiple_of((tb + j) * ROW_BLK, ROW_BLK)
                pltpu.make_async_copy(zbuf, x_hbm.at[pl.ds(d0, ROW_BLK), :], zsem).start()
                return carry

            lax.fori_loop(0, tail_n[e], zero_body, 0)
        for e in range(N_EXPERTS + 1):
            _block_copies_wait(tail_n[e], zbuf, x_hbm, zsem)


def _moe_dispatch(plan, u, n_rows):
    n, d = u.shape
    nt = n // DISP_R
    grid_spec = pltpu.PrefetchScalarGridSpec(
        num_scalar_prefetch=6,
        grid=(nt,),
        in_specs=[
            pl.BlockSpec((1, 2, DISP_R), lambda t, *_: (t, 0, 0)),
            pl.BlockSpec((DISP_R, d), lambda t, *_: (t, 0)),
        ],
        out_specs=pl.BlockSpec(memory_space=pl.ANY),
        scratch_shapes=[
            pltpu.VMEM((2, DISP_ROWS, d), _f32),
            pltpu.SemaphoreType.DMA((2,)),
            pltpu.VMEM((ROW_BLK, d), _f32),
            pltpu.SemaphoreType.DMA(()),
        ],
    )
    return pl.pallas_call(
        _dispatch_kernel,
        grid_spec=grid_spec,
        out_shape=jax.ShapeDtypeStruct((n_rows, d), _f32),
        compiler_params=_cparams(("arbitrary",), VMEM_LIMIT),
        name="moe_dispatch",
    )(plan["gblk"], plan["nblk"], plan["sblk"], plan["totblk"], plan["tail_blk"], plan["tail_n"], plan["d"], u)


def _load_expert_weights(e, wg_hbm, wu_hbm, wd_hbm, wg_s, wu_s, wd_s, stg_in, stg_out, sem):
    n_chunks = D_FF_EXPERT // MOE_FF_CHUNK
    pieces = ([(wg_hbm, wg_s, True, c) for c in range(n_chunks)]
              + [(wu_hbm, wu_s, True, c) for c in range(n_chunks)]
              + [(wd_hbm, wd_s, False, c) for c in range(n_chunks)])

    def copy(k):
        src, _, by_cols, c = pieces[k]
        span = pl.ds(c * MOE_FF_CHUNK, MOE_FF_CHUNK)
        if by_cols:
            return pltpu.make_async_copy(src.at[e, :, span], stg_in.at[k % 2], sem.at[k % 2])
        return pltpu.make_async_copy(src.at[e, span, :], stg_out.at[k % 2], sem.at[k % 2])

    copy(0).start()
    for k, (_, dst, by_cols, c) in enumerate(pieces):
        if k + 1 < len(pieces):
            copy(k + 1).start()
        copy(k).wait()
        span = slice(c * MOE_FF_CHUNK, (c + 1) * MOE_FF_CHUNK)
        if by_cols:
            dst[:, span] = stg_in[k % 2].astype(_bf16)
        else:
            dst[span, :] = stg_out[k % 2].astype(_bf16)


def _moe_kernel(te_ref, nu_ref, x_ref, wg_hbm, wu_hbm, wd_hbm, o_ref, wg_s, wu_s, wd_s, stg_in, stg_out, sem,
                acc):
    i = pl.program_id(0)
    n_used = nu_ref[0]
    e = te_ref[i]

    @pl.when((i == 0) | (e != te_ref[jnp.maximum(i - 1, 0)]))
    def _():
        _load_expert_weights(e, wg_hbm, wu_hbm, wd_hbm, wg_s, wu_s, wd_s, stg_in, stg_out, sem)

    @pl.when(i < n_used)
    def _():
        x = x_ref[...].astype(_bf16)
        for c in range(D_FF_EXPERT // MOE_FF_CHUNK):
            cols = slice(c * MOE_FF_CHUNK, (c + 1) * MOE_FF_CHUNK)
            a = _dot(x, wg_s[:, cols])
            mid = (_silu(a) * _dot(x, wu_s[:, cols])).astype(_bf16)
            part = _dot(mid, wd_s[cols, :])
            if c == 0:
                acc[...] = part
            else:
                acc[...] += part
        o_ref[...] = acc[...]

    @pl.when(i >= n_used)
    def _():
        o_ref[...] = jnp.zeros_like(o_ref)


def _moe_experts(plan, x, wg, wu, wd):
    tile_expert, n_used = plan["tile_expert"], plan["n_used"]
    n_tiles = tile_expert.shape[0]
    d = x.shape[1]
    grid_spec = pltpu.PrefetchScalarGridSpec(
        num_scalar_prefetch=2,
        grid=(n_tiles,),
        in_specs=[
            pl.BlockSpec((MOE_TM, d), lambda i, te, nu: (jnp.minimum(i, nu[0] - 1), 0)),
            pl.BlockSpec(memory_space=pl.ANY),
            pl.BlockSpec(memory_space=pl.ANY),
            pl.BlockSpec(memory_space=pl.ANY),
        ],
        out_specs=pl.BlockSpec((MOE_TM, d), lambda i, te, nu: (i, 0)),
        scratch_shapes=[
            pltpu.VMEM((d, D_FF_EXPERT), _bf16),
            pltpu.VMEM((d, D_FF_EXPERT), _bf16),
            pltpu.VMEM((D_FF_EXPERT, d), _bf16),
            pltpu.VMEM((2, d, MOE_FF_CHUNK), _f32),
            pltpu.VMEM((2, MOE_FF_CHUNK, d), _f32),
            pltpu.SemaphoreType.DMA((2,)),
            pltpu.VMEM((MOE_TM, d), _f32),
        ],
    )
    return pl.pallas_call(
        _moe_kernel,
        grid_spec=grid_spec,
        out_shape=jax.ShapeDtypeStruct((n_tiles * MOE_TM, d), _f32),
        compiler_params=_cparams(("arbitrary",), VMEM_LIMIT),
        name="moe_experts",
    )(tile_expert, n_used, x, wg, wu, wd)


def _combine_kernel(gblk, nblk, sblk, totblk, route_ref, h_ref, m_ref, g_ref, y_hbm, o_ref, z, sem):
    t = pl.program_id(0)
    nt = pl.num_programs(0)
    slot = t % 2

    @pl.when(t == 0)
    def _():
        z[...] = jnp.zeros_like(z)
        _block_copies_start(nblk, t, y_hbm, gblk, z.at[0], sblk, sem.at[0])

    @pl.when(t + 1 < nt)
    def _():
        _block_copies_start(nblk, t + 1, y_hbm, gblk, z.at[1 - slot], sblk, sem.at[1 - slot])

    _tile_copies_wait(totblk, t, y_hbm, z.at[slot], sem.at[slot])
    r = route_ref[...]
    col = lax.broadcasted_iota(jnp.int32, (DISP_R, DISP_ROWS), 1).astype(_f32)
    w = jnp.where(col == r[:, 4:5], r[:, 0:1], jnp.where(col == r[:, 5:6], r[:, 1:2], 0.0))
    f = _dot(w.astype(_bf16), z[slot].astype(_bf16))
    hn = h_ref[...] + m_ref[0][5:6] * f
    o_ref[...] = hn * lax.rsqrt(jnp.mean(hn * hn, axis=-1, keepdims=True) + EPS) * g_ref[...]


def _moe_combine(plan, route, h, mods, g, y_sorted, tokens_per_batch):
    n, d = h.shape
    nt = n // DISP_R
    per_b = tokens_per_batch // DISP_R
    grid_spec = pltpu.PrefetchScalarGridSpec(
        num_scalar_prefetch=4,
        grid=(nt,),
        in_specs=[
            pl.BlockSpec((DISP_R, ROUTE_LANES), lambda t, *_: (t, 0)),
            pl.BlockSpec((DISP_R, d), lambda t, *_: (t, 0)),
            pl.BlockSpec((1, 6, d), lambda t, *_: (t // per_b, 0, 0)),
            pl.BlockSpec((1, d), lambda t, *_: (0, 0)),
            pl.BlockSpec(memory_space=pl.ANY),
        ],
        out_specs=pl.BlockSpec((DISP_R, d), lambda t, *_: (t, 0)),
        scratch_shapes=[
            pltpu.VMEM((2, DISP_ROWS, d), _f32),
            pltpu.SemaphoreType.DMA((2,)),
        ],
    )
    return pl.pallas_call(
        _combine_kernel,
        grid_spec=grid_spec,
        out_shape=jax.ShapeDtypeStruct((n, d), _f32),
        compiler_params=_cparams(("arbitrary",), VMEM_LIMIT),
        name="moe_combine",
    )(plan["gblk"], plan["nblk"], plan["sblk"], plan["totblk"], route, h, mods, g, y_sorted)


def _routing_plan(counts, rows):
    nt = counts.shape[0]
    n = nt * DISP_R
    i32 = jnp.int32
    cnt = counts[:, :, 0].astype(i32)
    cpad = (cnt + ROW_BLK - 1) // ROW_BLK * ROW_BLK
    seg = jnp.cumsum(cpad, axis=1) - cpad
    rows_e = jnp.sum(cpad, axis=0)
    tiles_e = (rows_e + MOE_TM - 1) // MOE_TM
    tile_end = jnp.cumsum(tiles_e)
    off = (tile_end - tiles_e) * MOE_TM
    glob = off[None, :] + jnp.cumsum(cpad, axis=0) - cpad
    n_tiles = -(-(2 * n + nt * N_EXPERTS * (ROW_BLK - 1)) // MOE_TM) + N_EXPERTS
    n_used = tile_end[-1]
    ti = jnp.minimum(jnp.arange(n_tiles, dtype=i32), n_used - 1)
    tile_expert = jnp.sum((ti[:, None] >= tile_end[None, :]).astype(i32), axis=1)
    nblk = cpad // ROW_BLK
    return {
        "d": rows,
        "gblk": (glob // ROW_BLK).reshape(-1).astype(i32),
        "nblk": nblk.reshape(-1).astype(i32),
        "sblk": (seg // ROW_BLK).reshape(-1).astype(i32),
        "totblk": jnp.stack([jnp.sum(nblk // BIG_BLKS, axis=1), jnp.sum(nblk % BIG_BLKS, axis=1)],
                            axis=1).reshape(-1).astype(i32),
        "tail_blk": jnp.append((off + rows_e) // ROW_BLK, n_used * (MOE_TM // ROW_BLK)).astype(i32),
        "tail_n": jnp.append((tiles_e * MOE_TM - rows_e) // ROW_BLK,
                             (n_tiles - n_used) * (MOE_TM // ROW_BLK)).astype(i32),
        "tile_expert": tile_expert.astype(i32),
        "n_used": n_used.reshape(1).astype(i32),
        "n_rows": n_tiles * MOE_TM,
    }


def _permuted_w_in(w_in_l):
    d = w_in_l.shape[0]
    c = {}
    start = 0
    for name, size in (("na_qkv", 3 * NA_WIDTH), ("conv_glu", 2 * CONV_CH), ("ret_q", RET_QK_WIDTH),
                       ("ret_k", RET_QK_WIDTH), ("ret_v", RET_WIDTH), ("ret_g", 2 * RET_WIDTH)):
        c[name] = w_in_l[:, start:start + size]
        start += size
    sub = RET_QK_DIM // 2
    halves_first = lambda w: w.reshape(d, RET_HEADS, 2, sub).transpose(0, 2, 1, 3).reshape(d, RET_QK_WIDTH)
    wa = jnp.concatenate([c["na_qkv"], c["ret_v"]], axis=1)
    wb = jnp.concatenate([c["conv_glu"], halves_first(c["ret_q"]), halves_first(c["ret_k"]), c["ret_g"]], axis=1)
    return wa.astype(_bf16), wb.astype(_bf16)


def kernel(x, c, ctx, c_ctx, w_mod, b_mod, norm1_w, norm2_w, w_in, w_out, na_rpb, conv_w, conv_b, conv_ln_w,
           conv_ln_b, ret_decay, ret_gn_w, ffn_w_gate, ffn_w_up, ffn_w_down, moe_router, moe_router_b,
           moe_w_gate, moe_w_up, moe_w_down, final_norm_w):
    b, t, d = x.shape
    lc = ctx.shape[1]
    assert d == D_MODEL and t % Q_TILE == 0 and t // GRID_W >= K_ROWS and b + 1 <= MOD_ROWS
    assert t % 512 == 0 and lc % 256 == 0 and (2 * b * t) % MOE_TM == 0

    cvecs = jnp.zeros((MOD_ROWS, d), _f32).at[:b].set(c).at[b].set(c_ctx)
    mods = _mod_vectors(cvecs, w_mod, b_mod).reshape(DEPTH, MOD_ROWS, 6, d)
    cos_lat, sin_lat = _rope_tables(t)
    cos_ctx = jnp.ones((lc, RET_QK_WIDTH), _f32)
    sin_ctx = jnp.zeros((lc, RET_QK_WIDTH), _f32)
    vec = lambda a: a.reshape(1, -1)
    lat_tm = 512
    ctx_tm = 256

    h_lat, h_ctx = x, ctx
    out = None
    for l in range(DEPTH):
        last = l == DEPTH - 1
        m_lat = mods[l, :b]
        m_ctx = mods[l, b:b + 1]
        wa, wb = _permuted_w_in(w_in[l])
        wo = w_out[l].astype(_bf16)
        gamma = 1.0 - jnp.exp2(-ret_decay[l].astype(_f32))
        lg = jnp.log(gamma)
        lgq = jnp.tile(jnp.repeat(lg, RET_QK_DIM // 2, axis=1), (1, 2)).reshape(2, 1, RET_QK_WIDTH)
        lgv = jnp.repeat(lg, RET_V_DIM, axis=1).reshape(2, 1, RET_WIDTH)

        a_lat, b_lat = _inproj(h_lat, m_lat, vec(norm1_w[l]), wa, wb, lat_tm)
        a_ctx, b_ctx = _inproj(h_ctx, m_ctx, vec(norm1_w[l]), wa, wb, ctx_tm)

        y_na = _na_attention(a_lat, a_ctx, _na_bias_table(na_rpb[l]))
        y_conv = _conv_module(b_lat, conv_w[l], conv_b[l], conv_ln_w[l], conv_ln_b[l])
        s_zero = jnp.zeros((b, 2, RET_QK_WIDTH, RET_WIDTH), _f32)
        y_ret_c, s_ctx = _retention(a_ctx, b_ctx, cos_ctx, sin_ctx, s_zero, lg, lgq, lgv, ret_gn_w[l])
        y_ret, _ = _retention(a_lat, b_lat, cos_lat, sin_lat, s_ctx, lg, lgq, lgv, ret_gn_w[l])

        if not last:
            y_na_c = _ctx_attention(a_ctx)
            y_conv_c = _conv_module(b_ctx, conv_w[l], conv_b[l], conv_ln_w[l], conv_ln_b[l])
            h_ctx, u_ctx = _outproj(y_na_c, y_conv_c, y_ret_c, h_ctx, wo, m_ctx, vec(norm2_w[l]), ctx_tm)

        if l % 2 == 0:
            h_lat, u_lat = _outproj(y_na, y_conv, y_ret, h_lat, wo, m_lat, vec(norm2_w[l]), lat_tm)
            wg = ffn_w_gate[l // 2].astype(_bf16)
            wu = ffn_w_up[l // 2].astype(_bf16)
            wd = ffn_w_down[l // 2].astype(_bf16)
            h_lat = _ffn(u_lat, h_lat, m_lat, wg, wu, wd, lat_tm)
            if not last:
                h_ctx = _ffn(u_ctx, h_ctx, m_ctx, wg, wu, wd, ctx_tm)
            out = h_lat
        else:
            assert last
            rw = jnp.zeros((d, ROUTE_LANES), _f32).at[:, :N_EXPERTS].set(moe_router[l // 2])
            rb = jnp.full((1, ROUTE_LANES), NEG_INF, _f32).at[0, :N_EXPERTS].set(moe_router_b[l // 2])
            h_lat, u_lat, route, rows, counts = _outproj(y_na, y_conv, y_ret, h_lat, wo, m_lat,
                                                         vec(norm2_w[l]), lat_tm, router=(rw, rb))
            route = route.reshape(b * t, ROUTE_LANES)
            plan = _routing_plan(counts, rows)
            x_sorted = _moe_dispatch(plan, u_lat.reshape(b * t, d), plan["n_rows"])
            y_sorted = _moe_experts(plan, x_sorted, moe_w_gate[l // 2], moe_w_up[l // 2], moe_w_down[l // 2])
            out = _moe_combine(plan, route, h_lat.reshape(b * t, d), m_lat, vec(final_norm_w), y_sorted, t)
            out = out.reshape(b, t, d)
    return out
```

```python
import functools

import numpy as np
import jax
import jax.numpy as jnp
from jax import lax
from jax.experimental import pallas as pl
from jax.experimental.pallas import tpu as pltpu

D_MODEL = 1024
DEPTH = 2
GRID_W = 64
NA_HEAD_DIM = 64
NA_WIDTH = 512
NA_HEADS = 8
NB_ROWS = 8
NB_COLS = 16
CONV_CH = 256
CONV_WIDTH = 31
RET_WIDTH = 256
RET_HEADS = 4
RET_V_DIM = 64
RET_QK_DIM = 32
RET_QK_WIDTH = 128
D_FF = 2816
N_EXPERTS = 8
D_FF_EXPERT = 3584
ROPE_BASE = 10000.0
EPS = 1e-6
NEG_INF = -1e30

A_WIDTH = 3 * NA_WIDTH + RET_WIDTH
B_WIDTH = 2 * CONV_CH + 2 * RET_QK_WIDTH + 2 * RET_WIDTH

HEAD_GROUP = 4
HG_LANES = HEAD_GROUP * NA_HEAD_DIM
Q_ROWS = 4
Q_TILE = Q_ROWS * GRID_W
K_ROWS = 12
K_TILE = K_ROWS * GRID_W
RET_CHUNK = 256
CONV_CHUNK = 128
CONV_PAD = 16
SUBLANES = 8
LANES = 128
MOE_TM = 512
MOE_FF_CHUNK = 512
ROW_BLK = SUBLANES
BIG_SHIFT = 3
BIG_BLKS = 1 << BIG_SHIFT
DISP_R = 512
DISP_ROWS = 2 * DISP_R + N_EXPERTS * ROW_BLK
ROUTE_LANES = 128
VMEM_LIMIT = 56 * 1024 * 1024

_f32 = jnp.float32
_bf16 = jnp.bfloat16


def _cparams(sem, vmem=None):
    return pltpu.CompilerParams(dimension_semantics=sem, vmem_limit_bytes=vmem)


def _resident(shape):
    return pl.BlockSpec(shape, lambda *_: (0,) * len(shape), pipeline_mode=pl.Buffered(1))


def _sigmoid(x):
    return 1.0 / (1.0 + jnp.exp(-x))


def _silu(x):
    return x * _sigmoid(x)


def _dot(a, b):
    return jnp.dot(a, b, preferred_element_type=_f32)


def _dot_nt(a, b):
    return lax.dot_general(a, b, (((1,), (1,)), ((), ())), preferred_element_type=_f32)


def _dot_tn(a, b):
    return lax.dot_general(a, b, (((0,), (0,)), ((), ())), preferred_element_type=_f32)


def _split_bf16(a):
    hi = a.astype(_bf16)
    lo = (a - hi.astype(_f32)).astype(_bf16)
    return hi, lo


def _dot_split(a, b):
    ah, al = _split_bf16(a)
    bh, bl = _split_bf16(b)
    return _dot(ah, bh) + _dot(al, bh) + _dot(ah, bl)


def _rms_modulate(x, g, shift, scale):
    y = x * lax.rsqrt(jnp.mean(x * x, axis=-1, keepdims=True) + EPS)
    return (y * g) * (1.0 + scale) + shift


MOD_ROWS = 24
MOD_TN = 1536


def _mod_kernel(c_ref, w_ref, b_ref, o_ref):
    s = _silu(c_ref[...])
    o_ref[0] = _dot_split(s, w_ref[0]) + b_ref[0]


def _mod_vectors(cvecs, w_mod, b_mod):
    n = w_mod.shape[2]
    return pl.pallas_call(
        _mod_kernel,
        grid=(DEPTH, n // MOD_TN),
        in_specs=[
            pl.BlockSpec((MOD_ROWS, D_MODEL), lambda l, j: (0, 0)),
            pl.BlockSpec((1, D_MODEL, MOD_TN), lambda l, j: (l, 0, j)),
            pl.BlockSpec((1, 1, MOD_TN), lambda l, j: (l, 0, j)),
        ],
        out_specs=pl.BlockSpec((1, MOD_ROWS, MOD_TN), lambda l, j: (l, 0, j)),
        out_shape=jax.ShapeDtypeStruct((DEPTH, MOD_ROWS, n), _f32),
        compiler_params=_cparams(("arbitrary", "arbitrary"), VMEM_LIMIT),
        name="mod_vectors",
    )(cvecs, w_mod, b_mod.reshape(DEPTH, 1, n))


def _inproj_kernel(x_ref, m_ref, g_ref, wa_ref, wb_ref, oa_ref, ob_ref):
    m = m_ref[0]
    u = _rms_modulate(x_ref[0], g_ref[...], m[0:1], m[1:2]).astype(_bf16)
    oa_ref[0] = _dot(u, wa_ref[...]).astype(_bf16)
    ob_ref[0] = _dot(u, wb_ref[...])


def _inproj(x, mods, g, wa, wb, tm):
    b, l, d = x.shape
    nb = mods.shape[0]
    mod_map = (lambda i, j: (i, 0, 0)) if nb > 1 else (lambda i, j: (0, 0, 0))
    return pl.pallas_call(
        _inproj_kernel,
        grid=(b, l // tm),
        in_specs=[
            pl.BlockSpec((1, tm, d), lambda i, j: (i, j, 0)),
            pl.BlockSpec((1, 6, d), mod_map),
            pl.BlockSpec((1, d), lambda i, j: (0, 0)),
            _resident((d, A_WIDTH)),
            _resident((d, B_WIDTH)),
        ],
        out_specs=[
            pl.BlockSpec((1, tm, A_WIDTH), lambda i, j: (i, j, 0)),
            pl.BlockSpec((1, tm, B_WIDTH), lambda i, j: (i, j, 0)),
        ],
        out_shape=[
            jax.ShapeDtypeStruct((b, l, A_WIDTH), _bf16),
            jax.ShapeDtypeStruct((b, l, B_WIDTH), _f32),
        ],
        compiler_params=_cparams(("arbitrary", "arbitrary"), VMEM_LIMIT),
        name="inproj",
    )(x, mods, g, wa, wb)


def _masked_heads_attention(q, keys, vals, n_biased, bias, lane):
    out = jnp.zeros((q.shape[0], HG_LANES), _f32)
    for h in range(HEAD_GROUP):
        hm = (lane >= h * NA_HEAD_DIM) & (lane < (h + 1) * NA_HEAD_DIM)
        qm = jnp.where(hm, q, jnp.zeros_like(q)) * jnp.asarray(NA_HEAD_DIM ** -0.5, q.dtype)
        s = _dot_nt(qm, keys)
        parts = [s[:, :n_biased] + bias(h), s[:, n_biased:]] if n_biased else [s]
        mx = parts[0].max(axis=-1, keepdims=True)
        for part in parts[1:]:
            mx = jnp.maximum(mx, part.max(axis=-1, keepdims=True))
        probs = [jnp.exp(part - mx) for part in parts]
        den = probs[0].sum(axis=-1, keepdims=True)
        for p in probs[1:]:
            den = den + p.sum(axis=-1, keepdims=True)
        p = jnp.concatenate([p.astype(_bf16) for p in probs], axis=-1)
        out = jnp.where(hm, _dot(p, vals) / den, out)
    return out


def _fill_bias_table(rp_ref, bias_ref):
    lane = lax.broadcasted_iota(jnp.int32, (GRID_W, LANES), 1)
    qcol = lax.broadcasted_iota(jnp.int32, (GRID_W, LANES), 0)
    kcol = lane % GRID_W
    win0 = jnp.clip(qcol - NB_COLS // 2, 0, GRID_W - NB_COLS)
    col_ok = (kcol >= win0) & (kcol < win0 + NB_COLS)
    low_half = lane < GRID_W
    masked = jnp.full((GRID_W, LANES), NEG_INF, _f32)
    classes = ((lambda i: 0, NB_ROWS - 1),
               (lambda i: i, NB_ROWS - 1 - NB_ROWS // 2),
               (lambda i: K_ROWS - NB_ROWS, NB_ROWS - 1 - (K_ROWS - Q_ROWS)))
    for h in range(HEAD_GROUP):
        rolled = {}

        def block(dr, half):
            if (dr, half) not in rolled:
                row = jnp.broadcast_to(rp_ref[h, dr:dr + 1, :], (GRID_W, LANES))
                shift = (half * GRID_W - (NB_COLS - 1)) % LANES
                rolled[(dr, half)] = pltpu.roll(row, shift, 1, stride=1, stride_axis=0)
            return rolled[(dr, half)]

        for cls, (off, dr0) in enumerate(classes):
            for i in range(Q_ROWS):
                for jp in range(K_ROWS // 2):
                    parts = []
                    for half in range(2):
                        j = 2 * jp + half
                        ok = off(i) <= j < off(i) + NB_ROWS
                        parts.append(block(j - i + dr0, half) if ok else masked)
                    tile = jnp.where(col_ok, jnp.where(low_half, parts[0], parts[1]), NEG_INF)
                    bias_ref[cls, h, i * GRID_W:(i + 1) * GRID_W, jp * LANES:(jp + 1) * LANES] = tile


def _na_kernel(q_ref, k_ref, v_ref, kc_ref, vc_ref, rp_ref, o_ref, kall, vall, bias_ref):
    n_tiles = q_ref.shape[1] // Q_TILE
    rows = q_ref.shape[1] // GRID_W
    lane = lax.broadcasted_iota(jnp.int32, (1, HG_LANES), 1)

    @pl.when(pl.program_id(1) == 0)
    def _():
        _fill_bias_table(rp_ref, bias_ref)

    kall[K_TILE:, :] = kc_ref[0]
    vall[K_TILE:, :] = vc_ref[0]

    def body(g, carry):
        krow0 = jnp.clip(Q_ROWS * g - NB_ROWS // 2, 0, rows - K_ROWS)
        start = pl.multiple_of(krow0 * GRID_W, GRID_W)
        cls = jnp.where(g == 0, 0, jnp.where(g == n_tiles - 1, 2, 1))
        qs = pl.multiple_of(g * Q_TILE, Q_TILE)
        q = q_ref[0, pl.ds(qs, Q_TILE), :]
        kall[0:K_TILE, :] = k_ref[0, pl.ds(start, K_TILE), :]
        vall[0:K_TILE, :] = v_ref[0, pl.ds(start, K_TILE), :]
        out = _masked_heads_attention(q, kall[...], vall[...], K_TILE, lambda h: bias_ref[cls, h], lane)
        o_ref[0, pl.ds(qs, Q_TILE), :] = out.astype(o_ref.dtype)
        return carry

    lax.fori_loop(0, n_tiles, body, 0, unroll=2)


def _na_attention(a_lat, a_ctx, rpb):
    b, t, _ = a_lat.shape
    lc = a_ctx.shape[1]
    ng = NA_HEADS // HEAD_GROUP
    kq, kk, kv = 0, NA_WIDTH // HG_LANES, 2 * NA_WIDTH // HG_LANES
    rp = jnp.pad(rpb.astype(_f32), ((0, 0), (0, 1), (0, LANES - rpb.shape[2])))
    return pl.pallas_call(
        _na_kernel,
        grid=(ng, b),
        in_specs=[
            pl.BlockSpec((1, t, HG_LANES), lambda g, i: (i, 0, kq + g)),
            pl.BlockSpec((1, t, HG_LANES), lambda g, i: (i, 0, kk + g)),
            pl.BlockSpec((1, t, HG_LANES), lambda g, i: (i, 0, kv + g)),
            pl.BlockSpec((1, lc, HG_LANES), lambda g, i: (i, 0, kk + g)),
            pl.BlockSpec((1, lc, HG_LANES), lambda g, i: (i, 0, kv + g)),
            pl.BlockSpec((HEAD_GROUP, 2 * NB_ROWS, LANES), lambda g, i: (g, 0, 0)),
        ],
        out_specs=pl.BlockSpec((1, t, HG_LANES), lambda g, i: (i, 0, g)),
        out_shape=jax.ShapeDtypeStruct((b, t, NA_WIDTH), _bf16),
        scratch_shapes=[pltpu.VMEM((K_TILE + lc, HG_LANES), _bf16),
                        pltpu.VMEM((K_TILE + lc, HG_LANES), _bf16),
                        pltpu.VMEM((3, HEAD_GROUP, Q_TILE, K_TILE), _f32)],
        compiler_params=_cparams(("arbitrary", "arbitrary"), VMEM_LIMIT),
        name="na_attention",
    )(a_lat, a_lat, a_lat, a_ctx, a_ctx, rp)


def _ctx_attn_kernel(q_ref, k_ref, v_ref, o_ref):
    lane = lax.broadcasted_iota(jnp.int32, (1, HG_LANES), 1)
    out = _masked_heads_attention(q_ref[0], k_ref[0], v_ref[0], 0, None, lane)
    o_ref[0] = out.astype(o_ref.dtype)


def _ctx_attention(a_ctx):
    b, lc, _ = a_ctx.shape
    ng = NA_HEADS // HEAD_GROUP
    kq, kk, kv = 0, NA_WIDTH // HG_LANES, 2 * NA_WIDTH // HG_LANES
    return pl.pallas_call(
        _ctx_attn_kernel,
        grid=(ng, b),
        in_specs=[
            pl.BlockSpec((1, lc, HG_LANES), lambda g, i: (i, 0, kq + g)),
            pl.BlockSpec((1, lc, HG_LANES), lambda g, i: (i, 0, kk + g)),
            pl.BlockSpec((1, lc, HG_LANES), lambda g, i: (i, 0, kv + g)),
        ],
        out_specs=pl.BlockSpec((1, lc, HG_LANES), lambda g, i: (i, 0, g)),
        out_shape=jax.ShapeDtypeStruct((b, lc, NA_WIDTH), _bf16),
        compiler_params=_cparams(("arbitrary", "arbitrary")),
        name="ctx_attention",
    )(a_ctx, a_ctx, a_ctx)


def _conv_kernel(u_ref, w_ref, b_ref, lnw_ref, lnb_ref, o_ref, ypad):
    l = u_ref.shape[1]
    ypad[0:CONV_PAD, :] = jnp.zeros((CONV_PAD, CONV_CH), _f32)
    ypad[CONV_PAD + l:2 * CONV_PAD + l, :] = jnp.zeros((CONV_PAD, CONV_CH), _f32)
    ypad[CONV_PAD:CONV_PAD + l, :] = u_ref[0, :, 0:CONV_CH] * _sigmoid(u_ref[0, :, CONV_CH:2 * CONV_CH])
    shift = CONV_PAD - CONV_WIDTH // 2

    def body(c, carry):
        base = pl.multiple_of(c * CONV_CHUNK, CONV_CHUNK)
        win = ypad[pl.ds(base, CONV_CHUNK + 2 * CONV_PAD), :]
        acc = jnp.zeros((CONV_CHUNK, CONV_CH), _f32)
        for r in range(SUBLANES):
            offs = [o for o in range(shift, shift + CONV_WIDTH) if o % SUBLANES == r]
            wr = win if r == 0 else pltpu.roll(win, win.shape[0] - r, 0)
            for o in offs:
                acc = acc + wr[o - r:o - r + CONV_CHUNK, :] * w_ref[o - shift:o - shift + 1, :]
        y = acc + b_ref[...]
        mu = jnp.mean(y, axis=-1, keepdims=True)
        yc = y - mu
        var = jnp.mean(yc * yc, axis=-1, keepdims=True)
        z = yc * lax.rsqrt(var + EPS) * lnw_ref[...] + lnb_ref[...]
        o_ref[0, pl.ds(base, CONV_CHUNK), :] = _silu(z).astype(o_ref.dtype)
        return carry

    lax.fori_loop(0, l // CONV_CHUNK, body, 0)


def _conv_module(bf, conv_w, conv_b, ln_w, ln_b):
    b, l, _ = bf.shape
    vec = lambda a: a.reshape(1, CONV_CH)
    return pl.pallas_call(
        _conv_kernel,
        grid=(b,),
        in_specs=[
            pl.BlockSpec((1, l, 2 * CONV_CH), lambda i: (i, 0, 0)),
            pl.BlockSpec((CONV_WIDTH, CONV_CH), lambda i: (0, 0)),
            pl.BlockSpec((1, CONV_CH), lambda i: (0, 0)),
            pl.BlockSpec((1, CONV_CH), lambda i: (0, 0)),
            pl.BlockSpec((1, CONV_CH), lambda i: (0, 0)),
        ],
        out_specs=pl.BlockSpec((1, l, CONV_CH), lambda i: (i, 0, 0)),
        out_shape=jax.ShapeDtypeStruct((b, l, CONV_CH), _bf16),
        scratch_shapes=[pltpu.VMEM((l + 2 * CONV_PAD, CONV_CH), _f32)],
        compiler_params=_cparams(("arbitrary",), VMEM_LIMIT),
        name="conv_module",
    )(bf, conv_w, vec(conv_b), vec(ln_w), vec(ln_b))


def _ret_kernel(lg_ref, q_ref, k_ref, v_ref, gf_ref, gb_ref, cos_ref, sin_ref, s0_ref, lgq_ref, lgv_ref,
                gnw_ref, y_ref, sfin_ref, dmat, of_s, ob_s, st_s):
    l = q_ref.shape[1]
    c = min(RET_CHUNK, l)
    nc = l // c
    half = RET_QK_WIDTH // 2
    sub = RET_QK_DIM // 2
    ii = lax.broadcasted_iota(jnp.int32, (c, c), 0)
    jj = lax.broadcasted_iota(jnp.int32, (c, c), 1)
    diff = (ii - jj).astype(_f32)
    for h in range(RET_HEADS):
        dmat[0, h] = jnp.where(diff >= 0, jnp.exp(lg_ref[0, h] * jnp.maximum(diff, 0.0)), 0.0)
        dmat[1, h] = jnp.where(diff <= 0, jnp.exp(lg_ref[1, h] * jnp.maximum(-diff, 0.0)), 0.0)
    pos = lax.broadcasted_iota(jnp.int32, (c, 1), 0).astype(_f32)
    lane_q = lax.broadcasted_iota(jnp.int32, (1, RET_QK_WIDTH), 1)
    head_q = (lane_q % half) // sub
    lane_v = lax.broadcasted_iota(jnp.int32, (1, RET_WIDTH), 1)
    head_v = lane_v // RET_V_DIM
    row_h = (lax.broadcasted_iota(jnp.int32, (RET_QK_WIDTH, RET_WIDTH), 0) % half) // sub
    col_h = lax.broadcasted_iota(jnp.int32, (RET_QK_WIDTH, RET_WIDTH), 1) // RET_V_DIM
    blockmask = row_h == col_h
    q_dec = (jnp.exp(lgq_ref[0] * (pos + 1.0)), jnp.exp(lgq_ref[1] * (c - pos)))
    k_dec = (jnp.exp(lgq_ref[0] * (c - 1.0 - pos)), jnp.exp(lgq_ref[1] * pos))
    c_dec = (jnp.exp(lgv_ref[0] * float(c)), jnp.exp(lgv_ref[1] * float(c)))
    k_scale = RET_QK_DIM ** -0.5
    st_s[...] = s0_ref[0]

    def step(n, carry):
        for d in range(2):
            cidx = n if d == 0 else nc - 1 - n
            base = pl.multiple_of(cidx * c, c)
            cs = cos_ref[pl.ds(base, c), :]
            sn = sin_ref[pl.ds(base, c), :]
            q = q_ref[0, pl.ds(base, c), :]
            k = k_ref[0, pl.ds(base, c), :]
            qr = q * cs + pltpu.roll(q, half, 1) * sn
            kr = (k * cs + pltpu.roll(k, half, 1) * sn) * k_scale
            v = v_ref[0, pl.ds(base, c), :]
            qb = qr.astype(_bf16)
            kb = kr.astype(_bf16)
            o = _dot((qr * q_dec[d]).astype(_bf16), st_s[d].astype(_bf16))
            for h in range(RET_HEADS):
                s = _dot_nt(jnp.where(head_q == h, qb, jnp.zeros_like(qb)), kb)
                inner = (s * dmat[d, h]).astype(_bf16)
                o = o + _dot(inner, jnp.where(head_v == h, v, jnp.zeros_like(v)))
            if d == 0:
                of_s[pl.ds(base, c), :] = o
            else:
                ob_s[pl.ds(base, c), :] = o
            upd = _dot_tn((kr * k_dec[d]).astype(_bf16), v)
            st_s[d] = c_dec[d] * st_s[d] + jnp.where(blockmask, upd, 0.0)
        return carry

    lax.fori_loop(0, nc, step, 0)
    sfin_ref[0] = st_s[...]

    gi = lax.broadcasted_iota(jnp.int32, (RET_WIDTH, RET_WIDTH), 0) // RET_V_DIM
    gj = lax.broadcasted_iota(jnp.int32, (RET_WIDTH, RET_WIDTH), 1) // RET_V_DIM
    gmean = jnp.where(gi == gj, 1.0 / RET_V_DIM, 0.0).astype(_bf16)

    def group_mean(a):
        return _dot(a.astype(_bf16), gmean)

    def head_norm(o):
        dlt = o - group_mean(o)
        var = group_mean(dlt * dlt)
        return dlt * lax.rsqrt(var + EPS) * gnw_ref[...]

    def fin(n, carry):
        base = pl.multiple_of(n * c, c)
        yf = head_norm(of_s[pl.ds(base, c), :])
        yb = head_norm(ob_s[pl.ds(base, c), :])
        y = _silu(gf_ref[0, pl.ds(base, c), :]) * yf + _silu(gb_ref[0, pl.ds(base, c), :]) * yb
        y_ref[0, pl.ds(base, c), :] = y.astype(y_ref.dtype)
        return carry

    lax.fori_loop(0, nc, fin, 0)


def _retention(a, bf, cos_t, sin_t, s0, lg, lgq, lgv, gn_w):
    b, l, _ = a.shape
    c = min(RET_CHUNK, l)
    qi = 2 * CONV_CH // RET_QK_WIDTH
    gi = (2 * CONV_CH + 2 * RET_QK_WIDTH) // RET_WIDTH
    vi = 3 * NA_WIDTH // RET_WIDTH
    return pl.pallas_call(
        _ret_kernel,
        grid=(b,),
        in_specs=[
            pl.BlockSpec(memory_space=pltpu.SMEM),
            pl.BlockSpec((1, l, RET_QK_WIDTH), lambda i: (i, 0, qi)),
            pl.BlockSpec((1, l, RET_QK_WIDTH), lambda i: (i, 0, qi + 1)),
            pl.BlockSpec((1, l, RET_WIDTH), lambda i: (i, 0, vi)),
            pl.BlockSpec((1, l, RET_WIDTH), lambda i: (i, 0, gi)),
            pl.BlockSpec((1, l, RET_WIDTH), lambda i: (i, 0, gi + 1)),
            pl.BlockSpec((l, RET_QK_WIDTH), lambda i: (0, 0)),
            pl.BlockSpec((l, RET_QK_WIDTH), lambda i: (0, 0)),
            pl.BlockSpec((1, 2, RET_QK_WIDTH, RET_WIDTH), lambda i: (i, 0, 0, 0)),
            pl.BlockSpec((2, 1, RET_QK_WIDTH), lambda i: (0, 0, 0)),
            pl.BlockSpec((2, 1, RET_WIDTH), lambda i: (0, 0, 0)),
            pl.BlockSpec((1, RET_WIDTH), lambda i: (0, 0)),
        ],
        out_specs=[
            pl.BlockSpec((1, l, RET_WIDTH), lambda i: (i, 0, 0)),
            pl.BlockSpec((1, 2, RET_QK_WIDTH, RET_WIDTH), lambda i: (i, 0, 0, 0)),
        ],
        out_shape=[
            jax.ShapeDtypeStruct((b, l, RET_WIDTH), _bf16),
            jax.ShapeDtypeStruct((b, 2, RET_QK_WIDTH, RET_WIDTH), _f32),
        ],
        scratch_shapes=[
            pltpu.VMEM((2, RET_HEADS, c, c), _f32),
            pltpu.VMEM((l, RET_WIDTH), _f32),
            pltpu.VMEM((l, RET_WIDTH), _f32),
            pltpu.VMEM((2, RET_QK_WIDTH, RET_WIDTH), _f32),
        ],
        compiler_params=_cparams(("arbitrary",), VMEM_LIMIT),
        name="retention",
    )(lg, bf, bf, a, bf, bf, cos_t, sin_t, s0, lgq, lgv, gn_w.reshape(1, RET_WIDTH))


def _rope_tables(t_len):
    t = np.arange(t_len)
    row = (t // GRID_W).astype(np.float32)
    col = (t % GRID_W).astype(np.float32)
    axis_dim = RET_QK_DIM // 2
    inv = jnp.asarray(ROPE_BASE, _f32) ** (-jnp.arange(0, axis_dim, 2, dtype=_f32) / axis_dim)
    ang = jnp.concatenate([jnp.asarray(row)[:, None] * inv, jnp.asarray(col)[:, None] * inv], axis=-1)
    cos, sin = jnp.cos(ang), jnp.sin(ang)
    cos_t = jnp.tile(cos, (1, 2 * RET_HEADS))
    sin_t = jnp.concatenate([jnp.tile(-sin, (1, RET_HEADS)), jnp.tile(sin, (1, RET_HEADS))], axis=-1)
    return cos_t, sin_t


def _top2_route(logits):
    m = logits.shape[0]
    lt = logits.T[0:N_EXPERTS, :]
    sub = lax.broadcasted_iota(jnp.int32, lt.shape, 0).astype(_f32)
    m1 = lt.max(axis=0, keepdims=True)
    i1 = jnp.where(lt == m1, sub, float(N_EXPERTS)).min(axis=0, keepdims=True)
    rest = jnp.where(sub == i1, -jnp.inf, lt)
    m2 = rest.max(axis=0, keepdims=True)
    i2 = jnp.where(rest == m2, sub, float(N_EXPERTS)).min(axis=0, keepdims=True)
    e = jnp.exp(m2 - m1)
    p1 = 1.0 / (1.0 + e)
    p2 = e * p1
    oh1 = jnp.where(sub == i1, 1.0, 0.0)
    oh2 = jnp.where(sub == i2, 1.0, 0.0)
    cnt = oh1 + oh2
    before = (lax.broadcasted_iota(jnp.int32, (m, m), 0) < lax.broadcasted_iota(jnp.int32, (m, m), 1))
    pref = _dot(cnt.astype(_bf16), jnp.where(before, 1.0, 0.0).astype(_bf16))
    tot = cnt.sum(axis=1, keepdims=True)
    cpad = jnp.floor((tot + (ROW_BLK - 1.0)) * (1.0 / ROW_BLK)) * ROW_BLK
    d1 = (pref * oh1).sum(axis=0, keepdims=True)
    d2 = (pref * oh2).sum(axis=0, keepdims=True)
    seg = jnp.zeros((1, 1), _f32)
    for ex in range(N_EXPERTS):
        d1 = d1 + oh1[ex:ex + 1, :] * seg
        d2 = d2 + oh2[ex:ex + 1, :] * seg
        seg = seg + cpad[ex:ex + 1, :]
    rows8 = jnp.concatenate([p1, p2, i1, i2, d1, d2, jnp.zeros((ROUTE_LANES - 6, m), _f32)], axis=0)
    return (rows8.T, jnp.concatenate([d1, d2], axis=0).astype(jnp.int32),
            jnp.broadcast_to(tot, (N_EXPERTS, ROUTE_LANES)))


def _outproj_route_kernel(yna, yconv, yret, h_ref, w_ref, m_ref, g_ref, rw_ref, rb_ref,
                          ho_ref, u_ref, route_ref, rows_ref, cnt_ref):
    y = jnp.concatenate([yna[0], yconv[0], yret[0]], axis=-1)
    m = m_ref[0]
    hn = h_ref[0] + m[2:3] * _dot(y, w_ref[...])
    ho_ref[0] = hn
    u = _rms_modulate(hn, g_ref[...], m[3:4], m[4:5])
    u_ref[0] = u.astype(u_ref.dtype)
    route_ref[0], rows_ref[0], cnt_ref[0] = _top2_route(_dot_split(u, rw_ref[...]) + rb_ref[...])


def _outproj_route(y_na, y_conv, y_ret, h, w_out, mods, g2, rw, rb):
    b, l, d = h.shape
    tm = DISP_R
    nj = l // tm
    tok = lambda w: pl.BlockSpec((1, tm, w), lambda i, j: (i, j, 0))
    tile = lambda r, w: pl.BlockSpec((1, r, w), lambda i, j: (i * nj + j, 0, 0))
    return pl.pallas_call(
        _outproj_route_kernel,
        grid=(b, nj),
        in_specs=[tok(NA_WIDTH), tok(CONV_CH), tok(RET_WIDTH), tok(d), _resident((d, d)),
                  pl.BlockSpec((1, 6, d), lambda i, j: (i, 0, 0)),
                  pl.BlockSpec((1, d), lambda i, j: (0, 0)),
                  pl.BlockSpec((d, ROUTE_LANES), lambda i, j: (0, 0)),
                  pl.BlockSpec((1, ROUTE_LANES), lambda i, j: (0, 0))],
        out_specs=[tok(d), tok(d), tok(ROUTE_LANES), tile(2, tm), tile(N_EXPERTS, ROUTE_LANES)],
        out_shape=[jax.ShapeDtypeStruct((b, l, d), _f32), jax.ShapeDtypeStruct((b, l, d), _bf16),
                   jax.ShapeDtypeStruct((b, l, ROUTE_LANES), _f32),
                   jax.ShapeDtypeStruct((b * nj, 2, tm), jnp.int32),
                   jax.ShapeDtypeStruct((b * nj, N_EXPERTS, ROUTE_LANES), _f32)],
        compiler_params=_cparams(("arbitrary", "arbitrary"), VMEM_LIMIT),
        name="outproj_route",
    )(y_na, y_conv, y_ret, h, w_out, mods, g2, rw, rb)


def _outproj_ffn_kernel(yna, yconv, yret, h_ref, wo_ref, m_ref, g_ref, wg_ref, wu_ref, wd_ref, o_ref):
    y = jnp.concatenate([yna[0], yconv[0], yret[0]], axis=-1)
    m = m_ref[0]
    hn = h_ref[0] + m[2:3] * _dot(y, wo_ref[...])
    u = _rms_modulate(hn, g_ref[...], m[3:4], m[4:5]).astype(_bf16)
    a = _dot(u, wg_ref[...])
    mid = (_silu(a) * _dot(u, wu_ref[...])).astype(_bf16)
    o_ref[0] = hn + m[5:6] * _dot(mid, wd_ref[...])


def _outproj_ffn(y_na, y_conv, y_ret, h, w_out, mods, g2, wg, wu, wd, tm):
    b, l, d = h.shape
    nb = mods.shape[0]
    dff = wg.shape[1]
    mod_map = (lambda i, j: (i, 0, 0)) if nb > 1 else (lambda i, j: (0, 0, 0))
    tok = lambda w: pl.BlockSpec((1, tm, w), lambda i, j: (i, j, 0))
    return pl.pallas_call(
        _outproj_ffn_kernel,
        grid=(b, l // tm),
        in_specs=[tok(NA_WIDTH), tok(CONV_CH), tok(RET_WIDTH), tok(d), _resident((d, d)),
                  pl.BlockSpec((1, 6, d), mod_map), pl.BlockSpec((1, d), lambda i, j: (0, 0)),
                  _resident((d, dff)), _resident((d, dff)), _resident((dff, d))],
        out_specs=tok(d),
        out_shape=jax.ShapeDtypeStruct((b, l, d), _f32),
        compiler_params=_cparams(("arbitrary", "arbitrary"), VMEM_LIMIT),
        name="outproj_ffn",
    )(y_na, y_conv, y_ret, h, w_out, mods, g2, wg, wu, wd)


def _block_copies_start(nblk_ref, seg, src, src_blk_ref, dst, dst_blk_ref, sem):
    for e in range(N_EXPERTS):
        k = seg * N_EXPERTS + e
        sb = src_blk_ref[k]
        db = dst_blk_ref[k]
        n_big = lax.shift_right_logical(nblk_ref[k], BIG_SHIFT)
        n_small = nblk_ref[k] & (BIG_BLKS - 1)

        def copy(src_blk, dst_blk, blks):
            s0 = pl.multiple_of(src_blk * ROW_BLK, ROW_BLK)
            d0 = pl.multiple_of(dst_blk * ROW_BLK, ROW_BLK)
            rows = blks * ROW_BLK
            pltpu.make_async_copy(src.at[pl.ds(s0, rows), :], dst.at[pl.ds(d0, rows), :], sem).start()

        def big(j, carry):
            copy(sb + j * BIG_BLKS, db + j * BIG_BLKS, BIG_BLKS)
            return carry

        def small(j, carry):
            copy(sb + n_big * BIG_BLKS + j, db + n_big * BIG_BLKS + j, 1)
            return carry

        lax.fori_loop(0, n_big, big, 0)
        lax.fori_loop(0, n_small, small, 0)


def _block_copies_wait(n, src, dst, sem, blks=1):
    rows = blks * ROW_BLK

    def body(j, carry):
        pltpu.make_async_copy(src.at[pl.ds(0, rows), :], dst.at[pl.ds(0, rows), :], sem).wait()
        return carry

    lax.fori_loop(0, n, body, 0)


def _tile_copies_wait(tot_ref, seg, src, dst, sem):
    _block_copies_wait(tot_ref[2 * seg], src, dst, sem, BIG_BLKS)
    _block_copies_wait(tot_ref[2 * seg + 1], src, dst, sem)


def _dispatch_kernel(gblk, nblk, sblk, totblk, tail_blk, tail_n, d_ref, u_ref, x_hbm, y, sem, zbuf, zsem):
    t = pl.program_id(0)
    nt = pl.num_programs(0)
    slot = t % 2

    @pl.when(t >= 2)
    def _():
        _tile_copies_wait(totblk, t - 2, y.at[slot], x_hbm, sem.at[slot])

    rows = lax.broadcasted_iota(jnp.int32, (DISP_ROWS, DISP_R), 0)
    hit = jnp.where(rows == d_ref[0, 0:1, :], 1.0, jnp.where(rows == d_ref[0, 1:2, :], 1.0, 0.0))
    y[slot] = _dot(hit.astype(_bf16), u_ref[...].astype(_bf16))
    _block_copies_start(nblk, t, y.at[slot], sblk, x_hbm, gblk, sem.at[slot])

    @pl.when(t == nt - 1)
    def _():
        @pl.when(t >= 1)
        def _():
            _tile_copies_wait(totblk, jnp.maximum(t - 1, 0), y.at[1 - slot], x_hbm, sem.at[1 - slot])

        _tile_copies_wait(totblk, t, y.at[slot], x_hbm, sem.at[slot])
        zbuf[...] = jnp.zeros_like(zbuf)
        for e in range(N_EXPERTS + 1):
            tb = tail_blk[e]

            def zero_body(j, carry):
                d0 = pl.multiple_of((tb + j) * ROW_BLK, ROW_BLK)
                pltpu.make_async_copy(zbuf, x_hbm.at[pl.ds(d0, ROW_BLK), :], zsem).start()
                return carry

            lax.fori_loop(0, tail_n[e], zero_body, 0)
        for e in range(N_EXPERTS + 1):
            _block_copies_wait(tail_n[e], zbuf, x_hbm, zsem)


def _moe_dispatch(plan, u, n_rows):
    n, d = u.shape
    nt = n // DISP_R
    grid_spec = pltpu.PrefetchScalarGridSpec(
        num_scalar_prefetch=6,
        grid=(nt,),
        in_specs=[
            pl.BlockSpec((1, 2, DISP_R), lambda t, *_: (t, 0, 0)),
            pl.BlockSpec((DISP_R, d), lambda t, *_: (t, 0)),
        ],
        out_specs=pl.BlockSpec(memory_space=pl.ANY),
        scratch_shapes=[
            pltpu.VMEM((2, DISP_ROWS, d), _f32),
            pltpu.SemaphoreType.DMA((2,)),
            pltpu.VMEM((ROW_BLK, d), _f32),
            pltpu.SemaphoreType.DMA(()),
        ],
    )
    return pl.pallas_call(
        _dispatch_kernel,
        grid_spec=grid_spec,
        out_shape=jax.ShapeDtypeStruct((n_rows, d), _f32),
        compiler_params=_cparams(("arbitrary",), VMEM_LIMIT),
        name="moe_dispatch",
    )(plan["gblk"], plan["nblk"], plan["sblk"], plan["totblk"], plan["tail_blk"], plan["tail_n"], plan["d"], u)


def _load_expert_weights(e, wg_hbm, wu_hbm, wd_hbm, wg_s, wu_s, wd_s, stg_in, stg_out, sem):
    n_chunks = D_FF_EXPERT // MOE_FF_CHUNK
    pieces = ([(wg_hbm, wg_s, True, c) for c in range(n_chunks)]
              + [(wu_hbm, wu_s, True, c) for c in range(n_chunks)]
              + [(wd_hbm, wd_s, False, c) for c in range(n_chunks)])

    def copy(k):
        src, _, by_cols, c = pieces[k]
        span = pl.ds(c * MOE_FF_CHUNK, MOE_FF_CHUNK)
        if by_cols:
            return pltpu.make_async_copy(src.at[e, :, span], stg_in.at[k % 2], sem.at[k % 2])
        return pltpu.make_async_copy(src.at[e, span, :], stg_out.at[k % 2], sem.at[k % 2])

    copy(0).start()
    for k, (_, dst, by_cols, c) in enumerate(pieces):
        if k + 1 < len(pieces):
            copy(k + 1).start()
        copy(k).wait()
        span = slice(c * MOE_FF_CHUNK, (c + 1) * MOE_FF_CHUNK)
        if by_cols:
            dst[:, span] = stg_in[k % 2].astype(_bf16)
        else:
            dst[span, :] = stg_out[k % 2].astype(_bf16)


def _moe_kernel(te_ref, nu_ref, x_ref, wg_hbm, wu_hbm, wd_hbm, o_ref, wg_s, wu_s, wd_s, stg_in, stg_out, sem,
                acc):
    i = pl.program_id(0)
    n_used = nu_ref[0]
    e = te_ref[i]

    @pl.when((i == 0) | (e != te_ref[jnp.maximum(i - 1, 0)]))
    def _():
        _load_expert_weights(e, wg_hbm, wu_hbm, wd_hbm, wg_s, wu_s, wd_s, stg_in, stg_out, sem)

    @pl.when(i < n_used)
    def _():
        x = x_ref[...].astype(_bf16)
        for c in range(D_FF_EXPERT // MOE_FF_CHUNK):
            cols = slice(c * MOE_FF_CHUNK, (c + 1) * MOE_FF_CHUNK)
            a = _dot(x, wg_s[:, cols])
            mid = (_silu(a) * _dot(x, wu_s[:, cols])).astype(_bf16)
            part = _dot(mid, wd_s[cols, :])
            if c == 0:
                acc[...] = part
            else:
                acc[...] += part
        o_ref[...] = acc[...]

    @pl.when(i >= n_used)
    def _():
        o_ref[...] = jnp.zeros_like(o_ref)


def _moe_experts(plan, x, wg, wu, wd):
    tile_expert, n_used = plan["tile_expert"], plan["n_used"]
    n_tiles = tile_expert.shape[0]
    d = x.shape[1]
    grid_spec = pltpu.PrefetchScalarGridSpec(
        num_scalar_prefetch=2,
        grid=(n_tiles,),
        in_specs=[
            pl.BlockSpec((MOE_TM, d), lambda i, te, nu: (jnp.minimum(i, nu[0] - 1), 0)),
            pl.BlockSpec(memory_space=pl.ANY),
            pl.BlockSpec(memory_space=pl.ANY),
            pl.BlockSpec(memory_space=pl.ANY),
        ],
        out_specs=pl.BlockSpec((MOE_TM, d), lambda i, te, nu: (i, 0)),
        scratch_shapes=[
            pltpu.VMEM((d, D_FF_EXPERT), _bf16),
            pltpu.VMEM((d, D_FF_EXPERT), _bf16),
            pltpu.VMEM((D_FF_EXPERT, d), _bf16),
            pltpu.VMEM((2, d, MOE_FF_CHUNK), _f32),
            pltpu.VMEM((2, MOE_FF_CHUNK, d), _f32),
            pltpu.SemaphoreType.DMA((2,)),
            pltpu.VMEM((MOE_TM, d), _f32),
        ],
    )
    return pl.pallas_call(
        _moe_kernel,
        grid_spec=grid_spec,
        out_shape=jax.ShapeDtypeStruct((n_tiles * MOE_TM, d), _f32),
        compiler_params=_cparams(("arbitrary",), VMEM_LIMIT),
        name="moe_experts",
    )(tile_expert, n_used, x, wg, wu, wd)


def _combine_kernel(gblk, nblk, sblk, totblk, route_ref, h_ref, m_ref, g_ref, y_hbm, o_ref, z, sem):
    t = pl.program_id(0)
    nt = pl.num_programs(0)
    slot = t % 2

    @pl.when(t == 0)
    def _():
        z[...] = jnp.zeros_like(z)
        _block_copies_start(nblk, t, y_hbm, gblk, z.at[0], sblk, sem.at[0])

    @pl.when(t + 1 < nt)
    def _():
        _block_copies_start(nblk, t + 1, y_hbm, gblk, z.at[1 - slot], sblk, sem.at[1 - slot])

    _tile_copies_wait(totblk, t, y_hbm, z.at[slot], sem.at[slot])
    r = route_ref[...]
    col = lax.broadcasted_iota(jnp.int32, (DISP_R, DISP_ROWS), 1).astype(_f32)
    w = jnp.where(col == r[:, 4:5], r[:, 0:1], jnp.where(col == r[:, 5:6], r[:, 1:2], 0.0))
    f = _dot(w.astype(_bf16), z[slot].astype(_bf16))
    hn = h_ref[...] + m_ref[0][5:6] * f
    o_ref[...] = hn * lax.rsqrt(jnp.mean(hn * hn, axis=-1, keepdims=True) + EPS) * g_ref[...]


def _moe_combine(plan, route, h, mods, g, y_sorted, tokens_per_batch):
    n, d = h.shape
    nt = n // DISP_R
    per_b = tokens_per_batch // DISP_R
    grid_spec = pltpu.PrefetchScalarGridSpec(
        num_scalar_prefetch=4,
        grid=(nt,),
        in_specs=[
            pl.BlockSpec((DISP_R, ROUTE_LANES), lambda t, *_: (t, 0)),
            pl.BlockSpec((DISP_R, d), lambda t, *_: (t, 0)),
            pl.BlockSpec((1, 6, d), lambda t, *_: (t // per_b, 0, 0)),
            pl.BlockSpec((1, d), lambda t, *_: (0, 0)),
            pl.BlockSpec(memory_space=pl.ANY),
        ],
        out_specs=pl.BlockSpec((DISP_R, d), lambda t, *_: (t, 0)),
        scratch_shapes=[
            pltpu.VMEM((2, DISP_ROWS, d), _f32),
            pltpu.SemaphoreType.DMA((2,)),
        ],
    )
    return pl.pallas_call(
        _combine_kernel,
        grid_spec=grid_spec,
        out_shape=jax.ShapeDtypeStruct((n, d), _f32),
        compiler_params=_cparams(("arbitrary",), VMEM_LIMIT),
        name="moe_combine",
    )(plan["gblk"], plan["nblk"], plan["sblk"], plan["totblk"], route, h, mods, g, y_sorted)


def _routing_plan(counts, rows):
    nt = counts.shape[0]
    n = nt * DISP_R
    i32 = jnp.int32
    cnt = counts[:, :, 0].astype(i32)
    cpad = (cnt + ROW_BLK - 1) // ROW_BLK * ROW_BLK
    seg = jnp.cumsum(cpad, axis=1) - cpad
    rows_e = jnp.sum(cpad, axis=0)
    tiles_e = (rows_e + MOE_TM - 1) // MOE_TM
    tile_end = jnp.cumsum(tiles_e)
    off = (tile_end - tiles_e) * MOE_TM
    glob = off[None, :] + jnp.cumsum(cpad, axis=0) - cpad
    n_tiles = -(-(2 * n + nt * N_EXPERTS * (ROW_BLK - 1)) // MOE_TM) + N_EXPERTS
    n_used = tile_end[-1]
    ti = jnp.minimum(jnp.arange(n_tiles, dtype=i32), n_used - 1)
    tile_expert = jnp.sum((ti[:, None] >= tile_end[None, :]).astype(i32), axis=1)
    nblk = cpad // ROW_BLK
    return {
        "d": rows,
        "gblk": (glob // ROW_BLK).reshape(-1).astype(i32),
        "nblk": nblk.reshape(-1).astype(i32),
        "sblk": (seg // ROW_BLK).reshape(-1).astype(i32),
        "totblk": jnp.stack([jnp.sum(nblk // BIG_BLKS, axis=1), jnp.sum(nblk % BIG_BLKS, axis=1)],
                            axis=1).reshape(-1).astype(i32),
        "tail_blk": jnp.append((off + rows_e) // ROW_BLK, n_used * (MOE_TM // ROW_BLK)).astype(i32),
        "tail_n": jnp.append((tiles_e * MOE_TM - rows_e) // ROW_BLK,
                             (n_tiles - n_used) * (MOE_TM // ROW_BLK)).astype(i32),
        "tile_expert": tile_expert.astype(i32),
        "n_used": n_used.reshape(1).astype(i32),
        "n_rows": n_tiles * MOE_TM,
    }


def _permuted_w_in(w_in_l):
    d = w_in_l.shape[0]
    c = {}
    start = 0
    for name, size in (("na_qkv", 3 * NA_WIDTH), ("conv_glu", 2 * CONV_CH), ("ret_q", RET_QK_WIDTH),
                       ("ret_k", RET_QK_WIDTH), ("ret_v", RET_WIDTH), ("ret_g", 2 * RET_WIDTH)):
        c[name] = w_in_l[:, start:start + size]
        start += size
    sub = RET_QK_DIM // 2
    halves_first = lambda w: w.reshape(d, RET_HEADS, 2, sub).transpose(0, 2, 1, 3).reshape(d, RET_QK_WIDTH)
    wa = jnp.concatenate([c["na_qkv"], c["ret_v"]], axis=1)
    wb = jnp.concatenate([c["conv_glu"], halves_first(c["ret_q"]), halves_first(c["ret_k"]), c["ret_g"]], axis=1)
    return wa.astype(_bf16), wb.astype(_bf16)


def kernel(x, c, ctx, c_ctx, w_mod, b_mod, norm1_w, norm2_w, w_in, w_out, na_rpb, conv_w, conv_b, conv_ln_w,
           conv_ln_b, ret_decay, ret_gn_w, ffn_w_gate, ffn_w_up, ffn_w_down, moe_router, moe_router_b,
           moe_w_gate, moe_w_up, moe_w_down, final_norm_w):
    b, t, d = x.shape
    lc = ctx.shape[1]
    assert d == D_MODEL and t % Q_TILE == 0 and t // GRID_W >= K_ROWS and b + 1 <= MOD_ROWS
    assert t % DISP_R == 0 and lc % 256 == 0 and (2 * b * t) % MOE_TM == 0
    assert DEPTH == 2

    cvecs = jnp.zeros((MOD_ROWS, d), _f32).at[:b].set(c).at[b].set(c_ctx)
    mods = _mod_vectors(cvecs, w_mod, b_mod).reshape(DEPTH, MOD_ROWS, 6, d)
    cos_lat, sin_lat = _rope_tables(t)
    cos_ctx = jnp.ones((lc, RET_QK_WIDTH), _f32)
    sin_ctx = jnp.zeros((lc, RET_QK_WIDTH), _f32)
    vec = lambda a: a.reshape(1, -1)
    lat_tm = 512
    ctx_tm = 256

    h_lat, h_ctx = x, ctx
    out = None
    for l in range(DEPTH):
        last = l == DEPTH - 1
        m_lat = mods[l, :b]
        m_ctx = mods[l, b:b + 1]
        wa, wb = _permuted_w_in(w_in[l])
        wo = w_out[l].astype(_bf16)
        gamma = 1.0 - jnp.exp2(-ret_decay[l].astype(_f32))
        lg = jnp.log(gamma)
        lgq = jnp.tile(jnp.repeat(lg, RET_QK_DIM // 2, axis=1), (1, 2)).reshape(2, 1, RET_QK_WIDTH)
        lgv = jnp.repeat(lg, RET_V_DIM, axis=1).reshape(2, 1, RET_WIDTH)

        a_lat, b_lat = _inproj(h_lat, m_lat, vec(norm1_w[l]), wa, wb, lat_tm)
        a_ctx, b_ctx = _inproj(h_ctx, m_ctx, vec(norm1_w[l]), wa, wb, ctx_tm)

        y_na = _na_attention(a_lat, a_ctx, na_rpb[l])
        y_conv = _conv_module(b_lat, conv_w[l], conv_b[l], conv_ln_w[l], conv_ln_b[l])
        s_zero = jnp.zeros((b, 2, RET_QK_WIDTH, RET_WIDTH), _f32)
        y_ret_c, s_ctx = _retention(a_ctx, b_ctx, cos_ctx, sin_ctx, s_zero, lg, lgq, lgv, ret_gn_w[l])
        y_ret, _ = _retention(a_lat, b_lat, cos_lat, sin_lat, s_ctx, lg, lgq, lgv, ret_gn_w[l])

        if not last:
            wg = ffn_w_gate[l // 2].astype(_bf16)
            wu = ffn_w_up[l // 2].astype(_bf16)
            wd = ffn_w_down[l // 2].astype(_bf16)
            y_na_c = _ctx_attention(a_ctx)
            y_conv_c = _conv_module(b_ctx, conv_w[l], conv_b[l], conv_ln_w[l], conv_ln_b[l])
            h_ctx = _outproj_ffn(y_na_c, y_conv_c, y_ret_c, h_ctx, wo, m_ctx, vec(norm2_w[l]), wg, wu, wd, ctx_tm)
            h_lat = _outproj_ffn(y_na, y_conv, y_ret, h_lat, wo, m_lat, vec(norm2_w[l]), wg, wu, wd, lat_tm)
        else:
            rw = jnp.zeros((d, ROUTE_LANES), _f32).at[:, :N_EXPERTS].set(moe_router[l // 2])
            rb = jnp.full((1, ROUTE_LANES), NEG_INF, _f32).at[0, :N_EXPERTS].set(moe_router_b[l // 2])
            h_lat, u_lat, route, rows, counts = _outproj_route(y_na, y_conv, y_ret, h_lat, wo, m_lat,
                                                               vec(norm2_w[l]), rw, rb)
            route = route.reshape(b * t, ROUTE_LANES)
            plan = _routing_plan(counts, rows)
            x_sorted = _moe_dispatch(plan, u_lat.reshape(b * t, d), plan["n_rows"])
            y_sorted = _moe_experts(plan, x_sorted, moe_w_gate[l // 2], moe_w_up[l // 2], moe_w_down[l // 2])
            out = _moe_combine(plan, route, h_lat.reshape(b * t, d), m_lat, vec(final_norm_w), y_sorted, t)
            out = out.reshape(b, t, d)
    return out
```

```python
import functools

import numpy as np
import jax
import jax.numpy as jnp
from jax import lax
from jax.experimental import pallas as pl
from jax.experimental.pallas import tpu as pltpu

D_MODEL = 1024
DEPTH = 2
GRID_W = 64
NA_HEAD_DIM = 64
NA_WIDTH = 512
NA_HEADS = 8
NB_ROWS = 8
NB_COLS = 16
CONV_CH = 256
CONV_WIDTH = 31
RET_WIDTH = 256
RET_HEADS = 4
RET_V_DIM = 64
RET_QK_DIM = 32
RET_QK_WIDTH = 128
D_FF = 2816
N_EXPERTS = 8
D_FF_EXPERT = 3584
ROPE_BASE = 10000.0
EPS = 1e-6
NEG_INF = -1e30

A_WIDTH = 3 * NA_WIDTH + RET_WIDTH
B_WIDTH = 2 * CONV_CH + 2 * RET_QK_WIDTH + 2 * RET_WIDTH

HEAD_GROUP = 4
HG_LANES = HEAD_GROUP * NA_HEAD_DIM
Q_ROWS = 4
Q_TILE = Q_ROWS * GRID_W
K_ROWS = 12
K_TILE = K_ROWS * GRID_W
RET_CHUNK = 256
RET_UNROLL = 8
CONV_CHUNK = 128
CONV_PAD = 16
SUBLANES = 8
LANES = 128
MOE_TM = 512
MOE_FF_CHUNK = 512
ROW_BLK = SUBLANES
BIG_SHIFT = 3
BIG_BLKS = 1 << BIG_SHIFT
DISP_R = 512
DISP_ROWS = 2 * DISP_R + N_EXPERTS * ROW_BLK
ROUTE_LANES = 128
VMEM_LIMIT = 56 * 1024 * 1024

_f32 = jnp.float32
_bf16 = jnp.bfloat16


def _cparams(sem, vmem=None):
    return pltpu.CompilerParams(dimension_semantics=sem, vmem_limit_bytes=vmem)


def _resident(shape):
    return pl.BlockSpec(shape, lambda *_: (0,) * len(shape), pipeline_mode=pl.Buffered(1))


def _sigmoid(x):
    return 1.0 / (1.0 + jnp.exp(-x))


def _silu(x):
    return x * _sigmoid(x)


def _dot(a, b):
    return jnp.dot(a, b, preferred_element_type=_f32)


def _dot_nt(a, b):
    return lax.dot_general(a, b, (((1,), (1,)), ((), ())), preferred_element_type=_f32)


def _dot_tn(a, b):
    return lax.dot_general(a, b, (((0,), (0,)), ((), ())), preferred_element_type=_f32)


def _split_bf16(a):
    hi = a.astype(_bf16)
    lo = (a - hi.astype(_f32)).astype(_bf16)
    return hi, lo


def _dot_split(a, b):
    ah, al = _split_bf16(a)
    bh, bl = _split_bf16(b)
    return _dot(ah, bh) + _dot(al, bh) + _dot(ah, bl)


def _rms_modulate(x, g, shift, scale):
    y = x * lax.rsqrt(jnp.mean(x * x, axis=-1, keepdims=True) + EPS)
    return (y * g) * (1.0 + scale) + shift


MOD_ROWS = 24
MOD_TN = 1536


def _mod_kernel(c_ref, w_ref, b_ref, o_ref):
    s = _silu(c_ref[...])
    o_ref[0] = _dot_split(s, w_ref[0]) + b_ref[0]


def _mod_vectors(cvecs, w_mod, b_mod):
    n = w_mod.shape[2]
    return pl.pallas_call(
        _mod_kernel,
        grid=(DEPTH, n // MOD_TN),
        in_specs=[
            pl.BlockSpec((MOD_ROWS, D_MODEL), lambda l, j: (0, 0)),
            pl.BlockSpec((1, D_MODEL, MOD_TN), lambda l, j: (l, 0, j)),
            pl.BlockSpec((1, 1, MOD_TN), lambda l, j: (l, 0, j)),
        ],
        out_specs=pl.BlockSpec((1, MOD_ROWS, MOD_TN), lambda l, j: (l, 0, j)),
        out_shape=jax.ShapeDtypeStruct((DEPTH, MOD_ROWS, n), _f32),
        compiler_params=_cparams(("arbitrary", "arbitrary"), VMEM_LIMIT),
        name="mod_vectors",
    )(cvecs, w_mod, b_mod.reshape(DEPTH, 1, n))


def _inproj_kernel(x_ref, m_ref, g_ref, wa_ref, wb_ref, oa_ref, ob_ref):
    m = m_ref[0]
    u = _rms_modulate(x_ref[0], g_ref[...], m[0:1], m[1:2]).astype(_bf16)
    oa_ref[0] = _dot(u, wa_ref[...]).astype(_bf16)
    ob_ref[0] = _dot(u, wb_ref[...])


def _inproj(x, mods, g, wa, wb, tm):
    b, l, d = x.shape
    nb = mods.shape[0]
    mod_map = (lambda i, j: (i, 0, 0)) if nb > 1 else (lambda i, j: (0, 0, 0))
    return pl.pallas_call(
        _inproj_kernel,
        grid=(b, l // tm),
        in_specs=[
            pl.BlockSpec((1, tm, d), lambda i, j: (i, j, 0)),
            pl.BlockSpec((1, 6, d), mod_map),
            pl.BlockSpec((1, d), lambda i, j: (0, 0)),
            _resident((d, A_WIDTH)),
            _resident((d, B_WIDTH)),
        ],
        out_specs=[
            pl.BlockSpec((1, tm, A_WIDTH), lambda i, j: (i, j, 0)),
            pl.BlockSpec((1, tm, B_WIDTH), lambda i, j: (i, j, 0)),
        ],
        out_shape=[
            jax.ShapeDtypeStruct((b, l, A_WIDTH), _bf16),
            jax.ShapeDtypeStruct((b, l, B_WIDTH), _f32),
        ],
        compiler_params=_cparams(("arbitrary", "arbitrary"), VMEM_LIMIT),
        name="inproj",
    )(x, mods, g, wa, wb)


def _masked_heads_attention(q, keys, vals, n_biased, bias, lane):
    out = jnp.zeros((q.shape[0], HG_LANES), _f32)
    for h in range(HEAD_GROUP):
        hm = (lane >= h * NA_HEAD_DIM) & (lane < (h + 1) * NA_HEAD_DIM)
        qm = jnp.where(hm, q, jnp.zeros_like(q)) * jnp.asarray(NA_HEAD_DIM ** -0.5, q.dtype)
        s = _dot_nt(qm, keys)
        parts = [s[:, :n_biased] + bias(h), s[:, n_biased:]] if n_biased else [s]
        mx = parts[0].max(axis=-1, keepdims=True)
        for part in parts[1:]:
            mx = jnp.maximum(mx, part.max(axis=-1, keepdims=True))
        probs = [jnp.exp(part - mx) for part in parts]
        den = probs[0].sum(axis=-1, keepdims=True)
        for p in probs[1:]:
            den = den + p.sum(axis=-1, keepdims=True)
        p = jnp.concatenate([p.astype(_bf16) for p in probs], axis=-1)
        out = jnp.where(hm, _dot(p, vals) / den, out)
    return out


def _fill_bias_table(rp_ref, bias_ref):
    lane = lax.broadcasted_iota(jnp.int32, (GRID_W, LANES), 1)
    qcol = lax.broadcasted_iota(jnp.int32, (GRID_W, LANES), 0)
    kcol = lane % GRID_W
    win0 = jnp.clip(qcol - NB_COLS // 2, 0, GRID_W - NB_COLS)
    col_ok = (kcol >= win0) & (kcol < win0 + NB_COLS)
    low_half = lane < GRID_W
    masked = jnp.full((GRID_W, LANES), NEG_INF, _f32)
    classes = ((lambda i: 0, NB_ROWS - 1),
               (lambda i: i, NB_ROWS - 1 - NB_ROWS // 2),
               (lambda i: K_ROWS - NB_ROWS, NB_ROWS - 1 - (K_ROWS - Q_ROWS)))
    for h in range(HEAD_GROUP):
        rolled = {}

        def block(dr, half):
            if (dr, half) not in rolled:
                row = jnp.broadcast_to(rp_ref[h, dr:dr + 1, :], (GRID_W, LANES))
                shift = (half * GRID_W - (NB_COLS - 1)) % LANES
                rolled[(dr, half)] = pltpu.roll(row, shift, 1, stride=1, stride_axis=0)
            return rolled[(dr, half)]

        for cls, (off, dr0) in enumerate(classes):
            for i in range(Q_ROWS):
                for jp in range(K_ROWS // 2):
                    parts = []
                    for half in range(2):
                        j = 2 * jp + half
                        ok = off(i) <= j < off(i) + NB_ROWS
                        parts.append(block(j - i + dr0, half) if ok else masked)
                    tile = jnp.where(col_ok, jnp.where(low_half, parts[0], parts[1]), NEG_INF)
                    bias_ref[cls, h, i * GRID_W:(i + 1) * GRID_W, jp * LANES:(jp + 1) * LANES] = tile


def _na_kernel(q_ref, k_ref, v_ref, kc_ref, vc_ref, rp_ref, o_ref, kall, vall, bias_ref):
    n_tiles = q_ref.shape[1] // Q_TILE
    rows = q_ref.shape[1] // GRID_W
    lane = lax.broadcasted_iota(jnp.int32, (1, HG_LANES), 1)

    @pl.when(pl.program_id(1) == 0)
    def _():
        _fill_bias_table(rp_ref, bias_ref)

    kall[K_TILE:, :] = kc_ref[0]
    vall[K_TILE:, :] = vc_ref[0]

    def body(g, carry):
        krow0 = jnp.clip(Q_ROWS * g - NB_ROWS // 2, 0, rows - K_ROWS)
        start = pl.multiple_of(krow0 * GRID_W, GRID_W)
        cls = jnp.where(g == 0, 0, jnp.where(g == n_tiles - 1, 2, 1))
        qs = pl.multiple_of(g * Q_TILE, Q_TILE)
        q = q_ref[0, pl.ds(qs, Q_TILE), :]
        kall[0:K_TILE, :] = k_ref[0, pl.ds(start, K_TILE), :]
        vall[0:K_TILE, :] = v_ref[0, pl.ds(start, K_TILE), :]
        out = _masked_heads_attention(q, kall[...], vall[...], K_TILE, lambda h: bias_ref[cls, h], lane)
        o_ref[0, pl.ds(qs, Q_TILE), :] = out.astype(o_ref.dtype)
        return carry

    lax.fori_loop(0, n_tiles, body, 0, unroll=8)


def _na_attention(a_lat, a_ctx, rpb):
    b, t, _ = a_lat.shape
    lc = a_ctx.shape[1]
    ng = NA_HEADS // HEAD_GROUP
    kq, kk, kv = 0, NA_WIDTH // HG_LANES, 2 * NA_WIDTH // HG_LANES
    rp = jnp.pad(rpb.astype(_f32), ((0, 0), (0, 1), (0, LANES - rpb.shape[2])))
    return pl.pallas_call(
        _na_kernel,
        grid=(ng, b),
        in_specs=[
            pl.BlockSpec((1, t, HG_LANES), lambda g, i: (i, 0, kq + g)),
            pl.BlockSpec((1, t, HG_LANES), lambda g, i: (i, 0, kk + g)),
            pl.BlockSpec((1, t, HG_LANES), lambda g, i: (i, 0, kv + g)),
            pl.BlockSpec((1, lc, HG_LANES), lambda g, i: (i, 0, kk + g)),
            pl.BlockSpec((1, lc, HG_LANES), lambda g, i: (i, 0, kv + g)),
            pl.BlockSpec((HEAD_GROUP, 2 * NB_ROWS, LANES), lambda g, i: (g, 0, 0)),
        ],
        out_specs=pl.BlockSpec((1, t, HG_LANES), lambda g, i: (i, 0, g)),
        out_shape=jax.ShapeDtypeStruct((b, t, NA_WIDTH), _bf16),
        scratch_shapes=[pltpu.VMEM((K_TILE + lc, HG_LANES), _bf16),
                        pltpu.VMEM((K_TILE + lc, HG_LANES), _bf16),
                        pltpu.VMEM((3, HEAD_GROUP, Q_TILE, K_TILE), _f32)],
        compiler_params=_cparams(("arbitrary", "arbitrary"), VMEM_LIMIT),
        name="na_attention",
    )(a_lat, a_lat, a_lat, a_ctx, a_ctx, rp)


def _ctx_attn_kernel(q_ref, k_ref, v_ref, o_ref):
    lane = lax.broadcasted_iota(jnp.int32, (1, HG_LANES), 1)
    out = _masked_heads_attention(q_ref[0], k_ref[0], v_ref[0], 0, None, lane)
    o_ref[0] = out.astype(o_ref.dtype)


def _ctx_attention(a_ctx):
    b, lc, _ = a_ctx.shape
    ng = NA_HEADS // HEAD_GROUP
    kq, kk, kv = 0, NA_WIDTH // HG_LANES, 2 * NA_WIDTH // HG_LANES
    return pl.pallas_call(
        _ctx_attn_kernel,
        grid=(ng, b),
        in_specs=[
            pl.BlockSpec((1, lc, HG_LANES), lambda g, i: (i, 0, kq + g)),
            pl.BlockSpec((1, lc, HG_LANES), lambda g, i: (i, 0, kk + g)),
            pl.BlockSpec((1, lc, HG_LANES), lambda g, i: (i, 0, kv + g)),
        ],
        out_specs=pl.BlockSpec((1, lc, HG_LANES), lambda g, i: (i, 0, g)),
        out_shape=jax.ShapeDtypeStruct((b, lc, NA_WIDTH), _bf16),
        compiler_params=_cparams(("arbitrary", "arbitrary")),
        name="ctx_attention",
    )(a_ctx, a_ctx, a_ctx)


def _conv_kernel(u_ref, w_ref, b_ref, lnw_ref, lnb_ref, o_ref, ypad):
    l = u_ref.shape[1]
    ypad[0:CONV_PAD, :] = jnp.zeros((CONV_PAD, CONV_CH), _f32)
    ypad[CONV_PAD + l:2 * CONV_PAD + l, :] = jnp.zeros((CONV_PAD, CONV_CH), _f32)
    ypad[CONV_PAD:CONV_PAD + l, :] = u_ref[0, :, 0:CONV_CH] * _sigmoid(u_ref[0, :, CONV_CH:2 * CONV_CH])
    shift = CONV_PAD - CONV_WIDTH // 2

    def body(c, carry):
        base = pl.multiple_of(c * CONV_CHUNK, CONV_CHUNK)
        win = ypad[pl.ds(base, CONV_CHUNK + 2 * CONV_PAD), :]
        acc = jnp.zeros((CONV_CHUNK, CONV_CH), _f32)
        for r in range(SUBLANES):
            offs = [o for o in range(shift, shift + CONV_WIDTH) if o % SUBLANES == r]
            wr = win if r == 0 else pltpu.roll(win, win.shape[0] - r, 0)
            for o in offs:
                acc = acc + wr[o - r:o - r + CONV_CHUNK, :] * w_ref[o - shift:o - shift + 1, :]
        y = acc + b_ref[...]
        mu = jnp.mean(y, axis=-1, keepdims=True)
        yc = y - mu
        var = jnp.mean(yc * yc, axis=-1, keepdims=True)
        z = yc * lax.rsqrt(var + EPS) * lnw_ref[...] + lnb_ref[...]
        o_ref[0, pl.ds(base, CONV_CHUNK), :] = _silu(z).astype(o_ref.dtype)
        return carry

    lax.fori_loop(0, l // CONV_CHUNK, body, 0, unroll=2)


def _conv_module(bf, conv_w, conv_b, ln_w, ln_b):
    b, l, _ = bf.shape
    vec = lambda a: a.reshape(1, CONV_CH)
    return pl.pallas_call(
        _conv_kernel,
        grid=(b,),
        in_specs=[
            pl.BlockSpec((1, l, 2 * CONV_CH), lambda i: (i, 0, 0)),
            pl.BlockSpec((CONV_WIDTH, CONV_CH), lambda i: (0, 0)),
            pl.BlockSpec((1, CONV_CH), lambda i: (0, 0)),
            pl.BlockSpec((1, CONV_CH), lambda i: (0, 0)),
            pl.BlockSpec((1, CONV_CH), lambda i: (0, 0)),
        ],
        out_specs=pl.BlockSpec((1, l, CONV_CH), lambda i: (i, 0, 0)),
        out_shape=jax.ShapeDtypeStruct((b, l, CONV_CH), _bf16),
        scratch_shapes=[pltpu.VMEM((l + 2 * CONV_PAD, CONV_CH), _f32)],
        compiler_params=_cparams(("arbitrary",), VMEM_LIMIT),
        name="conv_module",
    )(bf, conv_w, vec(conv_b), vec(ln_w), vec(ln_b))


def _ret_kernel(lg_ref, q_ref, k_ref, v_ref, gf_ref, gb_ref, cos_ref, sin_ref, s0_ref, lgq_ref, lgv_ref,
                gnw_ref, y_ref, sfin_ref, dmat, of_s, ob_s, st_s):
    l = q_ref.shape[1]
    c = min(RET_CHUNK, l)
    nc = l // c
    half = RET_QK_WIDTH // 2
    sub = RET_QK_DIM // 2
    ii = lax.broadcasted_iota(jnp.int32, (c, c), 0)
    jj = lax.broadcasted_iota(jnp.int32, (c, c), 1)
    diff = (ii - jj).astype(_f32)
    for h in range(RET_HEADS):
        dmat[0, h] = jnp.where(diff >= 0, jnp.exp(lg_ref[0, h] * jnp.maximum(diff, 0.0)), 0.0)
        dmat[1, h] = jnp.where(diff <= 0, jnp.exp(lg_ref[1, h] * jnp.maximum(-diff, 0.0)), 0.0)
    pos = lax.broadcasted_iota(jnp.int32, (c, 1), 0).astype(_f32)
    lane_q = lax.broadcasted_iota(jnp.int32, (1, RET_QK_WIDTH), 1)
    head_q = (lane_q % half) // sub
    lane_v = lax.broadcasted_iota(jnp.int32, (1, RET_WIDTH), 1)
    head_v = lane_v // RET_V_DIM
    row_h = (lax.broadcasted_iota(jnp.int32, (RET_QK_WIDTH, RET_WIDTH), 0) % half) // sub
    col_h = lax.broadcasted_iota(jnp.int32, (RET_QK_WIDTH, RET_WIDTH), 1) // RET_V_DIM
    blockmask = row_h == col_h
    q_dec = (jnp.exp(lgq_ref[0] * (pos + 1.0)), jnp.exp(lgq_ref[1] * (c - pos)))
    k_dec = (jnp.exp(lgq_ref[0] * (c - 1.0 - pos)), jnp.exp(lgq_ref[1] * pos))
    c_dec = (jnp.exp(lgv_ref[0] * float(c)), jnp.exp(lgv_ref[1] * float(c)))
    k_scale = RET_QK_DIM ** -0.5
    st_s[...] = s0_ref[0]

    def step(n, carry):
        for d in range(2):
            cidx = n if d == 0 else nc - 1 - n
            base = pl.multiple_of(cidx * c, c)
            cs = cos_ref[pl.ds(base, c), :]
            sn = sin_ref[pl.ds(base, c), :]
            q = q_ref[0, pl.ds(base, c), :]
            k = k_ref[0, pl.ds(base, c), :]
            qr = q * cs + pltpu.roll(q, half, 1) * sn
            kr = (k * cs + pltpu.roll(k, half, 1) * sn) * k_scale
            v = v_ref[0, pl.ds(base, c), :]
            qb = qr.astype(_bf16)
            kb = kr.astype(_bf16)
            o = _dot((qr * q_dec[d]).astype(_bf16), st_s[d].astype(_bf16))
            for h in range(RET_HEADS):
                s = _dot_nt(jnp.where(head_q == h, qb, jnp.zeros_like(qb)), kb)
                inner = (s * dmat[d, h]).astype(_bf16)
                o = o + _dot(inner, jnp.where(head_v == h, v, jnp.zeros_like(v)))
            if d == 0:
                of_s[pl.ds(base, c), :] = o
            else:
                ob_s[pl.ds(base, c), :] = o
            upd = _dot_tn((kr * k_dec[d]).astype(_bf16), v)
            st_s[d] = c_dec[d] * st_s[d] + jnp.where(blockmask, upd, 0.0)
        return carry

    lax.fori_loop(0, nc, step, 0, unroll=RET_UNROLL if nc % RET_UNROLL == 0 else 1)
    sfin_ref[0] = st_s[...]

    gi = lax.broadcasted_iota(jnp.int32, (RET_WIDTH, RET_WIDTH), 0) // RET_V_DIM
    gj = lax.broadcasted_iota(jnp.int32, (RET_WIDTH, RET_WIDTH), 1) // RET_V_DIM
    gmean = jnp.where(gi == gj, 1.0 / RET_V_DIM, 0.0).astype(_bf16)

    def group_mean(a):
        return _dot(a.astype(_bf16), gmean)

    def head_norm(o):
        dlt = o - group_mean(o)
        var = group_mean(dlt * dlt)
        return dlt * lax.rsqrt(var + EPS) * gnw_ref[...]

    def fin(n, carry):
        base = pl.multiple_of(n * c, c)
        yf = head_norm(of_s[pl.ds(base, c), :])
        yb = head_norm(ob_s[pl.ds(base, c), :])
        y = _silu(gf_ref[0, pl.ds(base, c), :]) * yf + _silu(gb_ref[0, pl.ds(base, c), :]) * yb
        y_ref[0, pl.ds(base, c), :] = y.astype(y_ref.dtype)
        return carry

    lax.fori_loop(0, nc, fin, 0, unroll=RET_UNROLL if nc % RET_UNROLL == 0 else 1)


def _retention(a, bf, cos_t, sin_t, s0, lg, lgq, lgv, gn_w):
    b, l, _ = a.shape
    c = min(RET_CHUNK, l)
    qi = 2 * CONV_CH // RET_QK_WIDTH
    gi = (2 * CONV_CH + 2 * RET_QK_WIDTH) // RET_WIDTH
    vi = 3 * NA_WIDTH // RET_WIDTH
    return pl.pallas_call(
        _ret_kernel,
        grid=(b,),
        in_specs=[
            pl.BlockSpec(memory_space=pltpu.SMEM),
            pl.BlockSpec((1, l, RET_QK_WIDTH), lambda i: (i, 0, qi)),
            pl.BlockSpec((1, l, RET_QK_WIDTH), lambda i: (i, 0, qi + 1)),
            pl.BlockSpec((1, l, RET_WIDTH), lambda i: (i, 0, vi)),
            pl.BlockSpec((1, l, RET_WIDTH), lambda i: (i, 0, gi)),
            pl.BlockSpec((1, l, RET_WIDTH), lambda i: (i, 0, gi + 1)),
            pl.BlockSpec((l, RET_QK_WIDTH), lambda i: (0, 0)),
            pl.BlockSpec((l, RET_QK_WIDTH), lambda i: (0, 0)),
            pl.BlockSpec((1, 2, RET_QK_WIDTH, RET_WIDTH), lambda i: (i, 0, 0, 0)),
            pl.BlockSpec((2, 1, RET_QK_WIDTH), lambda i: (0, 0, 0)),
            pl.BlockSpec((2, 1, RET_WIDTH), lambda i: (0, 0, 0)),
            pl.BlockSpec((1, RET_WIDTH), lambda i: (0, 0)),
        ],
        out_specs=[
            pl.BlockSpec((1, l, RET_WIDTH), lambda i: (i, 0, 0)),
            pl.BlockSpec((1, 2, RET_QK_WIDTH, RET_WIDTH), lambda i: (i, 0, 0, 0)),
        ],
        out_shape=[
            jax.ShapeDtypeStruct((b, l, RET_WIDTH), _bf16),
            jax.ShapeDtypeStruct((b, 2, RET_QK_WIDTH, RET_WIDTH), _f32),
        ],
        scratch_shapes=[
            pltpu.VMEM((2, RET_HEADS, c, c), _f32),
            pltpu.VMEM((l, RET_WIDTH), _f32),
            pltpu.VMEM((l, RET_WIDTH), _f32),
            pltpu.VMEM((2, RET_QK_WIDTH, RET_WIDTH), _f32),
        ],
        compiler_params=_cparams(("arbitrary",), VMEM_LIMIT),
        name="retention",
    )(lg, bf, bf, a, bf, bf, cos_t, sin_t, s0, lgq, lgv, gn_w.reshape(1, RET_WIDTH))


def _rope_tables(t_len):
    t = np.arange(t_len)
    row = (t // GRID_W).astype(np.float32)
    col = (t % GRID_W).astype(np.float32)
    axis_dim = RET_QK_DIM // 2
    inv = jnp.asarray(ROPE_BASE, _f32) ** (-jnp.arange(0, axis_dim, 2, dtype=_f32) / axis_dim)
    ang = jnp.concatenate([jnp.asarray(row)[:, None] * inv, jnp.asarray(col)[:, None] * inv], axis=-1)
    cos, sin = jnp.cos(ang), jnp.sin(ang)
    cos_t = jnp.tile(cos, (1, 2 * RET_HEADS))
    sin_t = jnp.concatenate([jnp.tile(-sin, (1, RET_HEADS)), jnp.tile(sin, (1, RET_HEADS))], axis=-1)
    return cos_t, sin_t


def _top2_route(logits):
    m = logits.shape[0]
    lt = logits.T[0:N_EXPERTS, :]
    sub = lax.broadcasted_iota(jnp.int32, lt.shape, 0).astype(_f32)
    m1 = lt.max(axis=0, keepdims=True)
    i1 = jnp.where(lt == m1, sub, float(N_EXPERTS)).min(axis=0, keepdims=True)
    rest = jnp.where(sub == i1, -jnp.inf, lt)
    m2 = rest.max(axis=0, keepdims=True)
    i2 = jnp.where(rest == m2, sub, float(N_EXPERTS)).min(axis=0, keepdims=True)
    e = jnp.exp(m2 - m1)
    p1 = 1.0 / (1.0 + e)
    p2 = e * p1
    oh1 = jnp.where(sub == i1, 1.0, 0.0)
    oh2 = jnp.where(sub == i2, 1.0, 0.0)
    cnt = oh1 + oh2
    before = (lax.broadcasted_iota(jnp.int32, (m, m), 0) < lax.broadcasted_iota(jnp.int32, (m, m), 1))
    pref = _dot(cnt.astype(_bf16), jnp.where(before, 1.0, 0.0).astype(_bf16))
    tot = cnt.sum(axis=1, keepdims=True)
    cpad = jnp.floor((tot + (ROW_BLK - 1.0)) * (1.0 / ROW_BLK)) * ROW_BLK
    d1 = (pref * oh1).sum(axis=0, keepdims=True)
    d2 = (pref * oh2).sum(axis=0, keepdims=True)
    seg = jnp.zeros((1, 1), _f32)
    for ex in range(N_EXPERTS):
        d1 = d1 + oh1[ex:ex + 1, :] * seg
        d2 = d2 + oh2[ex:ex + 1, :] * seg
        seg = seg + cpad[ex:ex + 1, :]
    rows8 = jnp.concatenate([p1, p2, i1, i2, d1, d2, jnp.zeros((ROUTE_LANES - 6, m), _f32)], axis=0)
    return (rows8.T, jnp.concatenate([d1, d2], axis=0).astype(jnp.int32),
            jnp.broadcast_to(tot, (N_EXPERTS, ROUTE_LANES)))


def _outproj_route_kernel(yna, yconv, yret, h_ref, w_ref, m_ref, g_ref, rw_ref, rb_ref,
                          ho_ref, u_ref, route_ref, rows_ref, cnt_ref):
    y = jnp.concatenate([yna[0], yconv[0], yret[0]], axis=-1)
    m = m_ref[0]
    hn = h_ref[0] + m[2:3] * _dot(y, w_ref[...])
    ho_ref[0] = hn
    u = _rms_modulate(hn, g_ref[...], m[3:4], m[4:5])
    u_ref[0] = u.astype(u_ref.dtype)
    route_ref[0], rows_ref[0], cnt_ref[0] = _top2_route(_dot_split(u, rw_ref[...]) + rb_ref[...])


def _outproj_route(y_na, y_conv, y_ret, h, w_out, mods, g2, rw, rb):
    b, l, d = h.shape
    tm = DISP_R
    nj = l // tm
    tok = lambda w: pl.BlockSpec((1, tm, w), lambda i, j: (i, j, 0))
    tile = lambda r, w: pl.BlockSpec((1, r, w), lambda i, j: (i * nj + j, 0, 0))
    return pl.pallas_call(
        _outproj_route_kernel,
        grid=(b, nj),
        in_specs=[tok(NA_WIDTH), tok(CONV_CH), tok(RET_WIDTH), tok(d), _resident((d, d)),
                  pl.BlockSpec((1, 6, d), lambda i, j: (i, 0, 0)),
                  pl.BlockSpec((1, d), lambda i, j: (0, 0)),
                  pl.BlockSpec((d, ROUTE_LANES), lambda i, j: (0, 0)),
                  pl.BlockSpec((1, ROUTE_LANES), lambda i, j: (0, 0))],
        out_specs=[tok(d), tok(d), tok(ROUTE_LANES), tile(2, tm), tile(N_EXPERTS, ROUTE_LANES)],
        out_shape=[jax.ShapeDtypeStruct((b, l, d), _f32), jax.ShapeDtypeStruct((b, l, d), _bf16),
                   jax.ShapeDtypeStruct((b, l, ROUTE_LANES), _f32),
                   jax.ShapeDtypeStruct((b * nj, 2, tm), jnp.int32),
                   jax.ShapeDtypeStruct((b * nj, N_EXPERTS, ROUTE_LANES), _f32)],
        compiler_params=_cparams(("arbitrary", "arbitrary"), VMEM_LIMIT),
        name="outproj_route",
    )(y_na, y_conv, y_ret, h, w_out, mods, g2, rw, rb)


def _outproj_ffn_kernel(yna, yconv, yret, h_ref, wo_ref, m_ref, g_ref, wg_ref, wu_ref, wd_ref, o_ref):
    y = jnp.concatenate([yna[0], yconv[0], yret[0]], axis=-1)
    m = m_ref[0]
    hn = h_ref[0] + m[2:3] * _dot(y, wo_ref[...])
    u = _rms_modulate(hn, g_ref[...], m[3:4], m[4:5]).astype(_bf16)
    a = _dot(u, wg_ref[...])
    mid = (_silu(a) * _dot(u, wu_ref[...])).astype(_bf16)
    o_ref[0] = hn + m[5:6] * _dot(mid, wd_ref[...])


def _outproj_ffn(y_na, y_conv, y_ret, h, w_out, mods, g2, wg, wu, wd, tm):
    b, l, d = h.shape
    nb = mods.shape[0]
    dff = wg.shape[1]
    mod_map = (lambda i, j: (i, 0, 0)) if nb > 1 else (lambda i, j: (0, 0, 0))
    tok = lambda w: pl.BlockSpec((1, tm, w), lambda i, j: (i, j, 0))
    return pl.pallas_call(
        _outproj_ffn_kernel,
        grid=(b, l // tm),
        in_specs=[tok(NA_WIDTH), tok(CONV_CH), tok(RET_WIDTH), tok(d), _resident((d, d)),
                  pl.BlockSpec((1, 6, d), mod_map), pl.BlockSpec((1, d), lambda i, j: (0, 0)),
                  _resident((d, dff)), _resident((d, dff)), _resident((dff, d))],
        out_specs=tok(d),
        out_shape=jax.ShapeDtypeStruct((b, l, d), _f32),
        compiler_params=_cparams(("arbitrary", "arbitrary"), VMEM_LIMIT),
        name="outproj_ffn",
    )(y_na, y_conv, y_ret, h, w_out, mods, g2, wg, wu, wd)


def _block_copies_start(nblk_ref, seg, src, src_blk_ref, dst, dst_blk_ref, sem):
    for e in range(N_EXPERTS):
        k = seg * N_EXPERTS + e
        sb = src_blk_ref[k]
        db = dst_blk_ref[k]
        n_big = lax.shift_right_logical(nblk_ref[k], BIG_SHIFT)
        n_small = nblk_ref[k] & (BIG_BLKS - 1)

        def copy(src_blk, dst_blk, blks):
            s0 = pl.multiple_of(src_blk * ROW_BLK, ROW_BLK)
            d0 = pl.multiple_of(dst_blk * ROW_BLK, ROW_BLK)
            rows = blks * ROW_BLK
            pltpu.make_async_copy(src.at[pl.ds(s0, rows), :], dst.at[pl.ds(d0, rows), :], sem).start()

        def big(j, carry):
            copy(sb + j * BIG_BLKS, db + j * BIG_BLKS, BIG_BLKS)
            return carry

        def small(j, carry):
            copy(sb + n_big * BIG_BLKS + j, db + n_big * BIG_BLKS + j, 1)
            return carry

        lax.fori_loop(0, n_big, big, 0)
        lax.fori_loop(0, n_small, small, 0)


def _block_copies_wait(n, src, dst, sem, blks=1):
    rows = blks * ROW_BLK

    def body(j, carry):
        pltpu.make_async_copy(src.at[pl.ds(0, rows), :], dst.at[pl.ds(0, rows), :], sem).wait()
        return carry

    lax.fori_loop(0, n, body, 0)


def _tile_copies_wait(tot_ref, seg, src, dst, sem):
    _block_copies_wait(tot_ref[2 * seg], src, dst, sem, BIG_BLKS)
    _block_copies_wait(tot_ref[2 * seg + 1], src, dst, sem)


def _dispatch_kernel(gblk, nblk, sblk, totblk, tail_blk, tail_n, d_ref, u_ref, x_hbm, y, sem, zbuf, zsem):
    t = pl.program_id(0)
    nt = pl.num_programs(0)
    slot = t % 2

    @pl.when(t >= 2)
    def _():
        _tile_copies_wait(totblk, t - 2, y.at[slot], x_hbm, sem.at[slot])

    rows = lax.broadcasted_iota(jnp.int32, (DISP_ROWS, DISP_R), 0)
    hit = jnp.where(rows == d_ref[0, 0:1, :], 1.0, jnp.where(rows == d_ref[0, 1:2, :], 1.0, 0.0))
    y[slot] = _dot(hit.astype(_bf16), u_ref[...].astype(_bf16))
    _block_copies_start(nblk, t, y.at[slot], sblk, x_hbm, gblk, sem.at[slot])

    @pl.when(t == nt - 1)
    def _():
        @pl.when(t >= 1)
        def _():
            _tile_copies_wait(totblk, jnp.maximum(t - 1, 0), y.at[1 - slot], x_hbm, sem.at[1 - slot])

        _tile_copies_wait(totblk, t, y.at[slot], x_hbm, sem.at[slot])
        zbuf[...] = jnp.zeros_like(zbuf)
        for e in range(N_EXPERTS + 1):
            tb = tail_blk[e]

            def zero_body(j, carry):
                d0 = pl.multiple_of((tb + j) * ROW_BLK, ROW_BLK)
                pltpu.make_async_copy(zbuf, x_hbm.at[pl.ds(d0, ROW_BLK), :], zsem).start()
                return carry

            lax.fori_loop(0, tail_n[e], zero_body, 0)
        for e in range(N_EXPERTS + 1):
            _block_copies_wait(tail_n[e], zbuf, x_hbm, zsem)


def _moe_dispatch(plan, u, n_rows):
    n, d = u.shape
    nt = n // DISP_R
    grid_spec = pltpu.PrefetchScalarGridSpec(
        num_scalar_prefetch=6,
        grid=(nt,),
        in_specs=[
            pl.BlockSpec((1, 2, DISP_R), lambda t, *_: (t, 0, 0)),
            pl.BlockSpec((DISP_R, d), lambda t, *_: (t, 0)),
        ],
        out_specs=pl.BlockSpec(memory_space=pl.ANY),
        scratch_shapes=[
            pltpu.VMEM((2, DISP_ROWS, d), _f32),
            pltpu.SemaphoreType.DMA((2,)),
            pltpu.VMEM((ROW_BLK, d), _f32),
            pltpu.SemaphoreType.DMA(()),
        ],
    )
    return pl.pallas_call(
        _dispatch_kernel,
        grid_spec=grid_spec,
        out_shape=jax.ShapeDtypeStruct((n_rows, d), _f32),
        compiler_params=_cparams(("arbitrary",), VMEM_LIMIT),
        name="moe_dispatch",
    )(plan["gblk"], plan["nblk"], plan["sblk"], plan["totblk"], plan["tail_blk"], plan["tail_n"], plan["d"], u)


def _moe_kernel(te_ref, nu_ref, x_ref, wg_hbm, wu_hbm, wd_hbm, o_ref, wg_s, wu_s, wd_s, stg_g, stg_u, stg_d,
                sem, acc):
    i = pl.program_id(0)
    n_used = nu_ref[0]
    e = te_ref[i]
    n_chunks = D_FF_EXPERT // MOE_FF_CHUNK
    first_tile = (i == 0) | (e != te_ref[jnp.maximum(i - 1, 0)])

    def chunk_copies(c):
        span = pl.ds(c * MOE_FF_CHUNK, MOE_FF_CHUNK)
        s = c % 2
        return (pltpu.make_async_copy(wg_hbm.at[e, :, span], stg_g.at[s], sem.at[0, s]),
                pltpu.make_async_copy(wu_hbm.at[e, :, span], stg_u.at[s], sem.at[1, s]),
                pltpu.make_async_copy(wd_hbm.at[e, span, :], stg_d.at[s], sem.at[2, s]))

    def tile(load_weights):
        x = x_ref[...].astype(_bf16)
        if load_weights:
            for cp in chunk_copies(0):
                cp.start()
        for c in range(n_chunks):
            cols = slice(c * MOE_FF_CHUNK, (c + 1) * MOE_FF_CHUNK)
            if load_weights:
                if c + 1 < n_chunks:
                    for cp in chunk_copies(c + 1):
                        cp.start()
                for cp in chunk_copies(c):
                    cp.wait()
                wg_s[:, cols] = stg_g[c % 2].astype(_bf16)
                wu_s[:, cols] = stg_u[c % 2].astype(_bf16)
                wd_s[cols, :] = stg_d[c % 2].astype(_bf16)
            a = _dot(x, wg_s[:, cols])
            mid = (_silu(a) * _dot(x, wu_s[:, cols])).astype(_bf16)
            part = _dot(mid, wd_s[cols, :])
            if c == 0:
                acc[...] = part
            else:
                acc[...] += part
        o_ref[...] = acc[...]

    @pl.when(first_tile)
    def _():
        tile(True)

    @pl.when(jnp.logical_not(first_tile) & (i < n_used))
    def _():
        tile(False)

    @pl.when(i >= n_used)
    def _():
        o_ref[...] = jnp.zeros_like(o_ref)


def _moe_experts(plan, x, wg, wu, wd):
    tile_expert, n_used = plan["tile_expert"], plan["n_used"]
    n_tiles = tile_expert.shape[0]
    d = x.shape[1]
    grid_spec = pltpu.PrefetchScalarGridSpec(
        num_scalar_prefetch=2,
        grid=(n_tiles,),
        in_specs=[
            pl.BlockSpec((MOE_TM, d), lambda i, te, nu: (jnp.minimum(i, nu[0] - 1), 0)),
            pl.BlockSpec(memory_space=pl.ANY),
            pl.BlockSpec(memory_space=pl.ANY),
            pl.BlockSpec(memory_space=pl.ANY),
        ],
        out_specs=pl.BlockSpec((MOE_TM, d), lambda i, te, nu: (i, 0)),
        scratch_shapes=[
            pltpu.VMEM((d, D_FF_EXPERT), _bf16),
            pltpu.VMEM((d, D_FF_EXPERT), _bf16),
            pltpu.VMEM((D_FF_EXPERT, d), _bf16),
            pltpu.VMEM((2, d, MOE_FF_CHUNK), _f32),
            pltpu.VMEM((2, d, MOE_FF_CHUNK), _f32),
            pltpu.VMEM((2, MOE_FF_CHUNK, d), _f32),
            pltpu.SemaphoreType.DMA((3, 2)),
            pltpu.VMEM((MOE_TM, d), _f32),
        ],
    )
    return pl.pallas_call(
        _moe_kernel,
        grid_spec=grid_spec,
        out_shape=jax.ShapeDtypeStruct((n_tiles * MOE_TM, d), _f32),
        compiler_params=_cparams(("arbitrary",), VMEM_LIMIT),
        name="moe_experts",
    )(tile_expert, n_used, x, wg, wu, wd)


def _combine_kernel(gblk, nblk, sblk, totblk, route_ref, h_ref, m_ref, g_ref, y_hbm, o_ref, z, sem):
    t = pl.program_id(0)
    nt = pl.num_programs(0)
    slot = t % 2

    @pl.when(t == 0)
    def _():
        z[...] = jnp.zeros_like(z)
        _block_copies_start(nblk, t, y_hbm, gblk, z.at[0], sblk, sem.at[0])

    @pl.when(t + 1 < nt)
    def _():
        _block_copies_start(nblk, t + 1, y_hbm, gblk, z.at[1 - slot], sblk, sem.at[1 - slot])

    _tile_copies_wait(totblk, t, y_hbm, z.at[slot], sem.at[slot])
    r = route_ref[...]
    col = lax.broadcasted_iota(jnp.int32, (DISP_R, DISP_ROWS), 1).astype(_f32)
    w = jnp.where(col == r[:, 4:5], r[:, 0:1], jnp.where(col == r[:, 5:6], r[:, 1:2], 0.0))
    f = _dot(w.astype(_bf16), z[slot].astype(_bf16))
    hn = h_ref[...] + m_ref[0][5:6] * f
    o_ref[...] = hn * lax.rsqrt(jnp.mean(hn * hn, axis=-1, keepdims=True) + EPS) * g_ref[...]


def _moe_combine(plan, route, h, mods, g, y_sorted, tokens_per_batch):
    n, d = h.shape
    nt = n // DISP_R
    per_b = tokens_per_batch // DISP_R
    grid_spec = pltpu.PrefetchScalarGridSpec(
        num_scalar_prefetch=4,
        grid=(nt,),
        in_specs=[
            pl.BlockSpec((DISP_R, ROUTE_LANES), lambda t, *_: (t, 0)),
            pl.BlockSpec((DISP_R, d), lambda t, *_: (t, 0)),
            pl.BlockSpec((1, 6, d), lambda t, *_: (t // per_b, 0, 0)),
            pl.BlockSpec((1, d), lambda t, *_: (0, 0)),
            pl.BlockSpec(memory_space=pl.ANY),
        ],
        out_specs=pl.BlockSpec((DISP_R, d), lambda t, *_: (t, 0)),
        scratch_shapes=[
            pltpu.VMEM((2, DISP_ROWS, d), _f32),
            pltpu.SemaphoreType.DMA((2,)),
        ],
    )
    return pl.pallas_call(
        _combine_kernel,
        grid_spec=grid_spec,
        out_shape=jax.ShapeDtypeStruct((n, d), _f32),
        compiler_params=_cparams(("arbitrary",), VMEM_LIMIT),
        name="moe_combine",
    )(plan["gblk"], plan["nblk"], plan["sblk"], plan["totblk"], route, h, mods, g, y_sorted)


def _routing_plan(counts, rows):
    nt = counts.shape[0]
    n = nt * DISP_R
    i32 = jnp.int32
    cnt = counts[:, :, 0].astype(i32)
    cpad = (cnt + ROW_BLK - 1) // ROW_BLK * ROW_BLK
    seg = jnp.cumsum(cpad, axis=1) - cpad
    rows_e = jnp.sum(cpad, axis=0)
    tiles_e = (rows_e + MOE_TM - 1) // MOE_TM
    tile_end = jnp.cumsum(tiles_e)
    off = (tile_end - tiles_e) * MOE_TM
    glob = off[None, :] + jnp.cumsum(cpad, axis=0) - cpad
    n_tiles = -(-(2 * n + nt * N_EXPERTS * (ROW_BLK - 1)) // MOE_TM) + N_EXPERTS
    n_used = tile_end[-1]
    ti = jnp.minimum(jnp.arange(n_tiles, dtype=i32), n_used - 1)
    tile_expert = jnp.sum((ti[:, None] >= tile_end[None, :]).astype(i32), axis=1)
    nblk = cpad // ROW_BLK
    return {
        "d": rows,
        "gblk": (glob // ROW_BLK).reshape(-1).astype(i32),
        "nblk": nblk.reshape(-1).astype(i32),
        "sblk": (seg // ROW_BLK).reshape(-1).astype(i32),
        "totblk": jnp.stack([jnp.sum(nblk // BIG_BLKS, axis=1), jnp.sum(nblk % BIG_BLKS, axis=1)],
                            axis=1).reshape(-1).astype(i32),
        "tail_blk": jnp.append((off + rows_e) // ROW_BLK, n_used * (MOE_TM // ROW_BLK)).astype(i32),
        "tail_n": jnp.append((tiles_e * MOE_TM - rows_e) // ROW_BLK,
                             (n_tiles - n_used) * (MOE_TM // ROW_BLK)).astype(i32),
        "tile_expert": tile_expert.astype(i32),
        "n_used": n_used.reshape(1).astype(i32),
        "n_rows": n_tiles * MOE_TM,
    }


def _permuted_w_in(w_in_l):
    d = w_in_l.shape[0]
    c = {}
    start = 0
    for name, size in (("na_qkv", 3 * NA_WIDTH), ("conv_glu", 2 * CONV_CH), ("ret_q", RET_QK_WIDTH),
                       ("ret_k", RET_QK_WIDTH), ("ret_v", RET_WIDTH), ("ret_g", 2 * RET_WIDTH)):
        c[name] = w_in_l[:, start:start + size]
        start += size
    sub = RET_QK_DIM // 2
    halves_first = lambda w: w.reshape(d, RET_HEADS, 2, sub).transpose(0, 2, 1, 3).reshape(d, RET_QK_WIDTH)
    wa = jnp.concatenate([c["na_qkv"], c["ret_v"]], axis=1)
    wb = jnp.concatenate([c["conv_glu"], halves_first(c["ret_q"]), halves_first(c["ret_k"]), c["ret_g"]], axis=1)
    return wa.astype(_bf16), wb.astype(_bf16)


def kernel(x, c, ctx, c_ctx, w_mod, b_mod, norm1_w, norm2_w, w_in, w_out, na_rpb, conv_w, conv_b, conv_ln_w,
           conv_ln_b, ret_decay, ret_gn_w, ffn_w_gate, ffn_w_up, ffn_w_down, moe_router, moe_router_b,
           moe_w_gate, moe_w_up, moe_w_down, final_norm_w):
    b, t, d = x.shape
    lc = ctx.shape[1]
    assert d == D_MODEL and t % Q_TILE == 0 and t // GRID_W >= K_ROWS and b + 1 <= MOD_ROWS
    assert t % DISP_R == 0 and lc % 256 == 0 and (2 * b * t) % MOE_TM == 0
    assert DEPTH == 2

    cvecs = jnp.zeros((MOD_ROWS, d), _f32).at[:b].set(c).at[b].set(c_ctx)
    mods = _mod_vectors(cvecs, w_mod, b_mod).reshape(DEPTH, MOD_ROWS, 6, d)
    cos_lat, sin_lat = _rope_tables(t)
    cos_ctx = jnp.ones((lc, RET_QK_WIDTH), _f32)
    sin_ctx = jnp.zeros((lc, RET_QK_WIDTH), _f32)
    vec = lambda a: a.reshape(1, -1)
    lat_tm = 512
    ctx_tm = 256

    h_lat, h_ctx = x, ctx
    out = None
    for l in range(DEPTH):
        last = l == DEPTH - 1
        m_lat = mods[l, :b]
        m_ctx = mods[l, b:b + 1]
        wa, wb = _permuted_w_in(w_in[l])
        wo = w_out[l].astype(_bf16)
        gamma = 1.0 - jnp.exp2(-ret_decay[l].astype(_f32))
        lg = jnp.log(gamma)
        lgq = jnp.tile(jnp.repeat(lg, RET_QK_DIM // 2, axis=1), (1, 2)).reshape(2, 1, RET_QK_WIDTH)
        lgv = jnp.repeat(lg, RET_V_DIM, axis=1).reshape(2, 1, RET_WIDTH)

        a_lat, b_lat = _inproj(h_lat, m_lat, vec(norm1_w[l]), wa, wb, lat_tm)
        a_ctx, b_ctx = _inproj(h_ctx, m_ctx, vec(norm1_w[l]), wa, wb, ctx_tm)

        y_na = _na_attention(a_lat, a_ctx, na_rpb[l])
        y_conv = _conv_module(b_lat, conv_w[l], conv_b[l], conv_ln_w[l], conv_ln_b[l])
        s_zero = jnp.zeros((b, 2, RET_QK_WIDTH, RET_WIDTH), _f32)
        y_ret_c, s_ctx = _retention(a_ctx, b_ctx, cos_ctx, sin_ctx, s_zero, lg, lgq, lgv, ret_gn_w[l])
        y_ret, _ = _retention(a_lat, b_lat, cos_lat, sin_lat, s_ctx, lg, lgq, lgv, ret_gn_w[l])

        if not last:
            wg = ffn_w_gate[l // 2].astype(_bf16)
            wu = ffn_w_up[l // 2].astype(_bf16)
            wd = ffn_w_down[l // 2].astype(_bf16)
            y_na_c = _ctx_attention(a_ctx)
            y_conv_c = _conv_module(b_ctx, conv_w[l], conv_b[l], conv_ln_w[l], conv_ln_b[l])
            h_ctx = _outproj_ffn(y_na_c, y_conv_c, y_ret_c, h_ctx, wo, m_ctx, vec(norm2_w[l]), wg, wu, wd, ctx_tm)
            h_lat = _outproj_ffn(y_na, y_conv, y_ret, h_lat, wo, m_lat, vec(norm2_w[l]), wg, wu, wd, lat_tm)
        else:
            rw = jnp.zeros((d, ROUTE_LANES), _f32).at[:, :N_EXPERTS].set(moe_router[l // 2])
            rb = jnp.full((1, ROUTE_LANES), NEG_INF, _f32).at[0, :N_EXPERTS].set(moe_router_b[l // 2])
            h_lat, u_lat, route, rows, counts = _outproj_route(y_na, y_conv, y_ret, h_lat, wo, m_lat,
                                                               vec(norm2_w[l]), rw, rb)
            route = route.reshape(b * t, ROUTE_LANES)
            plan = _routing_plan(counts, rows)
            x_sorted = _moe_dispatch(plan, u_lat.reshape(b * t, d), plan["n_rows"])
            y_sorted = _moe_experts(plan, x_sorted, moe_w_gate[l // 2], moe_w_up[l // 2], moe_w_down[l // 2])
            out = _moe_combine(plan, route, h_lat.reshape(b * t, d), m_lat, vec(final_norm_w), y_sorted, t)
            out = out.reshape(b, t, d)
    return out
```

```python
import functools

import numpy as np
import jax
import jax.numpy as jnp
from jax import lax
from jax.experimental import pallas as pl
from jax.experimental.pallas import tpu as pltpu

D_MODEL = 1024
DEPTH = 2
GRID_W = 64
NA_HEAD_DIM = 64
NA_WIDTH = 512
NA_HEADS = 8
NB_ROWS = 8
NB_COLS = 16
CONV_CH = 256
CONV_WIDTH = 31
RET_WIDTH = 256
RET_HEADS = 4
RET_V_DIM = 64
RET_QK_DIM = 32
RET_QK_WIDTH = 128
D_FF = 2816
N_EXPERTS = 8
D_FF_EXPERT = 3584
ROPE_BASE = 10000.0
EPS = 1e-6
NEG_INF = -1e30

A_WIDTH = 3 * NA_WIDTH + RET_WIDTH
B_WIDTH = 2 * CONV_CH + 2 * RET_QK_WIDTH + 2 * RET_WIDTH

HEAD_GROUP = 4
HG_LANES = HEAD_GROUP * NA_HEAD_DIM
Q_ROWS = 4
Q_TILE = Q_ROWS * GRID_W
K_ROWS = 12
K_TILE = K_ROWS * GRID_W
RET_CHUNK = 256
RET_UNROLL = 8
CONV_CHUNK = 128
CONV_PAD = 16
SUBLANES = 8
LANES = 128
MOE_TM = 512
MOE_FF_CHUNK = 512
MOE_FF_WIDE = 1792
ROW_BLK = SUBLANES
BIG_SHIFT = 3
BIG_BLKS = 1 << BIG_SHIFT
DISP_R = 512
DISP_ROWS = 2 * DISP_R + N_EXPERTS * ROW_BLK
ROUTE_LANES = 128
VMEM_LIMIT = 56 * 1024 * 1024

_f32 = jnp.float32
_bf16 = jnp.bfloat16


def _cparams(sem, vmem=None):
    return pltpu.CompilerParams(dimension_semantics=sem, vmem_limit_bytes=vmem)


def _resident(shape):
    return pl.BlockSpec(shape, lambda *_: (0,) * len(shape), pipeline_mode=pl.Buffered(1))


def _sigmoid(x):
    return 1.0 / (1.0 + jnp.exp(-x))


def _silu(x):
    return x * _sigmoid(x)


def _dot(a, b):
    return jnp.dot(a, b, preferred_element_type=_f32)


def _dot_nt(a, b):
    return lax.dot_general(a, b, (((1,), (1,)), ((), ())), preferred_element_type=_f32)


def _dot_tn(a, b):
    return lax.dot_general(a, b, (((0,), (0,)), ((), ())), preferred_element_type=_f32)


def _split_bf16(a):
    hi = a.astype(_bf16)
    lo = (a - hi.astype(_f32)).astype(_bf16)
    return hi, lo


def _dot_split(a, b):
    ah, al = _split_bf16(a)
    bh, bl = _split_bf16(b)
    return _dot(ah, bh) + _dot(al, bh) + _dot(ah, bl)


def _rms_modulate(x, g, shift, scale):
    y = x * lax.rsqrt(jnp.mean(x * x, axis=-1, keepdims=True) + EPS)
    return (y * g) * (1.0 + scale) + shift


MOD_ROWS = 24
MOD_TN = 1536


def _mod_kernel(c_ref, w_ref, b_ref, o_ref):
    s = _silu(c_ref[...])
    o_ref[0] = _dot_split(s, w_ref[0]) + b_ref[0]


def _mod_vectors(cvecs, w_mod, b_mod):
    n = w_mod.shape[2]
    return pl.pallas_call(
        _mod_kernel,
        grid=(DEPTH, n // MOD_TN),
        in_specs=[
            pl.BlockSpec((MOD_ROWS, D_MODEL), lambda l, j: (0, 0)),
            pl.BlockSpec((1, D_MODEL, MOD_TN), lambda l, j: (l, 0, j)),
            pl.BlockSpec((1, 1, MOD_TN), lambda l, j: (l, 0, j)),
        ],
        out_specs=pl.BlockSpec((1, MOD_ROWS, MOD_TN), lambda l, j: (l, 0, j)),
        out_shape=jax.ShapeDtypeStruct((DEPTH, MOD_ROWS, n), _f32),
        compiler_params=_cparams(("arbitrary", "arbitrary"), VMEM_LIMIT),
        name="mod_vectors",
    )(cvecs, w_mod, b_mod.reshape(DEPTH, 1, n))


def _inproj_kernel(x_ref, m_ref, g_ref, wa_ref, wb_ref, oa_ref, ob_ref):
    m = m_ref[0]
    u = _rms_modulate(x_ref[0], g_ref[...], m[0:1], m[1:2]).astype(_bf16)
    oa_ref[0] = _dot(u, wa_ref[...]).astype(_bf16)
    ob_ref[0] = _dot(u, wb_ref[...])


def _inproj(x, mods, g, wa, wb, tm):
    b, l, d = x.shape
    nb = mods.shape[0]
    mod_map = (lambda i, j: (i, 0, 0)) if nb > 1 else (lambda i, j: (0, 0, 0))
    return pl.pallas_call(
        _inproj_kernel,
        grid=(b, l // tm),
        in_specs=[
            pl.BlockSpec((1, tm, d), lambda i, j: (i, j, 0)),
            pl.BlockSpec((1, 6, d), mod_map),
            pl.BlockSpec((1, d), lambda i, j: (0, 0)),
            _resident((d, A_WIDTH)),
            _resident((d, B_WIDTH)),
        ],
        out_specs=[
            pl.BlockSpec((1, tm, A_WIDTH), lambda i, j: (i, j, 0)),
            pl.BlockSpec((1, tm, B_WIDTH), lambda i, j: (i, j, 0)),
        ],
        out_shape=[
            jax.ShapeDtypeStruct((b, l, A_WIDTH), _bf16),
            jax.ShapeDtypeStruct((b, l, B_WIDTH), _f32),
        ],
        compiler_params=_cparams(("arbitrary", "arbitrary"), VMEM_LIMIT),
        name="inproj",
    )(x, mods, g, wa, wb)


def _masked_heads_attention(q, keys, vals, n_biased, bias, lane):
    out = jnp.zeros((q.shape[0], HG_LANES), _f32)
    for h in range(HEAD_GROUP):
        hm = (lane >= h * NA_HEAD_DIM) & (lane < (h + 1) * NA_HEAD_DIM)
        qm = jnp.where(hm, q, jnp.zeros_like(q)) * jnp.asarray(NA_HEAD_DIM ** -0.5, q.dtype)
        s = _dot_nt(qm, keys)
        parts = [s[:, :n_biased] + bias(h), s[:, n_biased:]] if n_biased else [s]
        mx = parts[0].max(axis=-1, keepdims=True)
        for part in parts[1:]:
            mx = jnp.maximum(mx, part.max(axis=-1, keepdims=True))
        probs = [jnp.exp(part - mx) for part in parts]
        den = probs[0].sum(axis=-1, keepdims=True)
        for p in probs[1:]:
            den = den + p.sum(axis=-1, keepdims=True)
        p = jnp.concatenate([p.astype(_bf16) for p in probs], axis=-1)
        out = jnp.where(hm, _dot(p, vals) / den, out)
    return out


def _fill_bias_table(rp_ref, bias_ref):
    lane = lax.broadcasted_iota(jnp.int32, (GRID_W, LANES), 1)
    qcol = lax.broadcasted_iota(jnp.int32, (GRID_W, LANES), 0)
    kcol = lane % GRID_W
    win0 = jnp.clip(qcol - NB_COLS // 2, 0, GRID_W - NB_COLS)
    col_ok = (kcol >= win0) & (kcol < win0 + NB_COLS)
    low_half = lane < GRID_W
    masked = jnp.full((GRID_W, LANES), NEG_INF, _f32)
    classes = ((lambda i: 0, NB_ROWS - 1),
               (lambda i: i, NB_ROWS - 1 - NB_ROWS // 2),
               (lambda i: K_ROWS - NB_ROWS, NB_ROWS - 1 - (K_ROWS - Q_ROWS)))
    for h in range(HEAD_GROUP):
        rolled = {}

        def block(dr, half):
            if (dr, half) not in rolled:
                row = jnp.broadcast_to(rp_ref[h, dr:dr + 1, :], (GRID_W, LANES))
                shift = (half * GRID_W - (NB_COLS - 1)) % LANES
                rolled[(dr, half)] = pltpu.roll(row, shift, 1, stride=1, stride_axis=0)
            return rolled[(dr, half)]

        for cls, (off, dr0) in enumerate(classes):
            for i in range(Q_ROWS):
                for jp in range(K_ROWS // 2):
                    parts = []
                    for half in range(2):
                        j = 2 * jp + half
                        ok = off(i) <= j < off(i) + NB_ROWS
                        parts.append(block(j - i + dr0, half) if ok else masked)
                    tile = jnp.where(col_ok, jnp.where(low_half, parts[0], parts[1]), NEG_INF)
                    bias_ref[cls, h, i * GRID_W:(i + 1) * GRID_W, jp * LANES:(jp + 1) * LANES] = tile


def _na_kernel(q_ref, k_ref, v_ref, kc_ref, vc_ref, rp_ref, o_ref, kall, vall, bias_ref):
    n_tiles = q_ref.shape[1] // Q_TILE
    rows = q_ref.shape[1] // GRID_W
    lane = lax.broadcasted_iota(jnp.int32, (1, HG_LANES), 1)

    @pl.when(pl.program_id(1) == 0)
    def _():
        _fill_bias_table(rp_ref, bias_ref)

    kall[K_TILE:, :] = kc_ref[0]
    vall[K_TILE:, :] = vc_ref[0]

    def body(g, carry):
        krow0 = jnp.clip(Q_ROWS * g - NB_ROWS // 2, 0, rows - K_ROWS)
        start = pl.multiple_of(krow0 * GRID_W, GRID_W)
        cls = jnp.where(g == 0, 0, jnp.where(g == n_tiles - 1, 2, 1))
        qs = pl.multiple_of(g * Q_TILE, Q_TILE)
        q = q_ref[0, pl.ds(qs, Q_TILE), :]
        kall[0:K_TILE, :] = k_ref[0, pl.ds(start, K_TILE), :]
        vall[0:K_TILE, :] = v_ref[0, pl.ds(start, K_TILE), :]
        out = _masked_heads_attention(q, kall[...], vall[...], K_TILE, lambda h: bias_ref[cls, h], lane)
        o_ref[0, pl.ds(qs, Q_TILE), :] = out.astype(o_ref.dtype)
        return carry

    lax.fori_loop(0, n_tiles, body, 0, unroll=8)


def _na_attention(a_lat, a_ctx, rpb):
    b, t, _ = a_lat.shape
    lc = a_ctx.shape[1]
    ng = NA_HEADS // HEAD_GROUP
    kq, kk, kv = 0, NA_WIDTH // HG_LANES, 2 * NA_WIDTH // HG_LANES
    rp = jnp.pad(rpb.astype(_f32), ((0, 0), (0, 1), (0, LANES - rpb.shape[2])))
    return pl.pallas_call(
        _na_kernel,
        grid=(ng, b),
        in_specs=[
            pl.BlockSpec((1, t, HG_LANES), lambda g, i: (i, 0, kq + g)),
            pl.BlockSpec((1, t, HG_LANES), lambda g, i: (i, 0, kk + g)),
            pl.BlockSpec((1, t, HG_LANES), lambda g, i: (i, 0, kv + g)),
            pl.BlockSpec((1, lc, HG_LANES), lambda g, i: (i, 0, kk + g)),
            pl.BlockSpec((1, lc, HG_LANES), lambda g, i: (i, 0, kv + g)),
            pl.BlockSpec((HEAD_GROUP, 2 * NB_ROWS, LANES), lambda g, i: (g, 0, 0)),
        ],
        out_specs=pl.BlockSpec((1, t, HG_LANES), lambda g, i: (i, 0, g)),
        out_shape=jax.ShapeDtypeStruct((b, t, NA_WIDTH), _bf16),
        scratch_shapes=[pltpu.VMEM((K_TILE + lc, HG_LANES), _bf16),
                        pltpu.VMEM((K_TILE + lc, HG_LANES), _bf16),
                        pltpu.VMEM((3, HEAD_GROUP, Q_TILE, K_TILE), _f32)],
        compiler_params=_cparams(("arbitrary", "arbitrary"), VMEM_LIMIT),
        name="na_attention",
    )(a_lat, a_lat, a_lat, a_ctx, a_ctx, rp)


def _ctx_attn_kernel(q_ref, k_ref, v_ref, o_ref):
    lane = lax.broadcasted_iota(jnp.int32, (1, HG_LANES), 1)
    out = _masked_heads_attention(q_ref[0], k_ref[0], v_ref[0], 0, None, lane)
    o_ref[0] = out.astype(o_ref.dtype)


def _ctx_attention(a_ctx):
    b, lc, _ = a_ctx.shape
    ng = NA_HEADS // HEAD_GROUP
    kq, kk, kv = 0, NA_WIDTH // HG_LANES, 2 * NA_WIDTH // HG_LANES
    return pl.pallas_call(
        _ctx_attn_kernel,
        grid=(ng, b),
        in_specs=[
            pl.BlockSpec((1, lc, HG_LANES), lambda g, i: (i, 0, kq + g)),
            pl.BlockSpec((1, lc, HG_LANES), lambda g, i: (i, 0, kk + g)),
            pl.BlockSpec((1, lc, HG_LANES), lambda g, i: (i, 0, kv + g)),
        ],
        out_specs=pl.BlockSpec((1, lc, HG_LANES), lambda g, i: (i, 0, g)),
        out_shape=jax.ShapeDtypeStruct((b, lc, NA_WIDTH), _bf16),
        compiler_params=_cparams(("arbitrary", "arbitrary")),
        name="ctx_attention",
    )(a_ctx, a_ctx, a_ctx)


def _conv_kernel(u_ref, w_ref, b_ref, lnw_ref, lnb_ref, o_ref, ypad):
    l = u_ref.shape[1]
    ypad[0:CONV_PAD, :] = jnp.zeros((CONV_PAD, CONV_CH), _f32)
    ypad[CONV_PAD + l:2 * CONV_PAD + l, :] = jnp.zeros((CONV_PAD, CONV_CH), _f32)
    ypad[CONV_PAD:CONV_PAD + l, :] = u_ref[0, :, 0:CONV_CH] * _sigmoid(u_ref[0, :, CONV_CH:2 * CONV_CH])
    shift = CONV_PAD - CONV_WIDTH // 2

    def body(c, carry):
        base = pl.multiple_of(c * CONV_CHUNK, CONV_CHUNK)
        win = ypad[pl.ds(base, CONV_CHUNK + 2 * CONV_PAD), :]
        acc = jnp.zeros((CONV_CHUNK, CONV_CH), _f32)
        for r in range(SUBLANES):
            offs = [o for o in range(shift, shift + CONV_WIDTH) if o % SUBLANES == r]
            wr = win if r == 0 else pltpu.roll(win, win.shape[0] - r, 0)
            for o in offs:
                acc = acc + wr[o - r:o - r + CONV_CHUNK, :] * w_ref[o - shift:o - shift + 1, :]
        y = acc + b_ref[...]
        mu = jnp.mean(y, axis=-1, keepdims=True)
        yc = y - mu
        var = jnp.mean(yc * yc, axis=-1, keepdims=True)
        z = yc * lax.rsqrt(var + EPS) * lnw_ref[...] + lnb_ref[...]
        o_ref[0, pl.ds(base, CONV_CHUNK), :] = _silu(z).astype(o_ref.dtype)
        return carry

    lax.fori_loop(0, l // CONV_CHUNK, body, 0, unroll=2)


def _conv_module(bf, conv_w, conv_b, ln_w, ln_b):
    b, l, _ = bf.shape
    vec = lambda a: a.reshape(1, CONV_CH)
    return pl.pallas_call(
        _conv_kernel,
        grid=(b,),
        in_specs=[
            pl.BlockSpec((1, l, 2 * CONV_CH), lambda i: (i, 0, 0)),
            pl.BlockSpec((CONV_WIDTH, CONV_CH), lambda i: (0, 0)),
            pl.BlockSpec((1, CONV_CH), lambda i: (0, 0)),
            pl.BlockSpec((1, CONV_CH), lambda i: (0, 0)),
            pl.BlockSpec((1, CONV_CH), lambda i: (0, 0)),
        ],
        out_specs=pl.BlockSpec((1, l, CONV_CH), lambda i: (i, 0, 0)),
        out_shape=jax.ShapeDtypeStruct((b, l, CONV_CH), _bf16),
        scratch_shapes=[pltpu.VMEM((l + 2 * CONV_PAD, CONV_CH), _f32)],
        compiler_params=_cparams(("arbitrary",), VMEM_LIMIT),
        name="conv_module",
    )(bf, conv_w, vec(conv_b), vec(ln_w), vec(ln_b))


def _ret_kernel(lg_ref, q_ref, k_ref, v_ref, gf_ref, gb_ref, cos_ref, sin_ref, s0_ref, lgq_ref, lgv_ref,
                gnw_ref, y_ref, sfin_ref, dmat, of_s, ob_s, st_s):
    l = q_ref.shape[1]
    c = min(RET_CHUNK, l)
    nc = l // c
    half = RET_QK_WIDTH // 2
    sub = RET_QK_DIM // 2
    ii = lax.broadcasted_iota(jnp.int32, (c, c), 0)
    jj = lax.broadcasted_iota(jnp.int32, (c, c), 1)
    diff = (ii - jj).astype(_f32)
    for h in range(RET_HEADS):
        dmat[0, h] = jnp.where(diff >= 0, jnp.exp(lg_ref[0, h] * jnp.maximum(diff, 0.0)), 0.0)
        dmat[1, h] = jnp.where(diff <= 0, jnp.exp(lg_ref[1, h] * jnp.maximum(-diff, 0.0)), 0.0)
    pos = lax.broadcasted_iota(jnp.int32, (c, 1), 0).astype(_f32)
    lane_q = lax.broadcasted_iota(jnp.int32, (1, RET_QK_WIDTH), 1)
    head_q = (lane_q % half) // sub
    lane_v = lax.broadcasted_iota(jnp.int32, (1, RET_WIDTH), 1)
    head_v = lane_v // RET_V_DIM
    row_h = (lax.broadcasted_iota(jnp.int32, (RET_QK_WIDTH, RET_WIDTH), 0) % half) // sub
    col_h = lax.broadcasted_iota(jnp.int32, (RET_QK_WIDTH, RET_WIDTH), 1) // RET_V_DIM
    blockmask = row_h == col_h
    q_dec = (jnp.exp(lgq_ref[0] * (pos + 1.0)), jnp.exp(lgq_ref[1] * (c - pos)))
    k_dec = (jnp.exp(lgq_ref[0] * (c - 1.0 - pos)), jnp.exp(lgq_ref[1] * pos))
    c_dec = (jnp.exp(lgv_ref[0] * float(c)), jnp.exp(lgv_ref[1] * float(c)))
    k_scale = RET_QK_DIM ** -0.5
    st_s[...] = s0_ref[0]

    def step(n, carry):
        for d in range(2):
            cidx = n if d == 0 else nc - 1 - n
            base = pl.multiple_of(cidx * c, c)
            cs = cos_ref[pl.ds(base, c), :]
            sn = sin_ref[pl.ds(base, c), :]
            q = q_ref[0, pl.ds(base, c), :]
            k = k_ref[0, pl.ds(base, c), :]
            qr = q * cs + pltpu.roll(q, half, 1) * sn
            kr = (k * cs + pltpu.roll(k, half, 1) * sn) * k_scale
            v = v_ref[0, pl.ds(base, c), :]
            qb = qr.astype(_bf16)
            kb = kr.astype(_bf16)
            o = _dot((qr * q_dec[d]).astype(_bf16), st_s[d].astype(_bf16))
            for h in range(RET_HEADS):
                s = _dot_nt(jnp.where(head_q == h, qb, jnp.zeros_like(qb)), kb)
                inner = (s * dmat[d, h]).astype(_bf16)
                o = o + _dot(inner, jnp.where(head_v == h, v, jnp.zeros_like(v)))
            if d == 0:
                of_s[pl.ds(base, c), :] = o
            else:
                ob_s[pl.ds(base, c), :] = o
            upd = _dot_tn((kr * k_dec[d]).astype(_bf16), v)
            st_s[d] = c_dec[d] * st_s[d] + jnp.where(blockmask, upd, 0.0)
        return carry

    lax.fori_loop(0, nc, step, 0, unroll=RET_UNROLL if nc % RET_UNROLL == 0 else 1)
    sfin_ref[0] = st_s[...]

    gi = lax.broadcasted_iota(jnp.int32, (RET_WIDTH, RET_WIDTH), 0) // RET_V_DIM
    gj = lax.broadcasted_iota(jnp.int32, (RET_WIDTH, RET_WIDTH), 1) // RET_V_DIM
    gmean = jnp.where(gi == gj, 1.0 / RET_V_DIM, 0.0).astype(_bf16)

    def group_mean(a):
        return _dot(a.astype(_bf16), gmean)

    def head_norm(o):
        dlt = o - group_mean(o)
        var = group_mean(dlt * dlt)
        return dlt * lax.rsqrt(var + EPS) * gnw_ref[...]

    def fin(n, carry):
        base = pl.multiple_of(n * c, c)
        yf = head_norm(of_s[pl.ds(base, c), :])
        yb = head_norm(ob_s[pl.ds(base, c), :])
        y = _silu(gf_ref[0, pl.ds(base, c), :]) * yf + _silu(gb_ref[0, pl.ds(base, c), :]) * yb
        y_ref[0, pl.ds(base, c), :] = y.astype(y_ref.dtype)
        return carry

    lax.fori_loop(0, nc, fin, 0, unroll=RET_UNROLL if nc % RET_UNROLL == 0 else 1)


def _retention(a, bf, cos_t, sin_t, s0, lg, lgq, lgv, gn_w):
    b, l, _ = a.shape
    c = min(RET_CHUNK, l)
    qi = 2 * CONV_CH // RET_QK_WIDTH
    gi = (2 * CONV_CH + 2 * RET_QK_WIDTH) // RET_WIDTH
    vi = 3 * NA_WIDTH // RET_WIDTH
    return pl.pallas_call(
        _ret_kernel,
        grid=(b,),
        in_specs=[
            pl.BlockSpec(memory_space=pltpu.SMEM),
            pl.BlockSpec((1, l, RET_QK_WIDTH), lambda i: (i, 0, qi)),
            pl.BlockSpec((1, l, RET_QK_WIDTH), lambda i: (i, 0, qi + 1)),
            pl.BlockSpec((1, l, RET_WIDTH), lambda i: (i, 0, vi)),
            pl.BlockSpec((1, l, RET_WIDTH), lambda i: (i, 0, gi)),
            pl.BlockSpec((1, l, RET_WIDTH), lambda i: (i, 0, gi + 1)),
            pl.BlockSpec((l, RET_QK_WIDTH), lambda i: (0, 0)),
            pl.BlockSpec((l, RET_QK_WIDTH), lambda i: (0, 0)),
            pl.BlockSpec((1, 2, RET_QK_WIDTH, RET_WIDTH), lambda i: (i, 0, 0, 0)),
            pl.BlockSpec((2, 1, RET_QK_WIDTH), lambda i: (0, 0, 0)),
            pl.BlockSpec((2, 1, RET_WIDTH), lambda i: (0, 0, 0)),
            pl.BlockSpec((1, RET_WIDTH), lambda i: (0, 0)),
        ],
        out_specs=[
            pl.BlockSpec((1, l, RET_WIDTH), lambda i: (i, 0, 0)),
            pl.BlockSpec((1, 2, RET_QK_WIDTH, RET_WIDTH), lambda i: (i, 0, 0, 0)),
        ],
        out_shape=[
            jax.ShapeDtypeStruct((b, l, RET_WIDTH), _bf16),
            jax.ShapeDtypeStruct((b, 2, RET_QK_WIDTH, RET_WIDTH), _f32),
        ],
        scratch_shapes=[
            pltpu.VMEM((2, RET_HEADS, c, c), _f32),
            pltpu.VMEM((l, RET_WIDTH), _f32),
            pltpu.VMEM((l, RET_WIDTH), _f32),
            pltpu.VMEM((2, RET_QK_WIDTH, RET_WIDTH), _f32),
        ],
        compiler_params=_cparams(("arbitrary",), VMEM_LIMIT),
        name="retention",
    )(lg, bf, bf, a, bf, bf, cos_t, sin_t, s0, lgq, lgv, gn_w.reshape(1, RET_WIDTH))


def _rope_tables(t_len):
    t = np.arange(t_len)
    row = (t // GRID_W).astype(np.float32)
    col = (t % GRID_W).astype(np.float32)
    axis_dim = RET_QK_DIM // 2
    inv = jnp.asarray(ROPE_BASE, _f32) ** (-jnp.arange(0, axis_dim, 2, dtype=_f32) / axis_dim)
    ang = jnp.concatenate([jnp.asarray(row)[:, None] * inv, jnp.asarray(col)[:, None] * inv], axis=-1)
    cos, sin = jnp.cos(ang), jnp.sin(ang)
    cos_t = jnp.tile(cos, (1, 2 * RET_HEADS))
    sin_t = jnp.concatenate([jnp.tile(-sin, (1, RET_HEADS)), jnp.tile(sin, (1, RET_HEADS))], axis=-1)
    return cos_t, sin_t


def _top2_route(logits):
    m = logits.shape[0]
    lt = logits.T[0:N_EXPERTS, :]
    sub = lax.broadcasted_iota(jnp.int32, lt.shape, 0).astype(_f32)
    m1 = lt.max(axis=0, keepdims=True)
    i1 = jnp.where(lt == m1, sub, float(N_EXPERTS)).min(axis=0, keepdims=True)
    rest = jnp.where(sub == i1, -jnp.inf, lt)
    m2 = rest.max(axis=0, keepdims=True)
    i2 = jnp.where(rest == m2, sub, float(N_EXPERTS)).min(axis=0, keepdims=True)
    e = jnp.exp(m2 - m1)
    p1 = 1.0 / (1.0 + e)
    p2 = e * p1
    oh1 = jnp.where(sub == i1, 1.0, 0.0)
    oh2 = jnp.where(sub == i2, 1.0, 0.0)
    cnt = oh1 + oh2
    before = (lax.broadcasted_iota(jnp.int32, (m, m), 0) < lax.broadcasted_iota(jnp.int32, (m, m), 1))
    pref = _dot(cnt.astype(_bf16), jnp.where(before, 1.0, 0.0).astype(_bf16))
    tot = cnt.sum(axis=1, keepdims=True)
    cpad = jnp.floor((tot + (ROW_BLK - 1.0)) * (1.0 / ROW_BLK)) * ROW_BLK
    d1 = (pref * oh1).sum(axis=0, keepdims=True)
    d2 = (pref * oh2).sum(axis=0, keepdims=True)
    seg = jnp.zeros((1, 1), _f32)
    for ex in range(N_EXPERTS):
        d1 = d1 + oh1[ex:ex + 1, :] * seg
        d2 = d2 + oh2[ex:ex + 1, :] * seg
        seg = seg + cpad[ex:ex + 1, :]
    rows8 = jnp.concatenate([p1, p2, i1, i2, d1, d2, jnp.zeros((ROUTE_LANES - 6, m), _f32)], axis=0)
    return (rows8.T, jnp.concatenate([d1, d2], axis=0).astype(jnp.int32),
            jnp.broadcast_to(tot, (N_EXPERTS, ROUTE_LANES)))


def _outproj_route_kernel(yna, yconv, yret, h_ref, w_ref, m_ref, g_ref, rw_ref, rb_ref,
                          ho_ref, u_ref, route_ref, rows_ref, cnt_ref):
    y = jnp.concatenate([yna[0], yconv[0], yret[0]], axis=-1)
    m = m_ref[0]
    hn = h_ref[0] + m[2:3] * _dot(y, w_ref[...])
    ho_ref[0] = hn
    u = _rms_modulate(hn, g_ref[...], m[3:4], m[4:5])
    u_ref[0] = u.astype(u_ref.dtype)
    route_ref[0], rows_ref[0], cnt_ref[0] = _top2_route(_dot_split(u, rw_ref[...]) + rb_ref[...])


def _outproj_route(y_na, y_conv, y_ret, h, w_out, mods, g2, rw, rb):
    b, l, d = h.shape
    tm = DISP_R
    nj = l // tm
    tok = lambda w: pl.BlockSpec((1, tm, w), lambda i, j: (i, j, 0))
    tile = lambda r, w: pl.BlockSpec((1, r, w), lambda i, j: (i * nj + j, 0, 0))
    return pl.pallas_call(
        _outproj_route_kernel,
        grid=(b, nj),
        in_specs=[tok(NA_WIDTH), tok(CONV_CH), tok(RET_WIDTH), tok(d), _resident((d, d)),
                  pl.BlockSpec((1, 6, d), lambda i, j: (i, 0, 0)),
                  pl.BlockSpec((1, d), lambda i, j: (0, 0)),
                  pl.BlockSpec((d, ROUTE_LANES), lambda i, j: (0, 0)),
                  pl.BlockSpec((1, ROUTE_LANES), lambda i, j: (0, 0))],
        out_specs=[tok(d), tok(d), tok(ROUTE_LANES), tile(2, tm), tile(N_EXPERTS, ROUTE_LANES)],
        out_shape=[jax.ShapeDtypeStruct((b, l, d), _f32), jax.ShapeDtypeStruct((b, l, d), _bf16),
                   jax.ShapeDtypeStruct((b, l, ROUTE_LANES), _f32),
                   jax.ShapeDtypeStruct((b * nj, 2, tm), jnp.int32),
                   jax.ShapeDtypeStruct((b * nj, N_EXPERTS, ROUTE_LANES), _f32)],
        compiler_params=_cparams(("arbitrary", "arbitrary"), VMEM_LIMIT),
        name="outproj_route",
    )(y_na, y_conv, y_ret, h, w_out, mods, g2, rw, rb)


def _outproj_ffn_kernel(yna, yconv, yret, h_ref, wo_ref, m_ref, g_ref, wg_ref, wu_ref, wd_ref, o_ref):
    y = jnp.concatenate([yna[0], yconv[0], yret[0]], axis=-1)
    m = m_ref[0]
    hn = h_ref[0] + m[2:3] * _dot(y, wo_ref[...])
    u = _rms_modulate(hn, g_ref[...], m[3:4], m[4:5]).astype(_bf16)
    a = _dot(u, wg_ref[...])
    mid = (_silu(a) * _dot(u, wu_ref[...])).astype(_bf16)
    o_ref[0] = hn + m[5:6] * _dot(mid, wd_ref[...])


def _outproj_ffn(y_na, y_conv, y_ret, h, w_out, mods, g2, wg, wu, wd, tm):
    b, l, d = h.shape
    nb = mods.shape[0]
    dff = wg.shape[1]
    mod_map = (lambda i, j: (i, 0, 0)) if nb > 1 else (lambda i, j: (0, 0, 0))
    tok = lambda w: pl.BlockSpec((1, tm, w), lambda i, j: (i, j, 0))
    return pl.pallas_call(
        _outproj_ffn_kernel,
        grid=(b, l // tm),
        in_specs=[tok(NA_WIDTH), tok(CONV_CH), tok(RET_WIDTH), tok(d), _resident((d, d)),
                  pl.BlockSpec((1, 6, d), mod_map), pl.BlockSpec((1, d), lambda i, j: (0, 0)),
                  _resident((d, dff)), _resident((d, dff)), _resident((dff, d))],
        out_specs=tok(d),
        out_shape=jax.ShapeDtypeStruct((b, l, d), _f32),
        compiler_params=_cparams(("arbitrary", "arbitrary"), VMEM_LIMIT),
        name="outproj_ffn",
    )(y_na, y_conv, y_ret, h, w_out, mods, g2, wg, wu, wd)


def _block_copies_start(nblk_ref, seg, src, src_blk_ref, dst, dst_blk_ref, sem):
    for e in range(N_EXPERTS):
        k = seg * N_EXPERTS + e
        sb = src_blk_ref[k]
        db = dst_blk_ref[k]
        n_big = lax.shift_right_logical(nblk_ref[k], BIG_SHIFT)
        n_small = nblk_ref[k] & (BIG_BLKS - 1)

        def copy(src_blk, dst_blk, blks):
            s0 = pl.multiple_of(src_blk * ROW_BLK, ROW_BLK)
            d0 = pl.multiple_of(dst_blk * ROW_BLK, ROW_BLK)
            rows = blks * ROW_BLK
            pltpu.make_async_copy(src.at[pl.ds(s0, rows), :], dst.at[pl.ds(d0, rows), :], sem).start()

        def big(j, carry):
            copy(sb + j * BIG_BLKS, db + j * BIG_BLKS, BIG_BLKS)
            return carry

        def small(j, carry):
            copy(sb + n_big * BIG_BLKS + j, db + n_big * BIG_BLKS + j, 1)
            return carry

        lax.fori_loop(0, n_big, big, 0)
        lax.fori_loop(0, n_small, small, 0)


def _block_copies_wait(n, src, dst, sem, blks=1):
    rows = blks * ROW_BLK

    def body(j, carry):
        pltpu.make_async_copy(src.at[pl.ds(0, rows), :], dst.at[pl.ds(0, rows), :], sem).wait()
        return carry

    lax.fori_loop(0, n, body, 0)


def _tile_copies_wait(tot_ref, seg, src, dst, sem):
    _block_copies_wait(tot_ref[2 * seg], src, dst, sem, BIG_BLKS)
    _block_copies_wait(tot_ref[2 * seg + 1], src, dst, sem)


def _dispatch_kernel(gblk, nblk, sblk, totblk, tail_blk, tail_n, d_ref, u_ref, x_hbm, y, sem, zbuf, zsem):
    t = pl.program_id(0)
    nt = pl.num_programs(0)
    slot = t % 2

    @pl.when(t >= 2)
    def _():
        _tile_copies_wait(totblk, t - 2, y.at[slot], x_hbm, sem.at[slot])

    rows = lax.broadcasted_iota(jnp.int32, (DISP_ROWS, DISP_R), 0)
    hit = jnp.where(rows == d_ref[0, 0:1, :], 1.0, jnp.where(rows == d_ref[0, 1:2, :], 1.0, 0.0))
    y[slot] = _dot(hit.astype(_bf16), u_ref[...].astype(_bf16))
    _block_copies_start(nblk, t, y.at[slot], sblk, x_hbm, gblk, sem.at[slot])

    @pl.when(t == nt - 1)
    def _():
        @pl.when(t >= 1)
        def _():
            _tile_copies_wait(totblk, jnp.maximum(t - 1, 0), y.at[1 - slot], x_hbm, sem.at[1 - slot])

        _tile_copies_wait(totblk, t, y.at[slot], x_hbm, sem.at[slot])
        zbuf[...] = jnp.zeros_like(zbuf)
        for e in range(N_EXPERTS + 1):
            tb = tail_blk[e]

            def zero_body(j, carry):
                d0 = pl.multiple_of((tb + j) * ROW_BLK, ROW_BLK)
                pltpu.make_async_copy(zbuf, x_hbm.at[pl.ds(d0, ROW_BLK), :], zsem).start()
                return carry

            lax.fori_loop(0, tail_n[e], zero_body, 0)
        for e in range(N_EXPERTS + 1):
            _block_copies_wait(tail_n[e], zbuf, x_hbm, zsem)


def _moe_dispatch(plan, u, n_rows):
    n, d = u.shape
    nt = n // DISP_R
    grid_spec = pltpu.PrefetchScalarGridSpec(
        num_scalar_prefetch=6,
        grid=(nt,),
        in_specs=[
            pl.BlockSpec((1, 2, DISP_R), lambda t, *_: (t, 0, 0)),
            pl.BlockSpec((DISP_R, d), lambda t, *_: (t, 0)),
        ],
        out_specs=pl.BlockSpec(memory_space=pl.ANY),
        scratch_shapes=[
            pltpu.VMEM((2, DISP_ROWS, d), _f32),
            pltpu.SemaphoreType.DMA((2,)),
            pltpu.VMEM((ROW_BLK, d), _f32),
            pltpu.SemaphoreType.DMA(()),
        ],
    )
    return pl.pallas_call(
        _dispatch_kernel,
        grid_spec=grid_spec,
        out_shape=jax.ShapeDtypeStruct((n_rows, d), _f32),
        compiler_params=_cparams(("arbitrary",), VMEM_LIMIT),
        name="moe_dispatch",
    )(plan["gblk"], plan["nblk"], plan["sblk"], plan["totblk"], plan["tail_blk"], plan["tail_n"], plan["d"], u)


def _moe_kernel(te_ref, nu_ref, x_ref, wg_hbm, wu_hbm, wd_hbm, o_ref, wg_s, wu_s, wd_s, stg_g, stg_u, stg_d,
                sem, acc):
    i = pl.program_id(0)
    n_used = nu_ref[0]
    e = te_ref[i]
    n_chunks = D_FF_EXPERT // MOE_FF_CHUNK
    first_tile = (i == 0) | (e != te_ref[jnp.maximum(i - 1, 0)])

    def chunk_copies(c):
        span = pl.ds(c * MOE_FF_CHUNK, MOE_FF_CHUNK)
        s = c % 2
        return (pltpu.make_async_copy(wg_hbm.at[e, :, span], stg_g.at[s], sem.at[0, s]),
                pltpu.make_async_copy(wu_hbm.at[e, :, span], stg_u.at[s], sem.at[1, s]),
                pltpu.make_async_copy(wd_hbm.at[e, span, :], stg_d.at[s], sem.at[2, s]))

    def tile(load_weights):
        x = x_ref[...].astype(_bf16)
        if load_weights:
            for cp in chunk_copies(0):
                cp.start()
        width = MOE_FF_CHUNK if load_weights else MOE_FF_WIDE
        for c in range(D_FF_EXPERT // width):
            cols = slice(c * width, (c + 1) * width)
            if load_weights:
                if c + 1 < n_chunks:
                    for cp in chunk_copies(c + 1):
                        cp.start()
                for cp in chunk_copies(c):
                    cp.wait()
                wg_s[:, cols] = stg_g[c % 2].astype(_bf16)
                wu_s[:, cols] = stg_u[c % 2].astype(_bf16)
                wd_s[cols, :] = stg_d[c % 2].astype(_bf16)
            a = _dot(x, wg_s[:, cols])
            mid = (_silu(a) * _dot(x, wu_s[:, cols])).astype(_bf16)
            part = _dot(mid, wd_s[cols, :])
            if c == 0:
                acc[...] = part
            else:
                acc[...] += part
        o_ref[...] = acc[...]

    @pl.when(first_tile)
    def _():
        tile(True)

    @pl.when(jnp.logical_not(first_tile) & (i < n_used))
    def _():
        tile(False)

    @pl.when(i >= n_used)
    def _():
        o_ref[...] = jnp.zeros_like(o_ref)


def _moe_experts(plan, x, wg, wu, wd):
    tile_expert, n_used = plan["tile_expert"], plan["n_used"]
    n_tiles = tile_expert.shape[0]
    d = x.shape[1]
    grid_spec = pltpu.PrefetchScalarGridSpec(
        num_scalar_prefetch=2,
        grid=(n_tiles,),
        in_specs=[
            pl.BlockSpec((MOE_TM, d), lambda i, te, nu: (jnp.minimum(i, nu[0] - 1), 0)),
            pl.BlockSpec(memory_space=pl.ANY),
            pl.BlockSpec(memory_space=pl.ANY),
            pl.BlockSpec(memory_space=pl.ANY),
        ],
        out_specs=pl.BlockSpec((MOE_TM, d), lambda i, te, nu: (i, 0)),
        scratch_shapes=[
            pltpu.VMEM((d, D_FF_EXPERT), _bf16),
            pltpu.VMEM((d, D_FF_EXPERT), _bf16),
            pltpu.VMEM((D_FF_EXPERT, d), _bf16),
            pltpu.VMEM((2, d, MOE_FF_CHUNK), _f32),
            pltpu.VMEM((2, d, MOE_FF_CHUNK), _f32),
            pltpu.VMEM((2, MOE_FF_CHUNK, d), _f32),
            pltpu.SemaphoreType.DMA((3, 2)),
            pltpu.VMEM((MOE_TM, d), _f32),
        ],
    )
    return pl.pallas_call(
        _moe_kernel,
        grid_spec=grid_spec,
        out_shape=jax.ShapeDtypeStruct((n_tiles * MOE_TM, d), _f32),
        compiler_params=_cparams(("arbitrary",), VMEM_LIMIT),
        name="moe_experts",
    )(tile_expert, n_used, x, wg, wu, wd)


def _combine_kernel(gblk, nblk, sblk, totblk, route_ref, h_ref, m_ref, g_ref, y_hbm, o_ref, z, sem):
    t = pl.program_id(0)
    nt = pl.num_programs(0)
    slot = t % 2

    @pl.when(t == 0)
    def _():
        z[...] = jnp.zeros_like(z)
        _block_copies_start(nblk, t, y_hbm, gblk, z.at[0], sblk, sem.at[0])

    @pl.when(t + 1 < nt)
    def _():
        _block_copies_start(nblk, t + 1, y_hbm, gblk, z.at[1 - slot], sblk, sem.at[1 - slot])

    _tile_copies_wait(totblk, t, y_hbm, z.at[slot], sem.at[slot])
    r = route_ref[...]
    col = lax.broadcasted_iota(jnp.int32, (DISP_R, DISP_ROWS), 1).astype(_f32)
    w = jnp.where(col == r[:, 4:5], r[:, 0:1], jnp.where(col == r[:, 5:6], r[:, 1:2], 0.0))
    f = _dot(w.astype(_bf16), z[slot].astype(_bf16))
    hn = h_ref[...] + m_ref[0][5:6] * f
    o_ref[...] = hn * lax.rsqrt(jnp.mean(hn * hn, axis=-1, keepdims=True) + EPS) * g_ref[...]


def _moe_combine(plan, route, h, mods, g, y_sorted, tokens_per_batch):
    n, d = h.shape
    nt = n // DISP_R
    per_b = tokens_per_batch // DISP_R
    grid_spec = pltpu.PrefetchScalarGridSpec(
        num_scalar_prefetch=4,
        grid=(nt,),
        in_specs=[
            pl.BlockSpec((DISP_R, ROUTE_LANES), lambda t, *_: (t, 0)),
            pl.BlockSpec((DISP_R, d), lambda t, *_: (t, 0)),
            pl.BlockSpec((1, 6, d), lambda t, *_: (t // per_b, 0, 0)),
            pl.BlockSpec((1, d), lambda t, *_: (0, 0)),
            pl.BlockSpec(memory_space=pl.ANY),
        ],
        out_specs=pl.BlockSpec((DISP_R, d), lambda t, *_: (t, 0)),
        scratch_shapes=[
            pltpu.VMEM((2, DISP_ROWS, d), _f32),
            pltpu.SemaphoreType.DMA((2,)),
        ],
    )
    return pl.pallas_call(
        _combine_kernel,
        grid_spec=grid_spec,
        out_shape=jax.ShapeDtypeStruct((n, d), _f32),
        compiler_params=_cparams(("arbitrary",), VMEM_LIMIT),
        name="moe_combine",
    )(plan["gblk"], plan["nblk"], plan["sblk"], plan["totblk"], route, h, mods, g, y_sorted)


def _routing_plan(counts, rows):
    nt = counts.shape[0]
    n = nt * DISP_R
    i32 = jnp.int32
    cnt = counts[:, :, 0].astype(i32)
    cpad = (cnt + ROW_BLK - 1) // ROW_BLK * ROW_BLK
    seg = jnp.cumsum(cpad, axis=1) - cpad
    rows_e = jnp.sum(cpad, axis=0)
    tiles_e = (rows_e + MOE_TM - 1) // MOE_TM
    tile_end = jnp.cumsum(tiles_e)
    off = (tile_end - tiles_e) * MOE_TM
    glob = off[None, :] + jnp.cumsum(cpad, axis=0) - cpad
    n_tiles = -(-(2 * n + nt * N_EXPERTS * (ROW_BLK - 1)) // MOE_TM) + N_EXPERTS
    n_used = tile_end[-1]
    ti = jnp.minimum(jnp.arange(n_tiles, dtype=i32), n_used - 1)
    tile_expert = jnp.sum((ti[:, None] >= tile_end[None, :]).astype(i32), axis=1)
    nblk = cpad // ROW_BLK
    return {
        "d": rows,
        "gblk": (glob // ROW_BLK).reshape(-1).astype(i32),
        "nblk": nblk.reshape(-1).astype(i32),
        "sblk": (seg // ROW_BLK).reshape(-1).astype(i32),
        "totblk": jnp.stack([jnp.sum(nblk // BIG_BLKS, axis=1), jnp.sum(nblk % BIG_BLKS, axis=1)],
                            axis=1).reshape(-1).astype(i32),
        "tail_blk": jnp.append((off + rows_e) // ROW_BLK, n_used * (MOE_TM // ROW_BLK)).astype(i32),
        "tail_n": jnp.append((tiles_e * MOE_TM - rows_e) // ROW_BLK,
                             (n_tiles - n_used) * (MOE_TM // ROW_BLK)).astype(i32),
        "tile_expert": tile_expert.astype(i32),
        "n_used": n_used.reshape(1).astype(i32),
        "n_rows": n_tiles * MOE_TM,
    }


def _permuted_w_in(w_in_l):
    d = w_in_l.shape[0]
    c = {}
    start = 0
    for name, size in (("na_qkv", 3 * NA_WIDTH), ("conv_glu", 2 * CONV_CH), ("ret_q", RET_QK_WIDTH),
                       ("ret_k", RET_QK_WIDTH), ("ret_v", RET_WIDTH), ("ret_g", 2 * RET_WIDTH)):
        c[name] = w_in_l[:, start:start + size]
        start += size
    sub = RET_QK_DIM // 2
    halves_first = lambda w: w.reshape(d, RET_HEADS, 2, sub).transpose(0, 2, 1, 3).reshape(d, RET_QK_WIDTH)
    wa = jnp.concatenate([c["na_qkv"], c["ret_v"]], axis=1)
    wb = jnp.concatenate([c["conv_glu"], halves_first(c["ret_q"]), halves_first(c["ret_k"]), c["ret_g"]], axis=1)
    return wa.astype(_bf16), wb.astype(_bf16)


def kernel(x, c, ctx, c_ctx, w_mod, b_mod, norm1_w, norm2_w, w_in, w_out, na_rpb, conv_w, conv_b, conv_ln_w,
           conv_ln_b, ret_decay, ret_gn_w, ffn_w_gate, ffn_w_up, ffn_w_down, moe_router, moe_router_b,
           moe_w_gate, moe_w_up, moe_w_down, final_norm_w):
    b, t, d = x.shape
    lc = ctx.shape[1]
    assert d == D_MODEL and t % Q_TILE == 0 and t // GRID_W >= K_ROWS and b + 1 <= MOD_ROWS
    assert t % DISP_R == 0 and lc % 256 == 0 and (2 * b * t) % MOE_TM == 0
    assert DEPTH == 2

    cvecs = jnp.zeros((MOD_ROWS, d), _f32).at[:b].set(c).at[b].set(c_ctx)
    mods = _mod_vectors(cvecs, w_mod, b_mod).reshape(DEPTH, MOD_ROWS, 6, d)
    cos_lat, sin_lat = _rope_tables(t)
    cos_ctx = jnp.ones((lc, RET_QK_WIDTH), _f32)
    sin_ctx = jnp.zeros((lc, RET_QK_WIDTH), _f32)
    vec = lambda a: a.reshape(1, -1)
    lat_tm = DISP_R
    flat = lambda a: a.reshape(1, b * lc, a.shape[-1])
    unflat = lambda a: a.reshape(b, lc, a.shape[-1])
    ctx_tm = min(DISP_R, b * lc)

    h_lat, h_ctx = x, ctx
    out = None
    for l in range(DEPTH):
        last = l == DEPTH - 1
        m_lat = mods[l, :b]
        m_ctx = mods[l, b:b + 1]
        wa, wb = _permuted_w_in(w_in[l])
        wo = w_out[l].astype(_bf16)
        gamma = 1.0 - jnp.exp2(-ret_decay[l].astype(_f32))
        lg = jnp.log(gamma)
        lgq = jnp.tile(jnp.repeat(lg, RET_QK_DIM // 2, axis=1), (1, 2)).reshape(2, 1, RET_QK_WIDTH)
        lgv = jnp.repeat(lg, RET_V_DIM, axis=1).reshape(2, 1, RET_WIDTH)

        a_lat, b_lat = _inproj(h_lat, m_lat, vec(norm1_w[l]), wa, wb, lat_tm)
        a_ctx, b_ctx = map(unflat, _inproj(flat(h_ctx), m_ctx, vec(norm1_w[l]), wa, wb, ctx_tm))

        y_na = _na_attention(a_lat, a_ctx, na_rpb[l])
        y_conv = _conv_module(b_lat, conv_w[l], conv_b[l], conv_ln_w[l], conv_ln_b[l])
        s_zero = jnp.zeros((b, 2, RET_QK_WIDTH, RET_WIDTH), _f32)
        y_ret_c, s_ctx = _retention(a_ctx, b_ctx, cos_ctx, sin_ctx, s_zero, lg, lgq, lgv, ret_gn_w[l])
        y_ret, _ = _retention(a_lat, b_lat, cos_lat, sin_lat, s_ctx, lg, lgq, lgv, ret_gn_w[l])

        if not last:
            wg = ffn_w_gate[l // 2].astype(_bf16)
            wu = ffn_w_up[l // 2].astype(_bf16)
            wd = ffn_w_down[l // 2].astype(_bf16)
            y_na_c = _ctx_attention(a_ctx)
            y_conv_c = _conv_module(b_ctx, conv_w[l], conv_b[l], conv_ln_w[l], conv_ln_b[l])
            h_ctx = unflat(_outproj_ffn(flat(y_na_c), flat(y_conv_c), flat(y_ret_c), flat(h_ctx), wo, m_ctx,
                                        vec(norm2_w[l]), wg, wu, wd, ctx_tm))
            h_lat = _outproj_ffn(y_na, y_conv, y_ret, h_lat, wo, m_lat, vec(norm2_w[l]), wg, wu, wd, lat_tm)
        else:
            rw = jnp.zeros((d, ROUTE_LANES), _f32).at[:, :N_EXPERTS].set(moe_router[l // 2])
            rb = jnp.full((1, ROUTE_LANES), NEG_INF, _f32).at[0, :N_EXPERTS].set(moe_router_b[l // 2])
            h_lat, u_lat, route, rows, counts = _outproj_route(y_na, y_conv, y_ret, h_lat, wo, m_lat,
                                                               vec(norm2_w[l]), rw, rb)
            route = route.reshape(b * t, ROUTE_LANES)
            plan = _routing_plan(counts, rows)
            x_sorted = _moe_dispatch(plan, u_lat.reshape(b * t, d), plan["n_rows"])
            y_sorted = _moe_experts(plan, x_sorted, moe_w_gate[l // 2], moe_w_up[l // 2], moe_w_down[l // 2])
            out = _moe_combine(plan, route, h_lat.reshape(b * t, d), m_lat, vec(final_norm_w), y_sorted, t)
            out = out.reshape(b, t, d)
    return out
```

```python
import numpy as np
import jax
import jax.numpy as jnp
from jax import lax
from jax.experimental import pallas as pl
from jax.experimental.pallas import tpu as pltpu

D_MODEL = 1024
DEPTH = 2
GRID_W = 64
NA_HEAD_DIM = 64
NA_WIDTH = 512
NA_HEADS = 8
NB_ROWS = 8
NB_COLS = 16
CONV_CH = 256
CONV_WIDTH = 31
RET_WIDTH = 256
RET_HEADS = 4
RET_V_DIM = 64
RET_QK_DIM = 32
RET_QK_WIDTH = 128
D_FF = 2816
N_EXPERTS = 8
D_FF_EXPERT = 3584
ROPE_BASE = 10000.0
EPS = 1e-6
NEG_INF = -1e30

A_WIDTH = 3 * NA_WIDTH + RET_WIDTH
B_WIDTH = 2 * CONV_CH + 2 * RET_QK_WIDTH + 2 * RET_WIDTH

HEAD_GROUP = 4
HG_LANES = HEAD_GROUP * NA_HEAD_DIM
Q_ROWS = 4
Q_TILE = Q_ROWS * GRID_W
K_ROWS = 12
K_TILE = K_ROWS * GRID_W
RET_CHUNK = 256
RET_UNROLL = 8
CONV_CHUNK = 128
CONV_PAD = 16
SUBLANES = 8
LANES = 128
MOE_TM = 512
MOE_FF_CHUNK = 512
MOE_FF_WIDE = 1792
ROW_BLK = SUBLANES
BIG_SHIFT = 3
BIG_BLKS = 1 << BIG_SHIFT
DISP_R = 512
DISP_ROWS = 2 * DISP_R + N_EXPERTS * ROW_BLK
ROUTE_LANES = 128
VMEM_LIMIT = 56 * 1024 * 1024

_f32 = jnp.float32
_bf16 = jnp.bfloat16


def _cparams(sem, vmem=None):
    return pltpu.CompilerParams(dimension_semantics=sem, vmem_limit_bytes=vmem)


def _resident(shape):
    return pl.BlockSpec(shape, lambda *_: (0,) * len(shape), pipeline_mode=pl.Buffered(1))


def _sigmoid(x):
    return 1.0 / (1.0 + jnp.exp(-x))


def _silu(x):
    return x * _sigmoid(x)


def _dot(a, b):
    return jnp.dot(a, b, preferred_element_type=_f32)


def _dot_nt(a, b):
    return lax.dot_general(a, b, (((1,), (1,)), ((), ())), preferred_element_type=_f32)


def _dot_tn(a, b):
    return lax.dot_general(a, b, (((0,), (0,)), ((), ())), preferred_element_type=_f32)


def _split_bf16(a):
    hi = a.astype(_bf16)
    lo = (a - hi.astype(_f32)).astype(_bf16)
    return hi, lo


def _dot_split(a, b):
    ah, al = _split_bf16(a)
    bh, bl = _split_bf16(b)
    return _dot(ah, bh) + _dot(al, bh) + _dot(ah, bl)


def _rms_modulate(x, g, shift, scale):
    y = x * lax.rsqrt(jnp.mean(x * x, axis=-1, keepdims=True) + EPS)
    return (y * g) * (1.0 + scale) + shift


MOD_ROWS = 24
MOD_TN = 1536


def _mod_kernel(c_ref, w_ref, b_ref, o_ref):
    s = _silu(c_ref[...])
    o_ref[0] = _dot_split(s, w_ref[0]) + b_ref[0]


def _mod_vectors(cvecs, w_mod, b_mod):
    n = w_mod.shape[2]
    return pl.pallas_call(
        _mod_kernel,
        grid=(DEPTH, n // MOD_TN),
        in_specs=[
            pl.BlockSpec((MOD_ROWS, D_MODEL), lambda l, j: (0, 0)),
            pl.BlockSpec((1, D_MODEL, MOD_TN), lambda l, j: (l, 0, j)),
            pl.BlockSpec((1, 1, MOD_TN), lambda l, j: (l, 0, j)),
        ],
        out_specs=pl.BlockSpec((1, MOD_ROWS, MOD_TN), lambda l, j: (l, 0, j)),
        out_shape=jax.ShapeDtypeStruct((DEPTH, MOD_ROWS, n), _f32),
        compiler_params=_cparams(("arbitrary", "arbitrary"), VMEM_LIMIT),
        name="mod_vectors",
    )(cvecs, w_mod, b_mod.reshape(DEPTH, 1, n))


def _inproj_kernel(x_ref, m_ref, g_ref, wa_ref, wb_ref, oa_ref, ob_ref):
    m = m_ref[0]
    u = _rms_modulate(x_ref[0], g_ref[...], m[0:1], m[1:2]).astype(_bf16)
    oa_ref[0] = _dot(u, wa_ref[...]).astype(_bf16)
    ob_ref[0] = _dot(u, wb_ref[...])


def _inproj(x, mods, g, wa, wb, tm):
    b, l, d = x.shape
    nb = mods.shape[0]
    mod_map = (lambda i, j: (i, 0, 0)) if nb > 1 else (lambda i, j: (0, 0, 0))
    return pl.pallas_call(
        _inproj_kernel,
        grid=(b, l // tm),
        in_specs=[
            pl.BlockSpec((1, tm, d), lambda i, j: (i, j, 0)),
            pl.BlockSpec((1, 6, d), mod_map),
            pl.BlockSpec((1, d), lambda i, j: (0, 0)),
            _resident((d, A_WIDTH)),
            _resident((d, B_WIDTH)),
        ],
        out_specs=[
            pl.BlockSpec((1, tm, A_WIDTH), lambda i, j: (i, j, 0)),
            pl.BlockSpec((1, tm, B_WIDTH), lambda i, j: (i, j, 0)),
        ],
        out_shape=[
            jax.ShapeDtypeStruct((b, l, A_WIDTH), _bf16),
            jax.ShapeDtypeStruct((b, l, B_WIDTH), _f32),
        ],
        compiler_params=_cparams(("arbitrary", "arbitrary"), VMEM_LIMIT),
        name="inproj",
    )(x, mods, g, wa, wb)


def _masked_heads_attention(q, keys, vals, n_biased, bias, lane):
    out = jnp.zeros((q.shape[0], HG_LANES), _f32)
    for h in range(HEAD_GROUP):
        hm = (lane >= h * NA_HEAD_DIM) & (lane < (h + 1) * NA_HEAD_DIM)
        qm = jnp.where(hm, q, jnp.zeros_like(q)) * jnp.asarray(NA_HEAD_DIM ** -0.5, q.dtype)
        s = _dot_nt(qm, keys)
        parts = [s[:, :n_biased] + bias(h), s[:, n_biased:]] if n_biased else [s]
        mx = parts[0].max(axis=-1, keepdims=True)
        for part in parts[1:]:
            mx = jnp.maximum(mx, part.max(axis=-1, keepdims=True))
        probs = [jnp.exp(part - mx) for part in parts]
        den = probs[0].sum(axis=-1, keepdims=True)
        for p in probs[1:]:
            den = den + p.sum(axis=-1, keepdims=True)
        p = jnp.concatenate([p.astype(_bf16) for p in probs], axis=-1)
        out = jnp.where(hm, _dot(p, vals) / den, out)
    return out


def _fill_bias_table(rp_ref, bias_ref):
    lane = lax.broadcasted_iota(jnp.int32, (GRID_W, LANES), 1)
    qcol = lax.broadcasted_iota(jnp.int32, (GRID_W, LANES), 0)
    kcol = lane % GRID_W
    win0 = jnp.clip(qcol - NB_COLS // 2, 0, GRID_W - NB_COLS)
    col_ok = (kcol >= win0) & (kcol < win0 + NB_COLS)
    low_half = lane < GRID_W
    masked = jnp.full((GRID_W, LANES), NEG_INF, _f32)
    classes = ((lambda i: 0, NB_ROWS - 1),
               (lambda i: i, NB_ROWS - 1 - NB_ROWS // 2),
               (lambda i: K_ROWS - NB_ROWS, NB_ROWS - 1 - (K_ROWS - Q_ROWS)))
    for h in range(HEAD_GROUP):
        rolled = {}

        def block(dr, half):
            if (dr, half) not in rolled:
                row = jnp.broadcast_to(rp_ref[h, dr:dr + 1, :], (GRID_W, LANES))
                shift = (half * GRID_W - (NB_COLS - 1)) % LANES
                rolled[(dr, half)] = pltpu.roll(row, shift, 1, stride=1, stride_axis=0)
            return rolled[(dr, half)]

        for cls, (off, dr0) in enumerate(classes):
            for i in range(Q_ROWS):
                for jp in range(K_ROWS // 2):
                    parts = []
                    for half in range(2):
                        j = 2 * jp + half
                        ok = off(i) <= j < off(i) + NB_ROWS
                        parts.append(block(j - i + dr0, half) if ok else masked)
                    tile = jnp.where(col_ok, jnp.where(low_half, parts[0], parts[1]), NEG_INF)
                    bias_ref[cls, h, i * GRID_W:(i + 1) * GRID_W, jp * LANES:(jp + 1) * LANES] = tile


def _na_kernel(q_ref, k_ref, v_ref, kc_ref, vc_ref, rp_ref, o_ref, kall, vall, bias_ref):
    n_tiles = q_ref.shape[1] // Q_TILE
    rows = q_ref.shape[1] // GRID_W
    lane = lax.broadcasted_iota(jnp.int32, (1, HG_LANES), 1)

    @pl.when(pl.program_id(1) == 0)
    def _():
        _fill_bias_table(rp_ref, bias_ref)

    kall[K_TILE:, :] = kc_ref[0]
    vall[K_TILE:, :] = vc_ref[0]

    def body(g, carry):
        krow0 = jnp.clip(Q_ROWS * g - NB_ROWS // 2, 0, rows - K_ROWS)
        start = pl.multiple_of(krow0 * GRID_W, GRID_W)
        cls = jnp.where(g == 0, 0, jnp.where(g == n_tiles - 1, 2, 1))
        qs = pl.multiple_of(g * Q_TILE, Q_TILE)
        q = q_ref[0, pl.ds(qs, Q_TILE), :]
        kall[0:K_TILE, :] = k_ref[0, pl.ds(start, K_TILE), :]
        vall[0:K_TILE, :] = v_ref[0, pl.ds(start, K_TILE), :]
        out = _masked_heads_attention(q, kall[...], vall[...], K_TILE, lambda h: bias_ref[cls, h], lane)
        o_ref[0, pl.ds(qs, Q_TILE), :] = out.astype(o_ref.dtype)
        return carry

    lax.fori_loop(0, n_tiles, body, 0, unroll=8)


def _na_attention(a_lat, a_ctx, rpb):
    b, t, _ = a_lat.shape
    lc = a_ctx.shape[1]
    ng = NA_HEADS // HEAD_GROUP
    kq, kk, kv = 0, NA_WIDTH // HG_LANES, 2 * NA_WIDTH // HG_LANES
    rp = jnp.pad(rpb.astype(_f32), ((0, 0), (0, 1), (0, LANES - rpb.shape[2])))
    return pl.pallas_call(
        _na_kernel,
        grid=(ng, b),
        in_specs=[
            pl.BlockSpec((1, t, HG_LANES), lambda g, i: (i, 0, kq + g)),
            pl.BlockSpec((1, t, HG_LANES), lambda g, i: (i, 0, kk + g)),
            pl.BlockSpec((1, t, HG_LANES), lambda g, i: (i, 0, kv + g)),
            pl.BlockSpec((1, lc, HG_LANES), lambda g, i: (i, 0, kk + g)),
            pl.BlockSpec((1, lc, HG_LANES), lambda g, i: (i, 0, kv + g)),
            pl.BlockSpec((HEAD_GROUP, 2 * NB_ROWS, LANES), lambda g, i: (g, 0, 0)),
        ],
        out_specs=pl.BlockSpec((1, t, HG_LANES), lambda g, i: (i, 0, g)),
        out_shape=jax.ShapeDtypeStruct((b, t, NA_WIDTH), _bf16),
        scratch_shapes=[pltpu.VMEM((K_TILE + lc, HG_LANES), _bf16),
                        pltpu.VMEM((K_TILE + lc, HG_LANES), _bf16),
                        pltpu.VMEM((3, HEAD_GROUP, Q_TILE, K_TILE), _f32)],
        compiler_params=_cparams(("arbitrary", "arbitrary"), VMEM_LIMIT),
        name="na_attention",
    )(a_lat, a_lat, a_lat, a_ctx, a_ctx, rp)


def _ctx_attn_kernel(q_ref, k_ref, v_ref, o_ref):
    lane = lax.broadcasted_iota(jnp.int32, (1, HG_LANES), 1)
    out = _masked_heads_attention(q_ref[0], k_ref[0], v_ref[0], 0, None, lane)
    o_ref[0] = out.astype(o_ref.dtype)


def _ctx_attention(a_ctx):
    b, lc, _ = a_ctx.shape
    ng = NA_HEADS // HEAD_GROUP
    kq, kk, kv = 0, NA_WIDTH // HG_LANES, 2 * NA_WIDTH // HG_LANES
    return pl.pallas_call(
        _ctx_attn_kernel,
        grid=(ng, b),
        in_specs=[
            pl.BlockSpec((1, lc, HG_LANES), lambda g, i: (i, 0, kq + g)),
            pl.BlockSpec((1, lc, HG_LANES), lambda g, i: (i, 0, kk + g)),
            pl.BlockSpec((1, lc, HG_LANES), lambda g, i: (i, 0, kv + g)),
        ],
        out_specs=pl.BlockSpec((1, lc, HG_LANES), lambda g, i: (i, 0, g)),
        out_shape=jax.ShapeDtypeStruct((b, lc, NA_WIDTH), _bf16),
        compiler_params=_cparams(("arbitrary", "arbitrary")),
        name="ctx_attention",
    )(a_ctx, a_ctx, a_ctx)


def _conv_kernel(u_ref, w_ref, b_ref, lnw_ref, lnb_ref, o_ref, ypad):
    l = u_ref.shape[1]
    ypad[0:CONV_PAD, :] = jnp.zeros((CONV_PAD, CONV_CH), _f32)
    ypad[CONV_PAD + l:2 * CONV_PAD + l, :] = jnp.zeros((CONV_PAD, CONV_CH), _f32)
    ypad[CONV_PAD:CONV_PAD + l, :] = u_ref[0, :, 0:CONV_CH] * _sigmoid(u_ref[0, :, CONV_CH:2 * CONV_CH])
    shift = CONV_PAD - CONV_WIDTH // 2

    def body(c, carry):
        base = pl.multiple_of(c * CONV_CHUNK, CONV_CHUNK)
        win = ypad[pl.ds(base, CONV_CHUNK + 2 * CONV_PAD), :]
        acc = jnp.zeros((CONV_CHUNK, CONV_CH), _f32)
        for r in range(SUBLANES):
            offs = [o for o in range(shift, shift + CONV_WIDTH) if o % SUBLANES == r]
            wr = win if r == 0 else pltpu.roll(win, win.shape[0] - r, 0)
            for o in offs:
                acc = acc + wr[o - r:o - r + CONV_CHUNK, :] * w_ref[o - shift:o - shift + 1, :]
        y = acc + b_ref[...]
        mu = jnp.mean(y, axis=-1, keepdims=True)
        yc = y - mu
        var = jnp.mean(yc * yc, axis=-1, keepdims=True)
        z = yc * lax.rsqrt(var + EPS) * lnw_ref[...] + lnb_ref[...]
        o_ref[0, pl.ds(base, CONV_CHUNK), :] = _silu(z).astype(o_ref.dtype)
        return carry

    lax.fori_loop(0, l // CONV_CHUNK, body, 0, unroll=2)


def _conv_module(bf, conv_w, conv_b, ln_w, ln_b):
    b, l, _ = bf.shape
    vec = lambda a: a.reshape(1, CONV_CH)
    return pl.pallas_call(
        _conv_kernel,
        grid=(b,),
        in_specs=[
            pl.BlockSpec((1, l, 2 * CONV_CH), lambda i: (i, 0, 0)),
            pl.BlockSpec((CONV_WIDTH, CONV_CH), lambda i: (0, 0)),
            pl.BlockSpec((1, CONV_CH), lambda i: (0, 0)),
            pl.BlockSpec((1, CONV_CH), lambda i: (0, 0)),
            pl.BlockSpec((1, CONV_CH), lambda i: (0, 0)),
        ],
        out_specs=pl.BlockSpec((1, l, CONV_CH), lambda i: (i, 0, 0)),
        out_shape=jax.ShapeDtypeStruct((b, l, CONV_CH), _bf16),
        scratch_shapes=[pltpu.VMEM((l + 2 * CONV_PAD, CONV_CH), _f32)],
        compiler_params=_cparams(("arbitrary",), VMEM_LIMIT),
        name="conv_module",
    )(bf, conv_w, vec(conv_b), vec(ln_w), vec(ln_b))


def _ret_kernel(lg_ref, q_ref, k_ref, v_ref, gf_ref, gb_ref, cos_ref, sin_ref, s0_ref, lgq_ref, lgv_ref,
                gnw_ref, y_ref, sfin_ref, dmat, of_s, ob_s, st_s):
    l = q_ref.shape[1]
    c = min(RET_CHUNK, l)
    nc = l // c
    half = RET_QK_WIDTH // 2
    sub = RET_QK_DIM // 2
    ii = lax.broadcasted_iota(jnp.int32, (c, c), 0)
    jj = lax.broadcasted_iota(jnp.int32, (c, c), 1)
    diff = (ii - jj).astype(_f32)
    for h in range(RET_HEADS):
        dmat[0, h] = jnp.where(diff >= 0, jnp.exp(lg_ref[0, h] * jnp.maximum(diff, 0.0)), 0.0)
        dmat[1, h] = jnp.where(diff <= 0, jnp.exp(lg_ref[1, h] * jnp.maximum(-diff, 0.0)), 0.0)
    pos = lax.broadcasted_iota(jnp.int32, (c, 1), 0).astype(_f32)
    lane_q = lax.broadcasted_iota(jnp.int32, (1, RET_QK_WIDTH), 1)
    head_q = (lane_q % half) // sub
    lane_v = lax.broadcasted_iota(jnp.int32, (1, RET_WIDTH), 1)
    head_v = lane_v // RET_V_DIM
    row_h = (lax.broadcasted_iota(jnp.int32, (RET_QK_WIDTH, RET_WIDTH), 0) % half) // sub
    col_h = lax.broadcasted_iota(jnp.int32, (RET_QK_WIDTH, RET_WIDTH), 1) // RET_V_DIM
    blockmask = row_h == col_h
    q_dec = (jnp.exp(lgq_ref[0] * (pos + 1.0)), jnp.exp(lgq_ref[1] * (c - pos)))
    k_dec = (jnp.exp(lgq_ref[0] * (c - 1.0 - pos)), jnp.exp(lgq_ref[1] * pos))
    c_dec = (jnp.exp(lgv_ref[0] * float(c)), jnp.exp(lgv_ref[1] * float(c)))
    k_scale = RET_QK_DIM ** -0.5
    st_s[...] = s0_ref[0]

    def step(n, carry):
        for d in range(2):
            cidx = n if d == 0 else nc - 1 - n
            base = pl.multiple_of(cidx * c, c)
            cs = cos_ref[pl.ds(base, c), :]
            sn = sin_ref[pl.ds(base, c), :]
            q = q_ref[0, pl.ds(base, c), :]
            k = k_ref[0, pl.ds(base, c), :]
            qr = q * cs + pltpu.roll(q, half, 1) * sn
            kr = (k * cs + pltpu.roll(k, half, 1) * sn) * k_scale
            v = v_ref[0, pl.ds(base, c), :]
            qb = qr.astype(_bf16)
            kb = kr.astype(_bf16)
            o = _dot((qr * q_dec[d]).astype(_bf16), st_s[d].astype(_bf16))
            for h in range(RET_HEADS):
                s = _dot_nt(jnp.where(head_q == h, qb, jnp.zeros_like(qb)), kb)
                inner = (s * dmat[d, h]).astype(_bf16)
                o = o + _dot(inner, jnp.where(head_v == h, v, jnp.zeros_like(v)))
            if d == 0:
                of_s[pl.ds(base, c), :] = o
            else:
                ob_s[pl.ds(base, c), :] = o
            upd = _dot_tn((kr * k_dec[d]).astype(_bf16), v)
            st_s[d] = c_dec[d] * st_s[d] + jnp.where(blockmask, upd, 0.0)
        return carry

    lax.fori_loop(0, nc, step, 0, unroll=RET_UNROLL if nc % RET_UNROLL == 0 else 1)
    sfin_ref[0] = st_s[...]

    gi = lax.broadcasted_iota(jnp.int32, (RET_WIDTH, RET_WIDTH), 0) // RET_V_DIM
    gj = lax.broadcasted_iota(jnp.int32, (RET_WIDTH, RET_WIDTH), 1) // RET_V_DIM
    gmean = jnp.where(gi == gj, 1.0 / RET_V_DIM, 0.0).astype(_bf16)

    def group_mean(a):
        return _dot(a.astype(_bf16), gmean)

    def head_norm(o):
        dlt = o - group_mean(o)
        var = group_mean(dlt * dlt)
        return dlt * lax.rsqrt(var + EPS) * gnw_ref[...]

    def fin(n, carry):
        base = pl.multiple_of(n * c, c)
        yf = head_norm(of_s[pl.ds(base, c), :])
        yb = head_norm(ob_s[pl.ds(base, c), :])
        y = _silu(gf_ref[0, pl.ds(base, c), :]) * yf + _silu(gb_ref[0, pl.ds(base, c), :]) * yb
        y_ref[0, pl.ds(base, c), :] = y.astype(y_ref.dtype)
        return carry

    lax.fori_loop(0, nc, fin, 0, unroll=RET_UNROLL if nc % RET_UNROLL == 0 else 1)


def _retention(a, bf, cos_t, sin_t, s0, lg, lgq, lgv, gn_w):
    b, l, _ = a.shape
    c = min(RET_CHUNK, l)
    qi = 2 * CONV_CH // RET_QK_WIDTH
    gi = (2 * CONV_CH + 2 * RET_QK_WIDTH) // RET_WIDTH
    vi = 3 * NA_WIDTH // RET_WIDTH
    return pl.pallas_call(
        _ret_kernel,
        grid=(b,),
        in_specs=[
            pl.BlockSpec(memory_space=pltpu.SMEM),
            pl.BlockSpec((1, l, RET_QK_WIDTH), lambda i: (i, 0, qi)),
            pl.BlockSpec((1, l, RET_QK_WIDTH), lambda i: (i, 0, qi + 1)),
            pl.BlockSpec((1, l, RET_WIDTH), lambda i: (i, 0, vi)),
            pl.BlockSpec((1, l, RET_WIDTH), lambda i: (i, 0, gi)),
            pl.BlockSpec((1, l, RET_WIDTH), lambda i: (i, 0, gi + 1)),
            pl.BlockSpec((l, RET_QK_WIDTH), lambda i: (0, 0)),
            pl.BlockSpec((l, RET_QK_WIDTH), lambda i: (0, 0)),
            pl.BlockSpec((1, 2, RET_QK_WIDTH, RET_WIDTH), lambda i: (i, 0, 0, 0)),
            pl.BlockSpec((2, 1, RET_QK_WIDTH), lambda i: (0, 0, 0)),
            pl.BlockSpec((2, 1, RET_WIDTH), lambda i: (0, 0, 0)),
            pl.BlockSpec((1, RET_WIDTH), lambda i: (0, 0)),
        ],
        out_specs=[
            pl.BlockSpec((1, l, RET_WIDTH), lambda i: (i, 0, 0)),
            pl.BlockSpec((1, 2, RET_QK_WIDTH, RET_WIDTH), lambda i: (i, 0, 0, 0)),
        ],
        out_shape=[
            jax.ShapeDtypeStruct((b, l, RET_WIDTH), _bf16),
            jax.ShapeDtypeStruct((b, 2, RET_QK_WIDTH, RET_WIDTH), _f32),
        ],
        scratch_shapes=[
            pltpu.VMEM((2, RET_HEADS, c, c), _f32),
            pltpu.VMEM((l, RET_WIDTH), _f32),
            pltpu.VMEM((l, RET_WIDTH), _f32),
            pltpu.VMEM((2, RET_QK_WIDTH, RET_WIDTH), _f32),
        ],
        compiler_params=_cparams(("arbitrary",), VMEM_LIMIT),
        name="retention",
    )(lg, bf, bf, a, bf, bf, cos_t, sin_t, s0, lgq, lgv, gn_w.reshape(1, RET_WIDTH))


def _rope_tables(t_len):
    t = np.arange(t_len)
    row = (t // GRID_W).astype(np.float32)
    col = (t % GRID_W).astype(np.float32)
    axis_dim = RET_QK_DIM // 2
    inv = jnp.asarray(ROPE_BASE, _f32) ** (-jnp.arange(0, axis_dim, 2, dtype=_f32) / axis_dim)
    ang = jnp.concatenate([jnp.asarray(row)[:, None] * inv, jnp.asarray(col)[:, None] * inv], axis=-1)
    cos, sin = jnp.cos(ang), jnp.sin(ang)
    cos_t = jnp.tile(cos, (1, 2 * RET_HEADS))
    sin_t = jnp.concatenate([jnp.tile(-sin, (1, RET_HEADS)), jnp.tile(sin, (1, RET_HEADS))], axis=-1)
    return cos_t, sin_t


def _top2_route(logits):
    m = logits.shape[0]
    lt = logits.T[0:N_EXPERTS, :]
    sub = lax.broadcasted_iota(jnp.int32, lt.shape, 0).astype(_f32)
    m1 = lt.max(axis=0, keepdims=True)
    i1 = jnp.where(lt == m1, sub, float(N_EXPERTS)).min(axis=0, keepdims=True)
    rest = jnp.where(sub == i1, -jnp.inf, lt)
    m2 = rest.max(axis=0, keepdims=True)
    i2 = jnp.where(rest == m2, sub, float(N_EXPERTS)).min(axis=0, keepdims=True)
    e = jnp.exp(m2 - m1)
    p1 = 1.0 / (1.0 + e)
    p2 = e * p1
    oh1 = jnp.where(sub == i1, 1.0, 0.0)
    oh2 = jnp.where(sub == i2, 1.0, 0.0)
    cnt = oh1 + oh2
    before = (lax.broadcasted_iota(jnp.int32, (m, m), 0) < lax.broadcasted_iota(jnp.int32, (m, m), 1))
    pref = _dot(cnt.astype(_bf16), jnp.where(before, 1.0, 0.0).astype(_bf16))
    tot = cnt.sum(axis=1, keepdims=True)
    cpad = jnp.floor((tot + (ROW_BLK - 1.0)) * (1.0 / ROW_BLK)) * ROW_BLK
    d1 = (pref * oh1).sum(axis=0, keepdims=True)
    d2 = (pref * oh2).sum(axis=0, keepdims=True)
    seg = jnp.zeros((1, 1), _f32)
    for ex in range(N_EXPERTS):
        d1 = d1 + oh1[ex:ex + 1, :] * seg
        d2 = d2 + oh2[ex:ex + 1, :] * seg
        seg = seg + cpad[ex:ex + 1, :]
    rows8 = jnp.concatenate([p1, p2, i1, i2, d1, d2, jnp.zeros((ROUTE_LANES - 6, m), _f32)], axis=0)
    return (rows8.T, jnp.concatenate([d1, d2], axis=0).astype(jnp.int32),
            jnp.broadcast_to(tot, (N_EXPERTS, ROUTE_LANES)))


def _outproj_route_kernel(yna, yconv, yret, h_ref, w_ref, m_ref, g_ref, rw_ref, rb_ref,
                          ho_ref, u_ref, route_ref, rows_ref, cnt_ref):
    y = jnp.concatenate([yna[0], yconv[0], yret[0]], axis=-1)
    m = m_ref[0]
    hn = h_ref[0] + m[2:3] * _dot(y, w_ref[...])
    ho_ref[0] = hn
    u = _rms_modulate(hn, g_ref[...], m[3:4], m[4:5])
    u_ref[0] = u.astype(u_ref.dtype)
    route_ref[0], rows_ref[0], cnt_ref[0] = _top2_route(_dot_split(u, rw_ref[...]) + rb_ref[...])


def _outproj_route(y_na, y_conv, y_ret, h, w_out, mods, g2, rw, rb):
    b, l, d = h.shape
    tm = DISP_R
    nj = l // tm
    tok = lambda w: pl.BlockSpec((1, tm, w), lambda i, j: (i, j, 0))
    tile = lambda r, w: pl.BlockSpec((1, r, w), lambda i, j: (i * nj + j, 0, 0))
    return pl.pallas_call(
        _outproj_route_kernel,
        grid=(b, nj),
        in_specs=[tok(NA_WIDTH), tok(CONV_CH), tok(RET_WIDTH), tok(d), _resident((d, d)),
                  pl.BlockSpec((1, 6, d), lambda i, j: (i, 0, 0)),
                  pl.BlockSpec((1, d), lambda i, j: (0, 0)),
                  pl.BlockSpec((d, ROUTE_LANES), lambda i, j: (0, 0)),
                  pl.BlockSpec((1, ROUTE_LANES), lambda i, j: (0, 0))],
        out_specs=[tok(d), tok(d), tok(ROUTE_LANES), tile(2, tm), tile(N_EXPERTS, ROUTE_LANES)],
        out_shape=[jax.ShapeDtypeStruct((b, l, d), _f32), jax.ShapeDtypeStruct((b, l, d), _bf16),
                   jax.ShapeDtypeStruct((b, l, ROUTE_LANES), _f32),
                   jax.ShapeDtypeStruct((b * nj, 2, tm), jnp.int32),
                   jax.ShapeDtypeStruct((b * nj, N_EXPERTS, ROUTE_LANES), _f32)],
        compiler_params=_cparams(("arbitrary", "arbitrary"), VMEM_LIMIT),
        name="outproj_route",
    )(y_na, y_conv, y_ret, h, w_out, mods, g2, rw, rb)


def _outproj_ffn_kernel(yna, yconv, yret, h_ref, wo_ref, m_ref, g_ref, wg_ref, wu_ref, wd_ref, o_ref):
    y = jnp.concatenate([yna[0], yconv[0], yret[0]], axis=-1)
    m = m_ref[0]
    hn = h_ref[0] + m[2:3] * _dot(y, wo_ref[...])
    u = _rms_modulate(hn, g_ref[...], m[3:4], m[4:5]).astype(_bf16)
    a = _dot(u, wg_ref[...])
    mid = (_silu(a) * _dot(u, wu_ref[...])).astype(_bf16)
    o_ref[0] = hn + m[5:6] * _dot(mid, wd_ref[...])


def _outproj_ffn(y_na, y_conv, y_ret, h, w_out, mods, g2, wg, wu, wd, tm):
    b, l, d = h.shape
    nb = mods.shape[0]
    dff = wg.shape[1]
    mod_map = (lambda i, j: (i, 0, 0)) if nb > 1 else (lambda i, j: (0, 0, 0))
    tok = lambda w: pl.BlockSpec((1, tm, w), lambda i, j: (i, j, 0))
    return pl.pallas_call(
        _outproj_ffn_kernel,
        grid=(b, l // tm),
        in_specs=[tok(NA_WIDTH), tok(CONV_CH), tok(RET_WIDTH), tok(d), _resident((d, d)),
                  pl.BlockSpec((1, 6, d), mod_map), pl.BlockSpec((1, d), lambda i, j: (0, 0)),
                  _resident((d, dff)), _resident((d, dff)), _resident((dff, d))],
        out_specs=tok(d),
        out_shape=jax.ShapeDtypeStruct((b, l, d), _f32),
        compiler_params=_cparams(("arbitrary", "arbitrary"), VMEM_LIMIT),
        name="outproj_ffn",
    )(y_na, y_conv, y_ret, h, w_out, mods, g2, wg, wu, wd)


def _block_copies_start(nblk_ref, seg, src, src_blk_ref, dst, dst_blk_ref, sem):
    for e in range(N_EXPERTS):
        k = seg * N_EXPERTS + e
        sb = src_blk_ref[k]
        db = dst_blk_ref[k]
        n_big = lax.shift_right_logical(nblk_ref[k], BIG_SHIFT)
        n_small = nblk_ref[k] & (BIG_BLKS - 1)

        def copy(src_blk, dst_blk, blks):
            s0 = pl.multiple_of(src_blk * ROW_BLK, ROW_BLK)
            d0 = pl.multiple_of(dst_blk * ROW_BLK, ROW_BLK)
            rows = blks * ROW_BLK
            pltpu.make_async_copy(src.at[pl.ds(s0, rows), :], dst.at[pl.ds(d0, rows), :], sem).start()

        def big(j, carry):
            copy(sb + j * BIG_BLKS, db + j * BIG_BLKS, BIG_BLKS)
            return carry

        def small(j, carry):
            copy(sb + n_big * BIG_BLKS + j, db + n_big * BIG_BLKS + j, 1)
            return carry

        lax.fori_loop(0, n_big, big, 0)
        lax.fori_loop(0, n_small, small, 0)


def _block_copies_wait(n, src, dst, sem, blks=1):
    rows = blks * ROW_BLK

    def body(j, carry):
        pltpu.make_async_copy(src.at[pl.ds(0, rows), :], dst.at[pl.ds(0, rows), :], sem).wait()
        return carry

    lax.fori_loop(0, n, body, 0)


def _tile_copies_wait(tot_ref, seg, src, dst, sem):
    _block_copies_wait(tot_ref[2 * seg], src, dst, sem, BIG_BLKS)
    _block_copies_wait(tot_ref[2 * seg + 1], src, dst, sem)


def _dispatch_kernel(gblk, nblk, sblk, totblk, tail_blk, tail_n, d_ref, u_ref, x_hbm, y, sem, zbuf, zsem):
    t = pl.program_id(0)
    nt = pl.num_programs(0)
    slot = t % 2

    @pl.when(t >= 2)
    def _():
        _tile_copies_wait(totblk, t - 2, y.at[slot], x_hbm, sem.at[slot])

    rows = lax.broadcasted_iota(jnp.int32, (DISP_ROWS, DISP_R), 0)
    hit = jnp.where(rows == d_ref[0, 0:1, :], 1.0, jnp.where(rows == d_ref[0, 1:2, :], 1.0, 0.0))
    y[slot] = _dot(hit.astype(_bf16), u_ref[...].astype(_bf16))
    _block_copies_start(nblk, t, y.at[slot], sblk, x_hbm, gblk, sem.at[slot])

    @pl.when(t == nt - 1)
    def _():
        @pl.when(t >= 1)
        def _():
            _tile_copies_wait(totblk, jnp.maximum(t - 1, 0), y.at[1 - slot], x_hbm, sem.at[1 - slot])

        _tile_copies_wait(totblk, t, y.at[slot], x_hbm, sem.at[slot])
        zbuf[...] = jnp.zeros_like(zbuf)
        for e in range(N_EXPERTS + 1):
            tb = tail_blk[e]

            def zero_body(j, carry):
                d0 = pl.multiple_of((tb + j) * ROW_BLK, ROW_BLK)
                pltpu.make_async_copy(zbuf, x_hbm.at[pl.ds(d0, ROW_BLK), :], zsem).start()
                return carry

            lax.fori_loop(0, tail_n[e], zero_body, 0)
        for e in range(N_EXPERTS + 1):
            _block_copies_wait(tail_n[e], zbuf, x_hbm, zsem)


def _moe_dispatch(plan, u, n_rows):
    n, d = u.shape
    nt = n // DISP_R
    grid_spec = pltpu.PrefetchScalarGridSpec(
        num_scalar_prefetch=6,
        grid=(nt,),
        in_specs=[
            pl.BlockSpec((1, 2, DISP_R), lambda t, *_: (t, 0, 0)),
            pl.BlockSpec((DISP_R, d), lambda t, *_: (t, 0)),
        ],
        out_specs=pl.BlockSpec(memory_space=pl.ANY),
        scratch_shapes=[
            pltpu.VMEM((2, DISP_ROWS, d), _f32),
            pltpu.SemaphoreType.DMA((2,)),
            pltpu.VMEM((ROW_BLK, d), _f32),
            pltpu.SemaphoreType.DMA(()),
        ],
    )
    return pl.pallas_call(
        _dispatch_kernel,
        grid_spec=grid_spec,
        out_shape=jax.ShapeDtypeStruct((n_rows, d), _f32),
        compiler_params=_cparams(("arbitrary",), VMEM_LIMIT),
        name="moe_dispatch",
    )(plan["gblk"], plan["nblk"], plan["sblk"], plan["totblk"], plan["tail_blk"], plan["tail_n"], plan["d"], u)


def _moe_kernel(te_ref, nu_ref, x_ref, wg_hbm, wu_hbm, wd_hbm, o_ref, wg_s, wu_s, wd_s, stg_g, stg_u, stg_d,
                sem, acc):
    i = pl.program_id(0)
    n_used = nu_ref[0]
    e = te_ref[i]
    n_chunks = D_FF_EXPERT // MOE_FF_CHUNK
    first_tile = (i == 0) | (e != te_ref[jnp.maximum(i - 1, 0)])

    def chunk_copies(c):
        span = pl.ds(c * MOE_FF_CHUNK, MOE_FF_CHUNK)
        s = c % 2
        return (pltpu.make_async_copy(wg_hbm.at[e, :, span], stg_g.at[s], sem.at[0, s]),
                pltpu.make_async_copy(wu_hbm.at[e, :, span], stg_u.at[s], sem.at[1, s]),
                pltpu.make_async_copy(wd_hbm.at[e, span, :], stg_d.at[s], sem.at[2, s]))

    def tile(load_weights):
        x = x_ref[...].astype(_bf16)
        if load_weights:
            for cp in chunk_copies(0):
                cp.start()
        width = MOE_FF_CHUNK if load_weights else MOE_FF_WIDE
        for c in range(D_FF_EXPERT // width):
            cols = slice(c * width, (c + 1) * width)
            if load_weights:
                if c + 1 < n_chunks:
                    for cp in chunk_copies(c + 1):
                        cp.start()
                for cp in chunk_copies(c):
                    cp.wait()
                wg_s[:, cols] = stg_g[c % 2].astype(_bf16)
                wu_s[:, cols] = stg_u[c % 2].astype(_bf16)
                wd_s[cols, :] = stg_d[c % 2].astype(_bf16)
            a = _dot(x, wg_s[:, cols])
            mid = (_silu(a) * _dot(x, wu_s[:, cols])).astype(_bf16)
            part = _dot(mid, wd_s[cols, :])
            if c == 0:
                acc[...] = part
            else:
                acc[...] += part
        o_ref[...] = acc[...]

    @pl.when(first_tile)
    def _():
        tile(True)

    @pl.when(jnp.logical_not(first_tile) & (i < n_used))
    def _():
        tile(False)

    @pl.when(i >= n_used)
    def _():
        o_ref[...] = jnp.zeros_like(o_ref)


def _moe_experts(plan, x, wg, wu, wd):
    tile_expert, n_used = plan["tile_expert"], plan["n_used"]
    n_tiles = tile_expert.shape[0]
    d = x.shape[1]
    grid_spec = pltpu.PrefetchScalarGridSpec(
        num_scalar_prefetch=2,
        grid=(n_tiles,),
        in_specs=[
            pl.BlockSpec((MOE_TM, d), lambda i, te, nu: (jnp.minimum(i, nu[0] - 1), 0)),
            pl.BlockSpec(memory_space=pl.ANY),
            pl.BlockSpec(memory_space=pl.ANY),
            pl.BlockSpec(memory_space=pl.ANY),
        ],
        out_specs=pl.BlockSpec((MOE_TM, d), lambda i, te, nu: (i, 0)),
        scratch_shapes=[
            pltpu.VMEM((d, D_FF_EXPERT), _bf16),
            pltpu.VMEM((d, D_FF_EXPERT), _bf16),
            pltpu.VMEM((D_FF_EXPERT, d), _bf16),
            pltpu.VMEM((2, d, MOE_FF_CHUNK), _f32),
            pltpu.VMEM((2, d, MOE_FF_CHUNK), _f32),
            pltpu.VMEM((2, MOE_FF_CHUNK, d), _f32),
            pltpu.SemaphoreType.DMA((3, 2)),
            pltpu.VMEM((MOE_TM, d), _f32),
        ],
    )
    return pl.pallas_call(
        _moe_kernel,
        grid_spec=grid_spec,
        out_shape=jax.ShapeDtypeStruct((n_tiles * MOE_TM, d), _f32),
        compiler_params=_cparams(("arbitrary",), VMEM_LIMIT),
        name="moe_experts",
    )(tile_expert, n_used, x, wg, wu, wd)


def _combine_kernel(gblk, nblk, sblk, totblk, route_ref, h_ref, m_ref, g_ref, y_hbm, o_ref, z, sem):
    t = pl.program_id(0)
    nt = pl.num_programs(0)
    slot = t % 2

    @pl.when(t == 0)
    def _():
        z[...] = jnp.zeros_like(z)
        _block_copies_start(nblk, t, y_hbm, gblk, z.at[0], sblk, sem.at[0])

    @pl.when(t + 1 < nt)
    def _():
        _block_copies_start(nblk, t + 1, y_hbm, gblk, z.at[1 - slot], sblk, sem.at[1 - slot])

    _tile_copies_wait(totblk, t, y_hbm, z.at[slot], sem.at[slot])
    r = route_ref[...]
    col = lax.broadcasted_iota(jnp.int32, (DISP_R, DISP_ROWS), 1).astype(_f32)
    w = jnp.where(col == r[:, 4:5], r[:, 0:1], jnp.where(col == r[:, 5:6], r[:, 1:2], 0.0))
    f = _dot(w.astype(_bf16), z[slot].astype(_bf16))
    hn = h_ref[...] + m_ref[0][5:6] * f
    o_ref[...] = hn * lax.rsqrt(jnp.mean(hn * hn, axis=-1, keepdims=True) + EPS) * g_ref[...]


def _moe_combine(plan, route, h, mods, g, y_sorted, tokens_per_batch):
    n, d = h.shape
    nt = n // DISP_R
    per_b = tokens_per_batch // DISP_R
    grid_spec = pltpu.PrefetchScalarGridSpec(
        num_scalar_prefetch=4,
        grid=(nt,),
        in_specs=[
            pl.BlockSpec((DISP_R, ROUTE_LANES), lambda t, *_: (t, 0)),
            pl.BlockSpec((DISP_R, d), lambda t, *_: (t, 0)),
            pl.BlockSpec((1, 6, d), lambda t, *_: (t // per_b, 0, 0)),
            pl.BlockSpec((1, d), lambda t, *_: (0, 0)),
            pl.BlockSpec(memory_space=pl.ANY),
        ],
        out_specs=pl.BlockSpec((DISP_R, d), lambda t, *_: (t, 0)),
        scratch_shapes=[
            pltpu.VMEM((2, DISP_ROWS, d), _f32),
            pltpu.SemaphoreType.DMA((2,)),
        ],
    )
    return pl.pallas_call(
        _combine_kernel,
        grid_spec=grid_spec,
        out_shape=jax.ShapeDtypeStruct((n, d), _f32),
        compiler_params=_cparams(("arbitrary",), VMEM_LIMIT),
        name="moe_combine",
    )(plan["gblk"], plan["nblk"], plan["sblk"], plan["totblk"], route, h, mods, g, y_sorted)


def _routing_plan(counts, rows):
    nt = counts.shape[0]
    n = nt * DISP_R
    i32 = jnp.int32
    cnt = counts[:, :, 0].astype(i32)
    cpad = (cnt + ROW_BLK - 1) // ROW_BLK * ROW_BLK
    seg = jnp.cumsum(cpad, axis=1) - cpad
    rows_e = jnp.sum(cpad, axis=0)
    tiles_e = (rows_e + MOE_TM - 1) // MOE_TM
    tile_end = jnp.cumsum(tiles_e)
    off = (tile_end - tiles_e) * MOE_TM
    glob = off[None, :] + jnp.cumsum(cpad, axis=0) - cpad
    n_tiles = -(-(2 * n + nt * N_EXPERTS * (ROW_BLK - 1)) // MOE_TM) + N_EXPERTS
    n_used = tile_end[-1]
    ti = jnp.minimum(jnp.arange(n_tiles, dtype=i32), n_used - 1)
    tile_expert = jnp.sum((ti[:, None] >= tile_end[None, :]).astype(i32), axis=1)
    nblk = cpad // ROW_BLK
    return {
        "d": rows,
        "gblk": (glob // ROW_BLK).reshape(-1).astype(i32),
        "nblk": nblk.reshape(-1).astype(i32),
        "sblk": (seg // ROW_BLK).reshape(-1).astype(i32),
        "totblk": jnp.stack([jnp.sum(nblk // BIG_BLKS, axis=1), jnp.sum(nblk % BIG_BLKS, axis=1)],
                            axis=1).reshape(-1).astype(i32),
        "tail_blk": jnp.append((off + rows_e) // ROW_BLK, n_used * (MOE_TM // ROW_BLK)).astype(i32),
        "tail_n": jnp.append((tiles_e * MOE_TM - rows_e) // ROW_BLK,
                             (n_tiles - n_used) * (MOE_TM // ROW_BLK)).astype(i32),
        "tile_expert": tile_expert.astype(i32),
        "n_used": n_used.reshape(1).astype(i32),
        "n_rows": n_tiles * MOE_TM,
    }


def _permuted_w_in(w_in_l):
    d = w_in_l.shape[0]
    c = {}
    start = 0
    for name, size in (("na_qkv", 3 * NA_WIDTH), ("conv_glu", 2 * CONV_CH), ("ret_q", RET_QK_WIDTH),
                       ("ret_k", RET_QK_WIDTH), ("ret_v", RET_WIDTH), ("ret_g", 2 * RET_WIDTH)):
        c[name] = w_in_l[:, start:start + size]
        start += size
    sub = RET_QK_DIM // 2
    halves_first = lambda w: w.reshape(d, RET_HEADS, 2, sub).transpose(0, 2, 1, 3).reshape(d, RET_QK_WIDTH)
    wa = jnp.concatenate([c["na_qkv"], c["ret_v"]], axis=1)
    wb = jnp.concatenate([c["conv_glu"], halves_first(c["ret_q"]), halves_first(c["ret_k"]), c["ret_g"]], axis=1)
    return wa.astype(_bf16), wb.astype(_bf16)


def kernel(x, c, ctx, c_ctx, w_mod, b_mod, norm1_w, norm2_w, w_in, w_out, na_rpb, conv_w, conv_b, conv_ln_w,
           conv_ln_b, ret_decay, ret_gn_w, ffn_w_gate, ffn_w_up, ffn_w_down, moe_router, moe_router_b,
           moe_w_gate, moe_w_up, moe_w_down, final_norm_w):
    b, t, d = x.shape
    lc = ctx.shape[1]
    assert d == D_MODEL and t % Q_TILE == 0 and t // GRID_W >= K_ROWS and b + 1 <= MOD_ROWS
    assert t % DISP_R == 0 and lc % 256 == 0 and (2 * b * t) % MOE_TM == 0
    assert DEPTH == 2

    cvecs = jnp.zeros((MOD_ROWS, d), _f32).at[:b].set(c).at[b].set(c_ctx)
    mods = _mod_vectors(cvecs, w_mod, b_mod).reshape(DEPTH, MOD_ROWS, 6, d)
    cos_lat, sin_lat = _rope_tables(t)
    cos_ctx = jnp.ones((lc, RET_QK_WIDTH), _f32)
    sin_ctx = jnp.zeros((lc, RET_QK_WIDTH), _f32)
    vec = lambda a: a.reshape(1, -1)
    lat_tm = DISP_R
    flat = lambda a: a.reshape(1, b * lc, a.shape[-1])
    unflat = lambda a: a.reshape(b, lc, a.shape[-1])
    ctx_tm = min(DISP_R, b * lc)

    h_lat, h_ctx = x, ctx
    out = None
    for l in range(DEPTH):
        last = l == DEPTH - 1
        m_lat = mods[l, :b]
        m_ctx = mods[l, b:b + 1]
        wa, wb = _permuted_w_in(w_in[l])
        wo = w_out[l].astype(_bf16)
        gamma = 1.0 - jnp.exp2(-ret_decay[l].astype(_f32))
        lg = jnp.log(gamma)
        lgq = jnp.tile(jnp.repeat(lg, RET_QK_DIM // 2, axis=1), (1, 2)).reshape(2, 1, RET_QK_WIDTH)
        lgv = jnp.repeat(lg, RET_V_DIM, axis=1).reshape(2, 1, RET_WIDTH)

        a_lat, b_lat = _inproj(h_lat, m_lat, vec(norm1_w[l]), wa, wb, lat_tm)
        a_ctx, b_ctx = map(unflat, _inproj(flat(h_ctx), m_ctx, vec(norm1_w[l]), wa, wb, ctx_tm))

        y_na = _na_attention(a_lat, a_ctx, na_rpb[l])
        y_conv = _conv_module(b_lat, conv_w[l], conv_b[l], conv_ln_w[l], conv_ln_b[l])
        s_zero = jnp.zeros((b, 2, RET_QK_WIDTH, RET_WIDTH), _f32)
        y_ret_c, s_ctx = _retention(a_ctx, b_ctx, cos_ctx, sin_ctx, s_zero, lg, lgq, lgv, ret_gn_w[l])
        y_ret, _ = _retention(a_lat, b_lat, cos_lat, sin_lat, s_ctx, lg, lgq, lgv, ret_gn_w[l])

        if not last:
            wg = ffn_w_gate[l // 2].astype(_bf16)
            wu = ffn_w_up[l // 2].astype(_bf16)
            wd = ffn_w_down[l // 2].astype(_bf16)
            y_na_c = _ctx_attention(a_ctx)
            y_conv_c = _conv_module(b_ctx, conv_w[l], conv_b[l], conv_ln_w[l], conv_ln_b[l])
            h_ctx = unflat(_outproj_ffn(flat(y_na_c), flat(y_conv_c), flat(y_ret_c), flat(h_ctx), wo, m_ctx,
                                        vec(norm2_w[l]), wg, wu, wd, ctx_tm))
            h_lat = _outproj_ffn(y_na, y_conv, y_ret, h_lat, wo, m_lat, vec(norm2_w[l]), wg, wu, wd, lat_tm)
        else:
            rw = jnp.zeros((d, ROUTE_LANES), _f32).at[:, :N_EXPERTS].set(moe_router[l // 2])
            rb = jnp.full((1, ROUTE_LANES), NEG_INF, _f32).at[0, :N_EXPERTS].set(moe_router_b[l // 2])
            h_lat, u_lat, route, rows, counts = _outproj_route(y_na, y_conv, y_ret, h_lat, wo, m_lat,
                                                               vec(norm2_w[l]), rw, rb)
            route = route.reshape(b * t, ROUTE_LANES)
            plan = _routing_plan(counts, rows)
            x_sorted = _moe_dispatch(plan, u_lat.reshape(b * t, d), plan["n_rows"])
            y_sorted = _moe_experts(plan, x_sorted, moe_w_gate[l // 2], moe_w_up[l // 2], moe_w_down[l // 2])
            out = _moe_combine(plan, route, h_lat.reshape(b * t, d), m_lat, vec(final_norm_w), y_sorted, t)
            out = out.reshape(b, t, d)
    return out
```

```python
import numpy as np
import jax
import jax.numpy as jnp
from jax import lax
from jax.experimental import pallas as pl
from jax.experimental.pallas import tpu as pltpu

D_MODEL = 1024
DEPTH = 2
GRID_W = 64
NA_HEAD_DIM = 64
NA_WIDTH = 512
NA_HEADS = 8
NB_ROWS = 8
NB_COLS = 16
CONV_CH = 256
CONV_WIDTH = 31
RET_WIDTH = 256
RET_HEADS = 4
RET_V_DIM = 64
RET_QK_DIM = 32
RET_QK_WIDTH = 128
D_FF = 2816
N_EXPERTS = 8
D_FF_EXPERT = 3584
ROPE_BASE = 10000.0
EPS = 1e-6
NEG_INF = -1e30

A_WIDTH = 3 * NA_WIDTH + RET_WIDTH
B_WIDTH = 2 * CONV_CH + 2 * RET_QK_WIDTH + 2 * RET_WIDTH

HEAD_GROUP = 4
HG_LANES = HEAD_GROUP * NA_HEAD_DIM
Q_ROWS = 4
Q_TILE = Q_ROWS * GRID_W
K_ROWS = 12
K_TILE = K_ROWS * GRID_W
RET_CHUNK = 256
RET_UNROLL = 8
CONV_CHUNK = 128
CONV_PAD = 16
SUBLANES = 8
LANES = 128
MOE_TM = 512
MOE_FF_CHUNK = 512
MOE_FF_WIDE = 1792
ROW_BLK = SUBLANES
BIG_SHIFT = 3
BIG_BLKS = 1 << BIG_SHIFT
DISP_R = 512
DISP_ROWS = 2 * DISP_R + N_EXPERTS * ROW_BLK
ROUTE_LANES = 128
VMEM_LIMIT = 56 * 1024 * 1024

_f32 = jnp.float32
_bf16 = jnp.bfloat16


def _cparams(sem, vmem=None):
    return pltpu.CompilerParams(dimension_semantics=sem, vmem_limit_bytes=vmem)


def _resident(shape):
    return pl.BlockSpec(shape, lambda *_: (0,) * len(shape), pipeline_mode=pl.Buffered(1))


def _sigmoid(x):
    return 1.0 / (1.0 + jnp.exp(-x))


def _silu(x):
    return x * _sigmoid(x)


def _dot(a, b):
    return jnp.dot(a, b, preferred_element_type=_f32)


def _dot_nt(a, b):
    return lax.dot_general(a, b, (((1,), (1,)), ((), ())), preferred_element_type=_f32)


def _dot_tn(a, b):
    return lax.dot_general(a, b, (((0,), (0,)), ((), ())), preferred_element_type=_f32)


def _split_bf16(a):
    hi = a.astype(_bf16)
    lo = (a - hi.astype(_f32)).astype(_bf16)
    return hi, lo


def _dot_split(a, b):
    ah, al = _split_bf16(a)
    bh, bl = _split_bf16(b)
    return _dot(ah, bh) + _dot(al, bh) + _dot(ah, bl)


def _rms_modulate(x, g, shift, scale):
    y = x * lax.rsqrt(jnp.mean(x * x, axis=-1, keepdims=True) + EPS)
    return (y * g) * (1.0 + scale) + shift


MOD_ROWS = 24
MOD_TN = 1536


def _mod_kernel(c_ref, w_ref, b_ref, o_ref):
    s = _silu(c_ref[...])
    o_ref[0] = _dot_split(s, w_ref[0]) + b_ref[0]


def _mod_vectors(cvecs, w_mod, b_mod):
    n = w_mod.shape[2]
    return pl.pallas_call(
        _mod_kernel,
        grid=(DEPTH, n // MOD_TN),
        in_specs=[
            pl.BlockSpec((MOD_ROWS, D_MODEL), lambda l, j: (0, 0)),
            pl.BlockSpec((1, D_MODEL, MOD_TN), lambda l, j: (l, 0, j)),
            pl.BlockSpec((1, 1, MOD_TN), lambda l, j: (l, 0, j)),
        ],
        out_specs=pl.BlockSpec((1, MOD_ROWS, MOD_TN), lambda l, j: (l, 0, j)),
        out_shape=jax.ShapeDtypeStruct((DEPTH, MOD_ROWS, n), _f32),
        compiler_params=_cparams(("arbitrary", "arbitrary"), VMEM_LIMIT),
        name="mod_vectors",
    )(cvecs, w_mod, b_mod.reshape(DEPTH, 1, n))


def _inproj_kernel(x_ref, m_ref, g_ref, wa_ref, wb_ref, oa_ref, ob_ref):
    m = m_ref[0]
    u = _rms_modulate(x_ref[0], g_ref[...], m[0:1], m[1:2]).astype(_bf16)
    oa_ref[0] = _dot(u, wa_ref[...]).astype(_bf16)
    ob_ref[0] = _dot(u, wb_ref[...])


def _inproj(x, mods, g, wa, wb, tm):
    b, l, d = x.shape
    nb = mods.shape[0]
    mod_map = (lambda i, j: (i, 0, 0)) if nb > 1 else (lambda i, j: (0, 0, 0))
    return pl.pallas_call(
        _inproj_kernel,
        grid=(b, l // tm),
        in_specs=[
            pl.BlockSpec((1, tm, d), lambda i, j: (i, j, 0)),
            pl.BlockSpec((1, 6, d), mod_map),
            pl.BlockSpec((1, d), lambda i, j: (0, 0)),
            _resident((d, A_WIDTH)),
            _resident((d, B_WIDTH)),
        ],
        out_specs=[
            pl.BlockSpec((1, tm, A_WIDTH), lambda i, j: (i, j, 0)),
            pl.BlockSpec((1, tm, B_WIDTH), lambda i, j: (i, j, 0)),
        ],
        out_shape=[
            jax.ShapeDtypeStruct((b, l, A_WIDTH), _bf16),
            jax.ShapeDtypeStruct((b, l, B_WIDTH), _f32),
        ],
        compiler_params=_cparams(("arbitrary", "arbitrary"), VMEM_LIMIT),
        name="inproj",
    )(x, mods, g, wa, wb)


def _masked_heads_attention(q, keys, vals, n_biased, bias, lane):
    out = jnp.zeros((q.shape[0], HG_LANES), _f32)
    for h in range(HEAD_GROUP):
        hm = (lane >= h * NA_HEAD_DIM) & (lane < (h + 1) * NA_HEAD_DIM)
        qm = jnp.where(hm, q, jnp.zeros_like(q)) * jnp.asarray(NA_HEAD_DIM ** -0.5, q.dtype)
        s = _dot_nt(qm, keys)
        parts = [s[:, :n_biased] + bias(h), s[:, n_biased:]] if n_biased else [s]
        mx = parts[0].max(axis=-1, keepdims=True)
        for part in parts[1:]:
            mx = jnp.maximum(mx, part.max(axis=-1, keepdims=True))
        probs = [jnp.exp(part - mx) for part in parts]
        den = probs[0].sum(axis=-1, keepdims=True)
        for p in probs[1:]:
            den = den + p.sum(axis=-1, keepdims=True)
        p = jnp.concatenate([p.astype(_bf16) for p in probs], axis=-1)
        out = jnp.where(hm, _dot(p, vals) / den, out)
    return out


def _fill_bias_table(rp_ref, bias_ref):
    lane = lax.broadcasted_iota(jnp.int32, (GRID_W, LANES), 1)
    qcol = lax.broadcasted_iota(jnp.int32, (GRID_W, LANES), 0)
    kcol = lane % GRID_W
    win0 = jnp.clip(qcol - NB_COLS // 2, 0, GRID_W - NB_COLS)
    col_ok = (kcol >= win0) & (kcol < win0 + NB_COLS)
    low_half = lane < GRID_W
    masked = jnp.full((GRID_W, LANES), NEG_INF, _f32)
    classes = ((lambda i: 0, NB_ROWS - 1),
               (lambda i: i, NB_ROWS - 1 - NB_ROWS // 2),
               (lambda i: K_ROWS - NB_ROWS, NB_ROWS - 1 - (K_ROWS - Q_ROWS)))
    for h in range(HEAD_GROUP):
        rolled = {}

        def block(dr, half):
            if (dr, half) not in rolled:
                row = jnp.broadcast_to(rp_ref[h, dr:dr + 1, :], (GRID_W, LANES))
                shift = (half * GRID_W - (NB_COLS - 1)) % LANES
                rolled[(dr, half)] = pltpu.roll(row, shift, 1, stride=1, stride_axis=0)
            return rolled[(dr, half)]

        for cls, (off, dr0) in enumerate(classes):
            for i in range(Q_ROWS):
                for jp in range(K_ROWS // 2):
                    parts = []
                    for half in range(2):
                        j = 2 * jp + half
                        ok = off(i) <= j < off(i) + NB_ROWS
                        parts.append(block(j - i + dr0, half) if ok else masked)
                    tile = jnp.where(col_ok, jnp.where(low_half, parts[0], parts[1]), NEG_INF)
                    bias_ref[cls, h, i * GRID_W:(i + 1) * GRID_W, jp * LANES:(jp + 1) * LANES] = tile


def _na_kernel(q_ref, k_ref, v_ref, kc_ref, vc_ref, rp_ref, o_ref, kall, vall, bias_ref):
    n_tiles = q_ref.shape[1] // Q_TILE
    rows = q_ref.shape[1] // GRID_W
    lane = lax.broadcasted_iota(jnp.int32, (1, HG_LANES), 1)

    @pl.when(pl.program_id(1) == 0)
    def _():
        _fill_bias_table(rp_ref, bias_ref)

    kall[K_TILE:, :] = kc_ref[0]
    vall[K_TILE:, :] = vc_ref[0]

    def body(g, carry):
        krow0 = jnp.clip(Q_ROWS * g - NB_ROWS // 2, 0, rows - K_ROWS)
        start = pl.multiple_of(krow0 * GRID_W, GRID_W)
        cls = jnp.where(g == 0, 0, jnp.where(g == n_tiles - 1, 2, 1))
        qs = pl.multiple_of(g * Q_TILE, Q_TILE)
        q = q_ref[0, pl.ds(qs, Q_TILE), :]
        kall[0:K_TILE, :] = k_ref[0, pl.ds(start, K_TILE), :]
        vall[0:K_TILE, :] = v_ref[0, pl.ds(start, K_TILE), :]
        out = _masked_heads_attention(q, kall[...], vall[...], K_TILE, lambda h: bias_ref[cls, h], lane)
        o_ref[0, pl.ds(qs, Q_TILE), :] = out.astype(o_ref.dtype)
        return carry

    lax.fori_loop(0, n_tiles, body, 0, unroll=8)


def _na_attention(a_lat, a_ctx, rpb):
    b, t, _ = a_lat.shape
    lc = a_ctx.shape[1]
    ng = NA_HEADS // HEAD_GROUP
    kq, kk, kv = 0, NA_WIDTH // HG_LANES, 2 * NA_WIDTH // HG_LANES
    rp = jnp.pad(rpb.astype(_f32), ((0, 0), (0, 1), (0, LANES - rpb.shape[2])))
    return pl.pallas_call(
        _na_kernel,
        grid=(ng, b),
        in_specs=[
            pl.BlockSpec((1, t, HG_LANES), lambda g, i: (i, 0, kq + g)),
            pl.BlockSpec((1, t, HG_LANES), lambda g, i: (i, 0, kk + g)),
            pl.BlockSpec((1, t, HG_LANES), lambda g, i: (i, 0, kv + g)),
            pl.BlockSpec((1, lc, HG_LANES), lambda g, i: (i, 0, kk + g)),
            pl.BlockSpec((1, lc, HG_LANES), lambda g, i: (i, 0, kv + g)),
            pl.BlockSpec((HEAD_GROUP, 2 * NB_ROWS, LANES), lambda g, i: (g, 0, 0)),
        ],
        out_specs=pl.BlockSpec((1, t, HG_LANES), lambda g, i: (i, 0, g)),
        out_shape=jax.ShapeDtypeStruct((b, t, NA_WIDTH), _bf16),
        scratch_shapes=[pltpu.VMEM((K_TILE + lc, HG_LANES), _bf16),
                        pltpu.VMEM((K_TILE + lc, HG_LANES), _bf16),
                        pltpu.VMEM((3, HEAD_GROUP, Q_TILE, K_TILE), _f32)],
        compiler_params=_cparams(("arbitrary", "arbitrary"), VMEM_LIMIT),
        name="na_attention",
    )(a_lat, a_lat, a_lat, a_ctx, a_ctx, rp)


def _ctx_attn_kernel(q_ref, k_ref, v_ref, o_ref):
    lane = lax.broadcasted_iota(jnp.int32, (1, HG_LANES), 1)
    out = _masked_heads_attention(q_ref[0], k_ref[0], v_ref[0], 0, None, lane)
    o_ref[0] = out.astype(o_ref.dtype)


def _ctx_attention(a_ctx):
    b, lc, _ = a_ctx.shape
    ng = NA_HEADS // HEAD_GROUP
    kq, kk, kv = 0, NA_WIDTH // HG_LANES, 2 * NA_WIDTH // HG_LANES
    return pl.pallas_call(
        _ctx_attn_kernel,
        grid=(ng, b),
        in_specs=[
            pl.BlockSpec((1, lc, HG_LANES), lambda g, i: (i, 0, kq + g)),
            pl.BlockSpec((1, lc, HG_LANES), lambda g, i: (i, 0, kk + g)),
            pl.BlockSpec((1, lc, HG_LANES), lambda g, i: (i, 0, kv + g)),
        ],
        out_specs=pl.BlockSpec((1, lc, HG_LANES), lambda g, i: (i, 0, g)),
        out_shape=jax.ShapeDtypeStruct((b, lc, NA_WIDTH), _bf16),
        compiler_params=_cparams(("arbitrary", "arbitrary")),
        name="ctx_attention",
    )(a_ctx, a_ctx, a_ctx)


def _conv_kernel(u_ref, w_ref, b_ref, lnw_ref, lnb_ref, o_ref, ypad):
    l = u_ref.shape[1]
    ypad[0:CONV_PAD, :] = jnp.zeros((CONV_PAD, CONV_CH), _f32)
    ypad[CONV_PAD + l:2 * CONV_PAD + l, :] = jnp.zeros((CONV_PAD, CONV_CH), _f32)
    ypad[CONV_PAD:CONV_PAD + l, :] = u_ref[0, :, 0:CONV_CH] * _sigmoid(u_ref[0, :, CONV_CH:2 * CONV_CH])
    shift = CONV_PAD - CONV_WIDTH // 2

    def body(c, carry):
        base = pl.multiple_of(c * CONV_CHUNK, CONV_CHUNK)
        win = ypad[pl.ds(base, CONV_CHUNK + 2 * CONV_PAD), :]
        acc = jnp.zeros((CONV_CHUNK, CONV_CH), _f32)
        for r in range(SUBLANES):
            offs = [o for o in range(shift, shift + CONV_WIDTH) if o % SUBLANES == r]
            wr = win if r == 0 else pltpu.roll(win, win.shape[0] - r, 0)
            for o in offs:
                acc = acc + wr[o - r:o - r + CONV_CHUNK, :] * w_ref[o - shift:o - shift + 1, :]
        y = acc + b_ref[...]
        mu = jnp.mean(y, axis=-1, keepdims=True)
        yc = y - mu
        var = jnp.mean(yc * yc, axis=-1, keepdims=True)
        z = yc * lax.rsqrt(var + EPS) * lnw_ref[...] + lnb_ref[...]
        o_ref[0, pl.ds(base, CONV_CHUNK), :] = _silu(z).astype(o_ref.dtype)
        return carry

    lax.fori_loop(0, l // CONV_CHUNK, body, 0, unroll=2)


def _conv_module(bf, conv_w, conv_b, ln_w, ln_b):
    b, l, _ = bf.shape
    vec = lambda a: a.reshape(1, CONV_CH)
    return pl.pallas_call(
        _conv_kernel,
        grid=(b,),
        in_specs=[
            pl.BlockSpec((1, l, 2 * CONV_CH), lambda i: (i, 0, 0)),
            pl.BlockSpec((CONV_WIDTH, CONV_CH), lambda i: (0, 0)),
            pl.BlockSpec((1, CONV_CH), lambda i: (0, 0)),
            pl.BlockSpec((1, CONV_CH), lambda i: (0, 0)),
            pl.BlockSpec((1, CONV_CH), lambda i: (0, 0)),
        ],
        out_specs=pl.BlockSpec((1, l, CONV_CH), lambda i: (i, 0, 0)),
        out_shape=jax.ShapeDtypeStruct((b, l, CONV_CH), _bf16),
        scratch_shapes=[pltpu.VMEM((l + 2 * CONV_PAD, CONV_CH), _f32)],
        compiler_params=_cparams(("arbitrary",), VMEM_LIMIT),
        name="conv_module",
    )(bf, conv_w, vec(conv_b), vec(ln_w), vec(ln_b))


def _ret_kernel(lg_ref, q_ref, k_ref, v_ref, gf_ref, gb_ref, cos_ref, sin_ref, s0_ref, lgq_ref, lgv_ref,
                gnw_ref, y_ref, sfin_ref, dmat, of_s, ob_s, st_s):
    l = q_ref.shape[1]
    c = min(RET_CHUNK, l)
    nc = l // c
    half = RET_QK_WIDTH // 2
    sub = RET_QK_DIM // 2
    ii = lax.broadcasted_iota(jnp.int32, (c, c), 0)
    jj = lax.broadcasted_iota(jnp.int32, (c, c), 1)
    diff = (ii - jj).astype(_f32)
    for h in range(RET_HEADS):
        dmat[0, h] = jnp.where(diff >= 0, jnp.exp(lg_ref[0, h] * jnp.maximum(diff, 0.0)), 0.0)
        dmat[1, h] = jnp.where(diff <= 0, jnp.exp(lg_ref[1, h] * jnp.maximum(-diff, 0.0)), 0.0)
    pos = lax.broadcasted_iota(jnp.int32, (c, 1), 0).astype(_f32)
    lane_q = lax.broadcasted_iota(jnp.int32, (1, RET_QK_WIDTH), 1)
    head_q = (lane_q % half) // sub
    lane_v = lax.broadcasted_iota(jnp.int32, (1, RET_WIDTH), 1)
    head_v = lane_v // RET_V_DIM
    row_h = (lax.broadcasted_iota(jnp.int32, (RET_QK_WIDTH, RET_WIDTH), 0) % half) // sub
    col_h = lax.broadcasted_iota(jnp.int32, (RET_QK_WIDTH, RET_WIDTH), 1) // RET_V_DIM
    blockmask = row_h == col_h
    q_dec = (jnp.exp(lgq_ref[0] * (pos + 1.0)), jnp.exp(lgq_ref[1] * (c - pos)))
    k_dec = (jnp.exp(lgq_ref[0] * (c - 1.0 - pos)), jnp.exp(lgq_ref[1] * pos))
    c_dec = (jnp.exp(lgv_ref[0] * float(c)), jnp.exp(lgv_ref[1] * float(c)))
    k_scale = RET_QK_DIM ** -0.5
    st_s[...] = s0_ref[0]

    def step(n, carry):
        for d in range(2):
            cidx = n if d == 0 else nc - 1 - n
            base = pl.multiple_of(cidx * c, c)
            cs = cos_ref[pl.ds(base, c), :]
            sn = sin_ref[pl.ds(base, c), :]
            q = q_ref[0, pl.ds(base, c), :]
            k = k_ref[0, pl.ds(base, c), :]
            qr = q * cs + pltpu.roll(q, half, 1) * sn
            kr = (k * cs + pltpu.roll(k, half, 1) * sn) * k_scale
            v = v_ref[0, pl.ds(base, c), :]
            qb = qr.astype(_bf16)
            kb = kr.astype(_bf16)
            o = _dot((qr * q_dec[d]).astype(_bf16), st_s[d].astype(_bf16))
            for h in range(RET_HEADS):
                s = _dot_nt(jnp.where(head_q == h, qb, jnp.zeros_like(qb)), kb)
                inner = (s * dmat[d, h]).astype(_bf16)
                o = o + _dot(inner, jnp.where(head_v == h, v, jnp.zeros_like(v)))
            if d == 0:
                of_s[pl.ds(base, c), :] = o
            else:
                ob_s[pl.ds(base, c), :] = o
            upd = _dot_tn((kr * k_dec[d]).astype(_bf16), v)
            st_s[d] = c_dec[d] * st_s[d] + jnp.where(blockmask, upd, 0.0)
        return carry

    lax.fori_loop(0, nc, step, 0, unroll=RET_UNROLL if nc % RET_UNROLL == 0 else 1)
    sfin_ref[0] = st_s[...]

    gi = lax.broadcasted_iota(jnp.int32, (RET_WIDTH, RET_WIDTH), 0) // RET_V_DIM
    gj = lax.broadcasted_iota(jnp.int32, (RET_WIDTH, RET_WIDTH), 1) // RET_V_DIM
    gmean = jnp.where(gi == gj, 1.0 / RET_V_DIM, 0.0).astype(_bf16)

    def group_mean(a):
        return _dot(a.astype(_bf16), gmean)

    def head_norm(o):
        dlt = o - group_mean(o)
        var = group_mean(dlt * dlt)
        return dlt * lax.rsqrt(var + EPS) * gnw_ref[...]

    def fin(n, carry):
        base = pl.multiple_of(n * c, c)
        yf = head_norm(of_s[pl.ds(base, c), :])
        yb = head_norm(ob_s[pl.ds(base, c), :])
        y = _silu(gf_ref[0, pl.ds(base, c), :]) * yf + _silu(gb_ref[0, pl.ds(base, c), :]) * yb
        y_ref[0, pl.ds(base, c), :] = y.astype(y_ref.dtype)
        return carry

    lax.fori_loop(0, nc, fin, 0, unroll=RET_UNROLL if nc % RET_UNROLL == 0 else 1)


def _retention(a, bf, cos_t, sin_t, s0, lg, lgq, lgv, gn_w):
    b, l, _ = a.shape
    c = min(RET_CHUNK, l)
    qi = 2 * CONV_CH // RET_QK_WIDTH
    gi = (2 * CONV_CH + 2 * RET_QK_WIDTH) // RET_WIDTH
    vi = 3 * NA_WIDTH // RET_WIDTH
    return pl.pallas_call(
        _ret_kernel,
        grid=(b,),
        in_specs=[
            pl.BlockSpec(memory_space=pltpu.SMEM),
            pl.BlockSpec((1, l, RET_QK_WIDTH), lambda i: (i, 0, qi)),
            pl.BlockSpec((1, l, RET_QK_WIDTH), lambda i: (i, 0, qi + 1)),
            pl.BlockSpec((1, l, RET_WIDTH), lambda i: (i, 0, vi)),
            pl.BlockSpec((1, l, RET_WIDTH), lambda i: (i, 0, gi)),
            pl.BlockSpec((1, l, RET_WIDTH), lambda i: (i, 0, gi + 1)),
            pl.BlockSpec((l, RET_QK_WIDTH), lambda i: (0, 0)),
            pl.BlockSpec((l, RET_QK_WIDTH), lambda i: (0, 0)),
            pl.BlockSpec((1, 2, RET_QK_WIDTH, RET_WIDTH), lambda i: (i, 0, 0, 0)),
            pl.BlockSpec((2, 1, RET_QK_WIDTH), lambda i: (0, 0, 0)),
            pl.BlockSpec((2, 1, RET_WIDTH), lambda i: (0, 0, 0)),
            pl.BlockSpec((1, RET_WIDTH), lambda i: (0, 0)),
        ],
        out_specs=[
            pl.BlockSpec((1, l, RET_WIDTH), lambda i: (i, 0, 0)),
            pl.BlockSpec((1, 2, RET_QK_WIDTH, RET_WIDTH), lambda i: (i, 0, 0, 0)),
        ],
        out_shape=[
            jax.ShapeDtypeStruct((b, l, RET_WIDTH), _bf16),
            jax.ShapeDtypeStruct((b, 2, RET_QK_WIDTH, RET_WIDTH), _f32),
        ],
        scratch_shapes=[
            pltpu.VMEM((2, RET_HEADS, c, c), _f32),
            pltpu.VMEM((l, RET_WIDTH), _f32),
            pltpu.VMEM((l, RET_WIDTH), _f32),
            pltpu.VMEM((2, RET_QK_WIDTH, RET_WIDTH), _f32),
        ],
        compiler_params=_cparams(("arbitrary",), VMEM_LIMIT),
        name="retention",
    )(lg, bf, bf, a, bf, bf, cos_t, sin_t, s0, lgq, lgv, gn_w.reshape(1, RET_WIDTH))


def _rope_tables(t_len):
    t = np.arange(t_len)
    row = (t // GRID_W).astype(np.float32)
    col = (t % GRID_W).astype(np.float32)
    axis_dim = RET_QK_DIM // 2
    inv = jnp.asarray(ROPE_BASE, _f32) ** (-jnp.arange(0, axis_dim, 2, dtype=_f32) / axis_dim)
    ang = jnp.concatenate([jnp.asarray(row)[:, None] * inv, jnp.asarray(col)[:, None] * inv], axis=-1)
    cos, sin = jnp.cos(ang), jnp.sin(ang)
    cos_t = jnp.tile(cos, (1, 2 * RET_HEADS))
    sin_t = jnp.concatenate([jnp.tile(-sin, (1, RET_HEADS)), jnp.tile(sin, (1, RET_HEADS))], axis=-1)
    return cos_t, sin_t


def _top2_route(logits):
    m = logits.shape[0]
    lt = logits.T[0:N_EXPERTS, :]
    sub = lax.broadcasted_iota(jnp.int32, lt.shape, 0).astype(_f32)
    m1 = lt.max(axis=0, keepdims=True)
    i1 = jnp.where(lt == m1, sub, float(N_EXPERTS)).min(axis=0, keepdims=True)
    rest = jnp.where(sub == i1, -jnp.inf, lt)
    m2 = rest.max(axis=0, keepdims=True)
    i2 = jnp.where(rest == m2, sub, float(N_EXPERTS)).min(axis=0, keepdims=True)
    e = jnp.exp(m2 - m1)
    p1 = 1.0 / (1.0 + e)
    p2 = e * p1
    oh1 = jnp.where(sub == i1, 1.0, 0.0)
    oh2 = jnp.where(sub == i2, 1.0, 0.0)
    cnt = oh1 + oh2
    before = (lax.broadcasted_iota(jnp.int32, (m, m), 0) < lax.broadcasted_iota(jnp.int32, (m, m), 1))
    pref = _dot(cnt.astype(_bf16), jnp.where(before, 1.0, 0.0).astype(_bf16))
    tot = cnt.sum(axis=1, keepdims=True)
    cpad = jnp.floor((tot + (ROW_BLK - 1.0)) * (1.0 / ROW_BLK)) * ROW_BLK
    d1 = (pref * oh1).sum(axis=0, keepdims=True)
    d2 = (pref * oh2).sum(axis=0, keepdims=True)
    seg = jnp.zeros((1, 1), _f32)
    for ex in range(N_EXPERTS):
        d1 = d1 + oh1[ex:ex + 1, :] * seg
        d2 = d2 + oh2[ex:ex + 1, :] * seg
        seg = seg + cpad[ex:ex + 1, :]
    rows8 = jnp.concatenate([p1, p2, i1, i2, d1, d2, jnp.zeros((ROUTE_LANES - 6, m), _f32)], axis=0)
    return (rows8.T, jnp.concatenate([d1, d2], axis=0).astype(jnp.int32),
            jnp.broadcast_to(tot, (N_EXPERTS, ROUTE_LANES)))


def _outproj_route_kernel(yna, yconv, yret, h_ref, w_ref, m_ref, g_ref, rw_ref, rb_ref,
                          ho_ref, u_ref, route_ref, rows_ref, cnt_ref):
    y = jnp.concatenate([yna[0], yconv[0], yret[0]], axis=-1)
    m = m_ref[0]
    hn = h_ref[0] + m[2:3] * _dot(y, w_ref[...])
    ho_ref[0] = hn
    u = _rms_modulate(hn, g_ref[...], m[3:4], m[4:5])
    u_ref[0] = u.astype(u_ref.dtype)
    route_ref[0], rows_ref[0], cnt_ref[0] = _top2_route(_dot_split(u, rw_ref[...]) + rb_ref[...])


def _outproj_route(y_na, y_conv, y_ret, h, w_out, mods, g2, rw, rb):
    b, l, d = h.shape
    tm = DISP_R
    nj = l // tm
    tok = lambda w: pl.BlockSpec((1, tm, w), lambda i, j: (i, j, 0))
    tile = lambda r, w: pl.BlockSpec((1, r, w), lambda i, j: (i * nj + j, 0, 0))
    return pl.pallas_call(
        _outproj_route_kernel,
        grid=(b, nj),
        in_specs=[tok(NA_WIDTH), tok(CONV_CH), tok(RET_WIDTH), tok(d), _resident((d, d)),
                  pl.BlockSpec((1, 6, d), lambda i, j: (i, 0, 0)),
                  pl.BlockSpec((1, d), lambda i, j: (0, 0)),
                  pl.BlockSpec((d, ROUTE_LANES), lambda i, j: (0, 0)),
                  pl.BlockSpec((1, ROUTE_LANES), lambda i, j: (0, 0))],
        out_specs=[tok(d), tok(d), tok(ROUTE_LANES), tile(2, tm), tile(N_EXPERTS, ROUTE_LANES)],
        out_shape=[jax.ShapeDtypeStruct((b, l, d), _f32), jax.ShapeDtypeStruct((b, l, d), _bf16),
                   jax.ShapeDtypeStruct((b, l, ROUTE_LANES), _f32),
                   jax.ShapeDtypeStruct((b * nj, 2, tm), jnp.int32),
                   jax.ShapeDtypeStruct((b * nj, N_EXPERTS, ROUTE_LANES), _f32)],
        compiler_params=_cparams(("arbitrary", "arbitrary"), VMEM_LIMIT),
        name="outproj_route",
    )(y_na, y_conv, y_ret, h, w_out, mods, g2, rw, rb)


def _outproj_ffn_kernel(yna, yconv, yret, h_ref, wo_ref, m_ref, g_ref, wg_ref, wu_ref, wd_ref, o_ref):
    y = jnp.concatenate([yna[0], yconv[0], yret[0]], axis=-1)
    m = m_ref[0]
    hn = h_ref[0] + m[2:3] * _dot(y, wo_ref[...])
    u = _rms_modulate(hn, g_ref[...], m[3:4], m[4:5]).astype(_bf16)
    a = _dot(u, wg_ref[...])
    mid = (_silu(a) * _dot(u, wu_ref[...])).astype(_bf16)
    o_ref[0] = hn + m[5:6] * _dot(mid, wd_ref[...])


def _outproj_ffn(y_na, y_conv, y_ret, h, w_out, mods, g2, wg, wu, wd, tm):
    b, l, d = h.shape
    nb = mods.shape[0]
    dff = wg.shape[1]
    mod_map = (lambda i, j: (i, 0, 0)) if nb > 1 else (lambda i, j: (0, 0, 0))
    tok = lambda w: pl.BlockSpec((1, tm, w), lambda i, j: (i, j, 0))
    return pl.pallas_call(
        _outproj_ffn_kernel,
        grid=(b, l // tm),
        in_specs=[tok(NA_WIDTH), tok(CONV_CH), tok(RET_WIDTH), tok(d), _resident((d, d)),
                  pl.BlockSpec((1, 6, d), mod_map), pl.BlockSpec((1, d), lambda i, j: (0, 0)),
                  _resident((d, dff)), _resident((d, dff)), _resident((dff, d))],
        out_specs=tok(d),
        out_shape=jax.ShapeDtypeStruct((b, l, d), _f32),
        compiler_params=_cparams(("arbitrary", "arbitrary"), VMEM_LIMIT),
        name="outproj_ffn",
    )(y_na, y_conv, y_ret, h, w_out, mods, g2, wg, wu, wd)


def _block_copies_start(nblk_ref, seg, src, src_blk_ref, dst, dst_blk_ref, sem):
    for e in range(N_EXPERTS):
        k = seg * N_EXPERTS + e
        sb = src_blk_ref[k]
        db = dst_blk_ref[k]
        n_big = lax.shift_right_logical(nblk_ref[k], BIG_SHIFT)
        n_small = nblk_ref[k] & (BIG_BLKS - 1)

        def copy(src_blk, dst_blk, blks):
            s0 = pl.multiple_of(src_blk * ROW_BLK, ROW_BLK)
            d0 = pl.multiple_of(dst_blk * ROW_BLK, ROW_BLK)
            rows = blks * ROW_BLK
            pltpu.make_async_copy(src.at[pl.ds(s0, rows), :], dst.at[pl.ds(d0, rows), :], sem).start()

        def big(j, carry):
            copy(sb + j * BIG_BLKS, db + j * BIG_BLKS, BIG_BLKS)
            return carry

        def small(j, carry):
            copy(sb + n_big * BIG_BLKS + j, db + n_big * BIG_BLKS + j, 1)
            return carry

        lax.fori_loop(0, n_big, big, 0)
        lax.fori_loop(0, n_small, small, 0)


def _block_copies_wait(n, src, dst, sem, blks=1):
    rows = blks * ROW_BLK

    def body(j, carry):
        pltpu.make_async_copy(src.at[pl.ds(0, rows), :], dst.at[pl.ds(0, rows), :], sem).wait()
        return carry

    lax.fori_loop(0, n, body, 0)


def _tile_copies_wait(tot_ref, seg, src, dst, sem):
    _block_copies_wait(tot_ref[2 * seg], src, dst, sem, BIG_BLKS)
    _block_copies_wait(tot_ref[2 * seg + 1], src, dst, sem)


def _dispatch_kernel(gblk, nblk, sblk, totblk, tail_blk, tail_n, d_ref, u_ref, x_hbm, y, sem, zbuf, zsem):
    t = pl.program_id(0)
    nt = pl.num_programs(0)
    slot = t % 2

    @pl.when(t >= 2)
    def _():
        _tile_copies_wait(totblk, t - 2, y.at[slot], x_hbm, sem.at[slot])

    rows = lax.broadcasted_iota(jnp.int32, (DISP_ROWS, DISP_R), 0)
    hit = jnp.where(rows == d_ref[0, 0:1, :], 1.0, jnp.where(rows == d_ref[0, 1:2, :], 1.0, 0.0))
    y[slot] = _dot(hit.astype(_bf16), u_ref[...].astype(_bf16))
    _block_copies_start(nblk, t, y.at[slot], sblk, x_hbm, gblk, sem.at[slot])

    @pl.when(t == nt - 1)
    def _():
        @pl.when(t >= 1)
        def _():
            _tile_copies_wait(totblk, jnp.maximum(t - 1, 0), y.at[1 - slot], x_hbm, sem.at[1 - slot])

        _tile_copies_wait(totblk, t, y.at[slot], x_hbm, sem.at[slot])
        zbuf[...] = jnp.zeros_like(zbuf)
        for e in range(N_EXPERTS + 1):
            tb = tail_blk[e]

            def zero_body(j, carry):
                d0 = pl.multiple_of((tb + j) * ROW_BLK, ROW_BLK)
                pltpu.make_async_copy(zbuf, x_hbm.at[pl.ds(d0, ROW_BLK), :], zsem).start()
                return carry

            lax.fori_loop(0, tail_n[e], zero_body, 0)
        for e in range(N_EXPERTS + 1):
            _block_copies_wait(tail_n[e], zbuf, x_hbm, zsem)


def _moe_dispatch(plan, u, n_rows):
    n, d = u.shape
    nt = n // DISP_R
    grid_spec = pltpu.PrefetchScalarGridSpec(
        num_scalar_prefetch=6,
        grid=(nt,),
        in_specs=[
            pl.BlockSpec((1, 2, DISP_R), lambda t, *_: (t, 0, 0)),
            pl.BlockSpec((DISP_R, d), lambda t, *_: (t, 0)),
        ],
        out_specs=pl.BlockSpec(memory_space=pl.ANY),
        scratch_shapes=[
            pltpu.VMEM((2, DISP_ROWS, d), _f32),
            pltpu.SemaphoreType.DMA((2,)),
            pltpu.VMEM((ROW_BLK, d), _f32),
            pltpu.SemaphoreType.DMA(()),
        ],
    )
    return pl.pallas_call(
        _dispatch_kernel,
        grid_spec=grid_spec,
        out_shape=jax.ShapeDtypeStruct((n_rows, d), _f32),
        compiler_params=_cparams(("arbitrary",), VMEM_LIMIT),
        name="moe_dispatch",
    )(plan["gblk"], plan["nblk"], plan["sblk"], plan["totblk"], plan["tail_blk"], plan["tail_n"], plan["d"], u)


def _moe_kernel(te_ref, nu_ref, hf_ref, x_ref, wg_hbm, wu_hbm, wd_hbm, o_ref, wg_s, wu_s, wd_s, stg_g, stg_u,
                stg_d, sem, acc):
    i = pl.program_id(0)
    n_used = nu_ref[0]
    e = te_ref[i]
    n_chunks = D_FF_EXPERT // MOE_FF_CHUNK
    first_tile = (i == 0) | (e != te_ref[jnp.maximum(i - 1, 0)])

    def chunk_copies(c):
        span = pl.ds(c * MOE_FF_CHUNK, MOE_FF_CHUNK)
        s = c % 2
        return (pltpu.make_async_copy(wg_hbm.at[e, :, span], stg_g.at[s], sem.at[0, s]),
                pltpu.make_async_copy(wu_hbm.at[e, :, span], stg_u.at[s], sem.at[1, s]),
                pltpu.make_async_copy(wd_hbm.at[e, span, :], stg_d.at[s], sem.at[2, s]))

    def tile(load_weights, rows=MOE_TM):
        x = x_ref[0:rows, :].astype(_bf16)
        if load_weights:
            for cp in chunk_copies(0):
                cp.start()
        width = MOE_FF_CHUNK if load_weights else MOE_FF_WIDE
        for c in range(D_FF_EXPERT // width):
            cols = slice(c * width, (c + 1) * width)
            if load_weights:
                if c + 1 < n_chunks:
                    for cp in chunk_copies(c + 1):
                        cp.start()
                for cp in chunk_copies(c):
                    cp.wait()
                wg_s[:, cols] = stg_g[c % 2].astype(_bf16)
                wu_s[:, cols] = stg_u[c % 2].astype(_bf16)
                wd_s[cols, :] = stg_d[c % 2].astype(_bf16)
            a = _dot(x, wg_s[:, cols])
            mid = (_silu(a) * _dot(x, wu_s[:, cols])).astype(_bf16)
            part = _dot(mid, wd_s[cols, :])
            if c == 0:
                acc[0:rows, :] = part
            else:
                acc[0:rows, :] += part
        o_ref[0:rows, :] = acc[0:rows, :]
        if rows < MOE_TM:
            o_ref[rows:, :] = jnp.zeros((MOE_TM - rows, o_ref.shape[1]), o_ref.dtype)

    @pl.when(first_tile)
    def _():
        tile(True)

    later_tile = jnp.logical_not(first_tile) & (i < n_used)
    half_filled = hf_ref[i] == 1

    @pl.when(later_tile & jnp.logical_not(half_filled))
    def _():
        tile(False)

    @pl.when(later_tile & half_filled)
    def _():
        tile(False, MOE_TM // 2)

    @pl.when(i >= n_used)
    def _():
        o_ref[...] = jnp.zeros_like(o_ref)


def _moe_experts(plan, x, wg, wu, wd):
    tile_expert, n_used = plan["tile_expert"], plan["n_used"]
    n_tiles = tile_expert.shape[0]
    d = x.shape[1]
    grid_spec = pltpu.PrefetchScalarGridSpec(
        num_scalar_prefetch=3,
        grid=(n_tiles,),
        in_specs=[
            pl.BlockSpec((MOE_TM, d), lambda i, te, nu, hf: (jnp.minimum(i, nu[0] - 1), 0)),
            pl.BlockSpec(memory_space=pl.ANY),
            pl.BlockSpec(memory_space=pl.ANY),
            pl.BlockSpec(memory_space=pl.ANY),
        ],
        out_specs=pl.BlockSpec((MOE_TM, d), lambda i, te, nu, hf: (i, 0)),
        scratch_shapes=[
            pltpu.VMEM((d, D_FF_EXPERT), _bf16),
            pltpu.VMEM((d, D_FF_EXPERT), _bf16),
            pltpu.VMEM((D_FF_EXPERT, d), _bf16),
            pltpu.VMEM((2, d, MOE_FF_CHUNK), _f32),
            pltpu.VMEM((2, d, MOE_FF_CHUNK), _f32),
            pltpu.VMEM((2, MOE_FF_CHUNK, d), _f32),
            pltpu.SemaphoreType.DMA((3, 2)),
            pltpu.VMEM((MOE_TM, d), _f32),
        ],
    )
    return pl.pallas_call(
        _moe_kernel,
        grid_spec=grid_spec,
        out_shape=jax.ShapeDtypeStruct((n_tiles * MOE_TM, d), _f32),
        compiler_params=_cparams(("arbitrary",), VMEM_LIMIT),
        name="moe_experts",
    )(tile_expert, n_used, plan["half_filled"], x, wg, wu, wd)


def _combine_kernel(gblk, nblk, sblk, totblk, route_ref, h_ref, m_ref, g_ref, y_hbm, o_ref, z, sem):
    t = pl.program_id(0)
    nt = pl.num_programs(0)
    slot = t % 2

    @pl.when(t == 0)
    def _():
        z[...] = jnp.zeros_like(z)
        _block_copies_start(nblk, t, y_hbm, gblk, z.at[0], sblk, sem.at[0])

    @pl.when(t + 1 < nt)
    def _():
        _block_copies_start(nblk, t + 1, y_hbm, gblk, z.at[1 - slot], sblk, sem.at[1 - slot])

    _tile_copies_wait(totblk, t, y_hbm, z.at[slot], sem.at[slot])
    r = route_ref[...]
    col = lax.broadcasted_iota(jnp.int32, (DISP_R, DISP_ROWS), 1).astype(_f32)
    w = jnp.where(col == r[:, 4:5], r[:, 0:1], jnp.where(col == r[:, 5:6], r[:, 1:2], 0.0))
    f = _dot(w.astype(_bf16), z[slot].astype(_bf16))
    hn = h_ref[...] + m_ref[0][5:6] * f
    o_ref[...] = hn * lax.rsqrt(jnp.mean(hn * hn, axis=-1, keepdims=True) + EPS) * g_ref[...]


def _moe_combine(plan, route, h, mods, g, y_sorted, tokens_per_batch):
    n, d = h.shape
    nt = n // DISP_R
    per_b = tokens_per_batch // DISP_R
    grid_spec = pltpu.PrefetchScalarGridSpec(
        num_scalar_prefetch=4,
        grid=(nt,),
        in_specs=[
            pl.BlockSpec((DISP_R, ROUTE_LANES), lambda t, *_: (t, 0)),
            pl.BlockSpec((DISP_R, d), lambda t, *_: (t, 0)),
            pl.BlockSpec((1, 6, d), lambda t, *_: (t // per_b, 0, 0)),
            pl.BlockSpec((1, d), lambda t, *_: (0, 0)),
            pl.BlockSpec(memory_space=pl.ANY),
        ],
        out_specs=pl.BlockSpec((DISP_R, d), lambda t, *_: (t, 0)),
        scratch_shapes=[
            pltpu.VMEM((2, DISP_ROWS, d), _f32),
            pltpu.SemaphoreType.DMA((2,)),
        ],
    )
    return pl.pallas_call(
        _combine_kernel,
        grid_spec=grid_spec,
        out_shape=jax.ShapeDtypeStruct((n, d), _f32),
        compiler_params=_cparams(("arbitrary",), VMEM_LIMIT),
        name="moe_combine",
    )(plan["gblk"], plan["nblk"], plan["sblk"], plan["totblk"], route, h, mods, g, y_sorted)


def _routing_plan(counts, rows):
    nt = counts.shape[0]
    n = nt * DISP_R
    i32 = jnp.int32
    cnt = counts[:, :, 0].astype(i32)
    cpad = (cnt + ROW_BLK - 1) // ROW_BLK * ROW_BLK
    seg = jnp.cumsum(cpad, axis=1) - cpad
    rows_e = jnp.sum(cpad, axis=0)
    tiles_e = (rows_e + MOE_TM - 1) // MOE_TM
    tile_end = jnp.cumsum(tiles_e)
    off = (tile_end - tiles_e) * MOE_TM
    glob = off[None, :] + jnp.cumsum(cpad, axis=0) - cpad
    n_tiles = -(-(2 * n + nt * N_EXPERTS * (ROW_BLK - 1)) // MOE_TM) + N_EXPERTS
    n_used = tile_end[-1]
    ti = jnp.minimum(jnp.arange(n_tiles, dtype=i32), n_used - 1)
    tile_expert = jnp.sum((ti[:, None] >= tile_end[None, :]).astype(i32), axis=1)
    nblk = cpad // ROW_BLK
    last_half = (tiles_e > 0) & (rows_e - (tiles_e - 1) * MOE_TM <= MOE_TM // 2)
    is_last = jnp.arange(n_tiles, dtype=i32)[:, None] == (tile_end - 1)[None, :]
    return {
        "half_filled": jnp.any(is_last & last_half[None, :], axis=1).astype(i32),
        "d": rows,
        "gblk": (glob // ROW_BLK).reshape(-1).astype(i32),
        "nblk": nblk.reshape(-1).astype(i32),
        "sblk": (seg // ROW_BLK).reshape(-1).astype(i32),
        "totblk": jnp.stack([jnp.sum(nblk // BIG_BLKS, axis=1), jnp.sum(nblk % BIG_BLKS, axis=1)],
                            axis=1).reshape(-1).astype(i32),
        "tail_blk": jnp.append((off + rows_e) // ROW_BLK, n_used * (MOE_TM // ROW_BLK)).astype(i32),
        "tail_n": jnp.append((tiles_e * MOE_TM - rows_e) // ROW_BLK,
                             (n_tiles - n_used) * (MOE_TM // ROW_BLK)).astype(i32),
        "tile_expert": tile_expert.astype(i32),
        "n_used": n_used.reshape(1).astype(i32),
        "n_rows": n_tiles * MOE_TM,
    }


def _permuted_w_in(w_in_l):
    d = w_in_l.shape[0]
    c = {}
    start = 0
    for name, size in (("na_qkv", 3 * NA_WIDTH), ("conv_glu", 2 * CONV_CH), ("ret_q", RET_QK_WIDTH),
                       ("ret_k", RET_QK_WIDTH), ("ret_v", RET_WIDTH), ("ret_g", 2 * RET_WIDTH)):
        c[name] = w_in_l[:, start:start + size]
        start += size
    sub = RET_QK_DIM // 2
    halves_first = lambda w: w.reshape(d, RET_HEADS, 2, sub).transpose(0, 2, 1, 3).reshape(d, RET_QK_WIDTH)
    wa = jnp.concatenate([c["na_qkv"], c["ret_v"]], axis=1)
    wb = jnp.concatenate([c["conv_glu"], halves_first(c["ret_q"]), halves_first(c["ret_k"]), c["ret_g"]], axis=1)
    return wa.astype(_bf16), wb.astype(_bf16)


def kernel(x, c, ctx, c_ctx, w_mod, b_mod, norm1_w, norm2_w, w_in, w_out, na_rpb, conv_w, conv_b, conv_ln_w,
           conv_ln_b, ret_decay, ret_gn_w, ffn_w_gate, ffn_w_up, ffn_w_down, moe_router, moe_router_b,
           moe_w_gate, moe_w_up, moe_w_down, final_norm_w):
    b, t, d = x.shape
    lc = ctx.shape[1]
    assert d == D_MODEL and t % Q_TILE == 0 and t // GRID_W >= K_ROWS and b + 1 <= MOD_ROWS
    assert t % DISP_R == 0 and lc % 256 == 0 and (2 * b * t) % MOE_TM == 0
    assert DEPTH == 2

    cvecs = jnp.zeros((MOD_ROWS, d), _f32).at[:b].set(c).at[b].set(c_ctx)
    mods = _mod_vectors(cvecs, w_mod, b_mod).reshape(DEPTH, MOD_ROWS, 6, d)
    cos_lat, sin_lat = _rope_tables(t)
    cos_ctx = jnp.ones((lc, RET_QK_WIDTH), _f32)
    sin_ctx = jnp.zeros((lc, RET_QK_WIDTH), _f32)
    vec = lambda a: a.reshape(1, -1)
    lat_tm = DISP_R
    flat = lambda a: a.reshape(1, b * lc, a.shape[-1])
    unflat = lambda a: a.reshape(b, lc, a.shape[-1])
    ctx_tm = min(DISP_R, b * lc)

    h_lat, h_ctx = x, ctx
    out = None
    for l in range(DEPTH):
        last = l == DEPTH - 1
        m_lat = mods[l, :b]
        m_ctx = mods[l, b:b + 1]
        wa, wb = _permuted_w_in(w_in[l])
        wo = w_out[l].astype(_bf16)
        gamma = 1.0 - jnp.exp2(-ret_decay[l].astype(_f32))
        lg = jnp.log(gamma)
        lgq = jnp.tile(jnp.repeat(lg, RET_QK_DIM // 2, axis=1), (1, 2)).reshape(2, 1, RET_QK_WIDTH)
        lgv = jnp.repeat(lg, RET_V_DIM, axis=1).reshape(2, 1, RET_WIDTH)

        a_lat, b_lat = _inproj(h_lat, m_lat, vec(norm1_w[l]), wa, wb, lat_tm)
        a_ctx, b_ctx = map(unflat, _inproj(flat(h_ctx), m_ctx, vec(norm1_w[l]), wa, wb, ctx_tm))

        y_na = _na_attention(a_lat, a_ctx, na_rpb[l])
        y_conv = _conv_module(b_lat, conv_w[l], conv_b[l], conv_ln_w[l], conv_ln_b[l])
        s_zero = jnp.zeros((b, 2, RET_QK_WIDTH, RET_WIDTH), _f32)
        y_ret_c, s_ctx = _retention(a_ctx, b_ctx, cos_ctx, sin_ctx, s_zero, lg, lgq, lgv, ret_gn_w[l])
        y_ret, _ = _retention(a_lat, b_lat, cos_lat, sin_lat, s_ctx, lg, lgq, lgv, ret_gn_w[l])

        if not last:
            wg = ffn_w_gate[l // 2].astype(_bf16)
            wu = ffn_w_up[l // 2].astype(_bf16)
            wd = ffn_w_down[l // 2].astype(_bf16)
            y_na_c = _ctx_attention(a_ctx)
            y_conv_c = _conv_module(b_ctx, conv_w[l], conv_b[l], conv_ln_w[l], conv_ln_b[l])
            h_ctx = unflat(_outproj_ffn(flat(y_na_c), flat(y_conv_c), flat(y_ret_c), flat(h_ctx), wo, m_ctx,
                                        vec(norm2_w[l]), wg, wu, wd, ctx_tm))
            h_lat = _outproj_ffn(y_na, y_conv, y_ret, h_lat, wo, m_lat, vec(norm2_w[l]), wg, wu, wd, lat_tm)
        else:
            rw = jnp.zeros((d, ROUTE_LANES), _f32).at[:, :N_EXPERTS].set(moe_router[l // 2])
            rb = jnp.full((1, ROUTE_LANES), NEG_INF, _f32).at[0, :N_EXPERTS].set(moe_router_b[l // 2])
            h_lat, u_lat, route, rows, counts = _outproj_route(y_na, y_conv, y_ret, h_lat, wo, m_lat,
                                                               vec(norm2_w[l]), rw, rb)
            route = route.reshape(b * t, ROUTE_LANES)
            plan = _routing_plan(counts, rows)
            x_sorted = _moe_dispatch(plan, u_lat.reshape(b * t, d), plan["n_rows"])
            y_sorted = _moe_experts(plan, x_sorted, moe_w_gate[l // 2], moe_w_up[l // 2], moe_w_down[l // 2])
            out = _moe_combine(plan, route, h_lat.reshape(b * t, d), m_lat, vec(final_norm_w), y_sorted, t)
            out = out.reshape(b, t, d)
    return out
```
